```python
import math
import jax, jax.numpy as jnp
from jax import lax
import numpy as np

D_MODEL = 1024
BATCH = 2
SEQ = 8192
DEPTH = 4
DEC_BATCH = 128
DEC_SEQ = 8
PAST_LEN = 8192
PAGE_SIZE = 128

N_MIXERS = 3
HEAD_DIM = 64
RMS_EPS = 1e-6
Q_BLOCK = 128
A_HEADS = D_MODEL // (2 * HEAD_DIM)
A_KV_HEADS = 2
A_GROUP = A_HEADS // A_KV_HEADS
A_VDIM = 2 * HEAD_DIM
A_Q = A_HEADS * 2 * HEAD_DIM
A_K = A_KV_HEADS * 2 * HEAD_DIM
A_V = A_KV_HEADS * A_VDIM
B_HEADS = D_MODEL // HEAD_DIM
B_KV_HEADS = 2
B_GROUP = B_HEADS // B_KV_HEADS
B_WINDOW = 128
B_Q = B_HEADS * HEAD_DIM
B_K = B_KV_HEADS * HEAD_DIM
B_V = B_KV_HEADS * HEAD_DIM
C_PAIRS = ((128, 1), (512, 4), (2048, 16))
C_N_GROUPS = 3
C_HEADS = D_MODEL // HEAD_DIM
C_KV_HEADS = 4
C_GROUP = C_HEADS // C_KV_HEADS
C_Q = C_N_GROUPS * C_HEADS * HEAD_DIM
C_K = C_N_GROUPS * C_KV_HEADS * HEAD_DIM
C_V = C_N_GROUPS * C_KV_HEADS * HEAD_DIM
N_BUCKETS = 32
MAX_DISTANCE = 2048
REL_COLS = 16
D_FF = ((8 * D_MODEL // 3 + 255) // 256) * 256
CONV_W = 3
N_LAYERS_A = (DEPTH + 2) // 3
N_LAYERS_B = (DEPTH + 1) // 3
N_LAYERS_C = DEPTH // 3

kernel_name = "hybrid_diff_swa_dilated_decoder_step"


def rms_norm(x, gain):
    xf = x.astype(jnp.float32)
    xf = xf * lax.rsqrt(jnp.mean(xf * xf, axis=-1, keepdims=True) + RMS_EPS)
    return (xf * gain.astype(jnp.float32)).astype(x.dtype)


def t5_bucket(dist):
    n = jnp.maximum(dist, 0)
    max_exact = N_BUCKETS // 2
    nf = jnp.maximum(n, 1).astype(jnp.float32)
    large = max_exact + (jnp.log(nf / max_exact) / math.log(MAX_DISTANCE / max_exact)
                         * (N_BUCKETS - max_exact)).astype(jnp.int32)
    return jnp.where(n < max_exact, n, jnp.minimum(large, N_BUCKETS - 1))


def rel_bias(dist, table):
    return jnp.take(table, t5_bucket(dist), axis=0).astype(jnp.float32)


def a_project(x, w_in, q_gain, k_gain):
    b, t, _ = x.shape
    q, k, v = jnp.split(x @ w_in, [A_Q, A_Q + A_K], axis=-1)
    q = rms_norm(q.reshape(b, t, A_KV_HEADS, A_GROUP, 2, HEAD_DIM), q_gain)
    k = rms_norm(k.reshape(b, t, A_KV_HEADS, 2, HEAD_DIM), k_gain)
    return q, k, v.reshape(b, t, A_KV_HEADS, A_VDIM)


def diff_lambda(lam_p, lam_init):
    lp = lam_p.astype(jnp.float32)
    return jnp.exp(jnp.sum(lp[0] * lp[1])) - jnp.exp(jnp.sum(lp[2] * lp[3])) + lam_init


def diff_attend(q, k, v, dist, lam, table):
    tq, tk = dist.shape
    s = jnp.einsum('bqkgmd,bskmd->bkgmqs', q, k).astype(jnp.float32) * HEAD_DIM ** -0.5
    bias = rel_bias(dist, table).reshape(tq, tk, A_KV_HEADS, A_GROUP, 2)
    s = jnp.where(dist >= 0, s + jnp.transpose(bias, (2, 3, 4, 0, 1)), -jnp.inf)
    p = jax.nn.softmax(s, axis=-1)
    a = p[:, :, :, 0] - lam * p[:, :, :, 1]
    return jnp.einsum('bkgqs,bskv->bqkgv', a.astype(v.dtype), v)


def a_prompt(q, k, v, lam, table):
    b, t = q.shape[:2]
    nb = t // Q_BLOCK
    qb = q.reshape((b, nb, Q_BLOCK) + q.shape[2:]).swapaxes(0, 1)
    kpos = jnp.arange(t)

    def block(args):
        qi, bi = args
        dist = (bi * Q_BLOCK + jnp.arange(Q_BLOCK))[:, None] - kpos[None, :]
        return diff_attend(qi, k, v, dist, lam, table)

    o = lax.map(block, (qb, jnp.arange(nb)))
    return o.swapaxes(0, 1).reshape(b, t, A_KV_HEADS, A_GROUP, A_VDIM)


def a_sample(q, new_kv, past, lam, table):
    db, tq = q.shape[:2]
    p_len = past.shape[1] * past.shape[2]
    kv = jnp.concatenate([past.reshape((db, p_len) + past.shape[3:]), new_kv], axis=1)
    tk = p_len + tq
    kk = kv[:, :, 0].reshape(db, tk, A_KV_HEADS, 2, HEAD_DIM)
    dist = (p_len + jnp.arange(tq))[:, None] - jnp.arange(tk)[None, :]
    return diff_attend(q, kk, kv[:, :, 1], dist, lam, table)


def a_output(o, lam_init, subln, w_o):
    b, t = o.shape[:2]
    o = rms_norm(o, subln) * (1.0 - lam_init)
    return o.reshape(b, t, A_HEADS * A_VDIM) @ w_o


def b_project(x, w_in, q_gain, k_gain):
    b, t, _ = x.shape
    q, k, v = jnp.split(x @ w_in, [B_Q, B_Q + B_K], axis=-1)
    q = rms_norm(q.reshape(b, t, B_KV_HEADS, B_GROUP, HEAD_DIM), q_gain)
    k = rms_norm(k.reshape(b, t, B_KV_HEADS, HEAD_DIM), k_gain)
    return q, k, v.reshape(b, t, B_KV_HEADS, HEAD_DIM)


def sink_attend(q, k, v, dist, valid, sinks, table):
    tq, tk = dist.shape
    s = jnp.einsum('...qkgd,...skd->...kgqs', q, k).astype(jnp.float32) * HEAD_DIM ** -0.5
    bias = rel_bias(dist, table).reshape(tq, tk, B_KV_HEADS, B_GROUP)
    s = jnp.where(valid, s + jnp.transpose(bias, (2, 3, 0, 1)), -jnp.inf)
    sink = jnp.broadcast_to(sinks.astype(jnp.float32).reshape(B_KV_HEADS, B_GROUP, 1, 1),
                            s.shape[:-1] + (1,))
    p = jax.nn.softmax(jnp.concatenate([s, sink], axis=-1), axis=-1)[..., :tk]
    return jnp.einsum('...kgqs,...skd->...qkgd', p.astype(v.dtype), v)


def b_prompt(q, k, v, sinks, table):
    b, t = q.shape[:2]
    w = B_WINDOW
    nb = t // w

    def band(z):
        zb = z.reshape((b, nb, w) + z.shape[2:])
        prev = jnp.pad(zb[:, :-1], ((0, 0), (1, 0)) + ((0, 0),) * (zb.ndim - 2))
        return jnp.concatenate([prev, zb], axis=2)

    li = jnp.arange(w)[:, None]
    lj = jnp.arange(2 * w)[None, :]
    dist = w + li - lj
    kpos = (jnp.arange(nb) * w - w)[:, None, None] + lj[None]
    valid = ((dist >= 0) & (dist <= B_WINDOW))[None] & (kpos >= 0)
    qb = q.reshape((b, nb, w) + q.shape[2:])
    o = sink_attend(qb, band(k), band(v), dist, valid[:, None, None], sinks, table)
    return o.reshape(b, t, B_HEADS * HEAD_DIM)


def b_sample(q, new_kv, buf, sinks, table):
    db, tq = q.shape[:2]
    lb = buf.shape[1]
    kv = jnp.concatenate([buf, new_kv], axis=1)
    dist = lb + jnp.arange(tq)[:, None] - jnp.arange(lb + tq)[None, :]
    valid = (dist >= 0) & (dist <= B_WINDOW)
    o = sink_attend(q, kv[:, :, 0], kv[:, :, 1], dist, valid, sinks, table)
    return o.reshape(db, tq, B_HEADS * HEAD_DIM), kv[:, tq:]


def c_project(x, w_in, q_gain, k_gain):
    b, t, _ = x.shape
    q, k, v = jnp.split(x @ w_in, [C_Q, C_Q + C_K], axis=-1)
    q = rms_norm(q.reshape(b, t, C_N_GROUPS, C_KV_HEADS, C_GROUP, HEAD_DIM), q_gain)
    k = rms_norm(k.reshape(b, t, C_N_GROUPS, C_KV_HEADS, HEAD_DIM), k_gain)
    return q, k, v.reshape(b, t, C_N_GROUPS, C_KV_HEADS, HEAD_DIM)


def dilated_attend(q, k, v, offs, valid, table):
    s = jnp.einsum('bqkgd,bqjkd->bqkgj', q, k).astype(jnp.float32) * HEAD_DIM ** -0.5
    bias = rel_bias(offs, table).reshape(-1, C_KV_HEADS, C_GROUP)
    s = jnp.where(valid[None, :, None, None, :], s + jnp.transpose(bias, (1, 2, 0)), -jnp.inf)
    lse = jax.nn.logsumexp(s, axis=-1, keepdims=True)
    o = jnp.einsum('bqkgj,bqjkd->bqkgd', jnp.exp(s - lse).astype(v.dtype), v)
    return o, lse[..., 0]


def combine_groups(outs, lses):
    wts = jax.nn.softmax(jnp.stack(lses), axis=0)
    o = jnp.sum(wts[..., None] * jnp.stack(outs).astype(jnp.float32), axis=0)
    return o.astype(outs[0].dtype)


def c_prompt(q, k, v, table):
    b, t = q.shape[:2]
    nb = t // Q_BLOCK
    qb = q.reshape((b, nb, Q_BLOCK) + q.shape[2:]).swapaxes(0, 1)
    pads = [(jnp.pad(k[:, :, g], ((0, 0), (w, 0), (0, 0), (0, 0))),
             jnp.pad(v[:, :, g], ((0, 0), (w, 0), (0, 0), (0, 0))))
            for g, (w, d) in enumerate(C_PAIRS)]
    li = jnp.arange(Q_BLOCK)[:, None]

    def block(args):
        qi, bi = args
        outs, lses = [], []
        for g, (w, d) in enumerate(C_PAIRS):
            offs = jnp.arange(w // d + 1) * d
            ks = lax.dynamic_slice_in_dim(pads[g][0], bi * Q_BLOCK, Q_BLOCK + w, axis=1)
            vs = lax.dynamic_slice_in_dim(pads[g][1], bi * Q_BLOCK, Q_BLOCK + w, axis=1)
            idx = li + w - offs[None, :]
            valid = bi * Q_BLOCK + li - offs[None, :] >= 0
            o, lse = dilated_attend(qi[:, :, g], ks[:, idx], vs[:, idx], offs, valid, table)
            outs.append(o)
            lses.append(lse)
        return combine_groups(outs, lses)

    o = lax.map(block, (qb, jnp.arange(nb)))
    return o.swapaxes(0, 1).reshape(b, t, C_HEADS * HEAD_DIM)


def c_sample(q, k, v, bufs, table):
    db, tq = q.shape[:2]
    outs, lses, new_bufs = [], [], []
    for g, (w, d) in enumerate(C_PAIRS):
        offs = jnp.arange(w // d + 1) * d
        lb = bufs[g].shape[1]
        kv = jnp.concatenate([bufs[g], jnp.stack([k[:, :, g], v[:, :, g]], axis=2)], axis=1)
        idx = lb + jnp.arange(tq)[:, None] - offs[None, :]
        kvg = kv[:, jnp.maximum(idx, 0)]
        o, lse = dilated_attend(q[:, :, g], kvg[:, :, :, 0], kvg[:, :, :, 1], offs, idx >= 0, table)
        outs.append(o)
        lses.append(lse)
        new_bufs.append(kv[:, tq:])
    return combine_groups(outs, lses).reshape(db, tq, C_HEADS * HEAD_DIM), new_bufs


def conv_ffn(x, hist, w_in, conv_w, conv_b, w_out):
    t = x.shape[1]
    gate, up = jnp.split(x @ w_in, 2, axis=-1)
    gp = jnp.concatenate([hist.astype(gate.dtype), gate], axis=1)
    gc = conv_b + conv_w[0] * gp[:, 0:t]
    for j in range(1, CONV_W):
        gc = gc + conv_w[j] * gp[:, j:j + t]
    return (jax.nn.silu(gc) * up) @ w_out, gp[:, t:]


def setup_inputs(seed: int = 0) -> dict:
    key = jax.random.key(seed)
    ks = iter(jax.random.split(key, 40))

    def nrm(shape, scale=1.0):
        return jax.random.normal(next(ks), shape, jnp.float32) * scale

    def gain(shape):
        return 1.0 + nrm(shape, 0.02)

    n_pages = PAST_LEN // PAGE_SIZE
    n_used = DEC_BATCH * n_pages
    n_pool = n_used + n_used // 4
    page_table = jax.random.permutation(next(ks), n_pool)[:n_used].reshape(DEC_BATCH, n_pages).astype(jnp.int32)
    D = D_MODEL
    return {
        "x_prompt": nrm((BATCH, SEQ, D)),
        "x_sample": nrm((DEC_BATCH, DEC_SEQ, D)),
        "cache_kv_a": nrm((N_LAYERS_A, n_pool, PAGE_SIZE, 2, A_KV_HEADS, A_VDIM)),
        "state_kv_b": nrm((N_LAYERS_B, DEC_BATCH, min(B_WINDOW, PAST_LEN), 2, B_KV_HEADS, HEAD_DIM)),
        "state_kv_c1": nrm((N_LAYERS_C, DEC_BATCH, min(C_PAIRS[0][0], PAST_LEN), 2, C_KV_HEADS, HEAD_DIM)),
        "state_kv_c2": nrm((N_LAYERS_C, DEC_BATCH, min(C_PAIRS[1][0], PAST_LEN), 2, C_KV_HEADS, HEAD_DIM)),
        "state_kv_c3": nrm((N_LAYERS_C, DEC_BATCH, min(C_PAIRS[2][0], PAST_LEN), 2, C_KV_HEADS, HEAD_DIM)),
        "state_conv_ffn": nrm((DEPTH, DEC_BATCH, CONV_W - 1, D_FF)),
        "page_table": page_table,
        "rel_bias_table": nrm((N_BUCKETS, REL_COLS), 0.5),
        "norm_mix": gain((DEPTH, D)),
        "norm_ffn": gain((DEPTH, D)),
        "w_in_a": nrm((N_LAYERS_A, D, A_Q + A_K + A_V), D ** -0.5),
        "q_norm_a": gain((N_LAYERS_A, HEAD_DIM)),
        "k_norm_a": gain((N_LAYERS_A, HEAD_DIM)),
        "lambda_a": nrm((N_LAYERS_A, 4, HEAD_DIM), 0.1),
        "subln_a": gain((N_LAYERS_A, A_VDIM)),
        "w_o_a": nrm((N_LAYERS_A, A_HEADS * A_VDIM, D), (A_HEADS * A_VDIM) ** -0.5),
        "w_in_b": nrm((N_LAYERS_B, D, B_Q + B_K + B_V), D ** -0.5),
        "q_norm_b": gain((N_LAYERS_B, HEAD_DIM)),
        "k_norm_b": gain((N_LAYERS_B, HEAD_DIM)),
        "sinks_b": nrm((N_LAYERS_B, B_HEADS), 0.5),
        "w_o_b": nrm((N_LAYERS_B, B_HEADS * HEAD_DIM, D), (B_HEADS * HEAD_DIM) ** -0.5),
        "w_in_c": nrm((N_LAYERS_C, D, C_Q + C_K + C_V), D ** -0.5),
        "q_norm_c": gain((N_LAYERS_C, HEAD_DIM)),
        "k_norm_c": gain((N_LAYERS_C, HEAD_DIM)),
        "w_o_c": nrm((N_LAYERS_C, C_HEADS * HEAD_DIM, D), (C_HEADS * HEAD_DIM) ** -0.5),
        "w_ffn_in": nrm((DEPTH, D, 2 * D_FF), D ** -0.5),
        "conv_ffn_w": nrm((DEPTH, CONV_W, D_FF), CONV_W ** -0.5),
        "conv_ffn_b": nrm((DEPTH, D_FF), 0.02),
        "w_ffn_out": nrm((DEPTH, D_FF, D), D_FF ** -0.5),
    }


def reference(x_prompt, x_sample, cache_kv_a, state_kv_b, state_kv_c1, state_kv_c2, state_kv_c3,
              state_conv_ffn, page_table, rel_bias_table, norm_mix, norm_ffn,
              w_in_a, q_norm_a, k_norm_a, lambda_a, subln_a, w_o_a,
              w_in_b, q_norm_b, k_norm_b, sinks_b, w_o_b,
              w_in_c, q_norm_c, k_norm_c, w_o_c,
              w_ffn_in, conv_ffn_w, conv_ffn_b, w_ffn_out):
    xp, xs = x_prompt, x_sample
    bp, tp = xp.shape[:2]
    bs, ts = xs.shape[:2]
    kv_a_p, kv_a_s, kv_b_p, kv_b_s = [], [], [], []
    kv_c_p = [[] for _ in C_PAIRS]
    kv_c_s = [[] for _ in C_PAIRS]
    conv_p, conv_s = [], []
    state_c = (state_kv_c1, state_kv_c2, state_kv_c3)
    for i in range(DEPTH):
        n = i // N_MIXERS
        hp = rms_norm(xp, norm_mix[i])
        hs = rms_norm(xs, norm_mix[i])
        if i % N_MIXERS == 0:
            lam_init = 0.8 - 0.6 * math.exp(-0.3 * i)
            lam = diff_lambda(lambda_a[n], lam_init)
            qp, kp, vp = a_project(hp, w_in_a[n], q_norm_a[n], k_norm_a[n])
            qs, ks_, vs = a_project(hs, w_in_a[n], q_norm_a[n], k_norm_a[n])
            new_p = jnp.stack([kp.reshape(bp, tp, A_KV_HEADS, 2 * HEAD_DIM), vp], axis=2)
            new_s = jnp.stack([ks_.reshape(bs, ts, A_KV_HEADS, 2 * HEAD_DIM), vs], axis=2)
            op = a_prompt(qp, kp, vp, lam, rel_bias_table)
            os_ = a_sample(qs, new_s, cache_kv_a[n, page_table], lam, rel_bias_table)
            mp = a_output(op, lam_init, subln_a[n], w_o_a[n])
            ms = a_output(os_, lam_init, subln_a[n], w_o_a[n])
            kv_a_p.append(new_p)
            kv_a_s.append(new_s)
        elif i % N_MIXERS == 1:
            qp, kp, vp = b_project(hp, w_in_b[n], q_norm_b[n], k_norm_b[n])
            qs, ks_, vs = b_project(hs, w_in_b[n], q_norm_b[n], k_norm_b[n])
            op = b_prompt(qp, kp, vp, sinks_b[n], rel_bias_table)
            os_, buf_s = b_sample(qs, jnp.stack([ks_, vs], axis=2), state_kv_b[n], sinks_b[n], rel_bias_table)
            mp = op @ w_o_b[n]
            ms = os_ @ w_o_b[n]
            kv_b_p.append(jnp.stack([kp, vp], axis=2)[:, -min(B_WINDOW, tp):])
            kv_b_s.append(buf_s)
        else:
            qp, kp, vp = c_project(hp, w_in_c[n], q_norm_c[n], k_norm_c[n])
            qs, ks_, vs = c_project(hs, w_in_c[n], q_norm_c[n], k_norm_c[n])
            op = c_prompt(qp, kp, vp, rel_bias_table)
            os_, bufs_s = c_sample(qs, ks_, vs, [s[n] for s in state_c], rel_bias_table)
            mp = op @ w_o_c[n]
            ms = os_ @ w_o_c[n]
            for g, (w, d) in enumerate(C_PAIRS):
                kv_c_p[g].append(jnp.stack([kp[:, :, g], vp[:, :, g]], axis=2)[:, -min(w, tp):])
                kv_c_s[g].append(bufs_s[g])
        xp = xp + mp
        xs = xs + ms
        yp, cp = conv_ffn(rms_norm(xp, norm_ffn[i]), jnp.zeros((bp, CONV_W - 1, D_FF), xp.dtype),
                          w_ffn_in[i], conv_ffn_w[i], conv_ffn_b[i], w_ffn_out[i])
        ys, cs = conv_ffn(rms_norm(xs, norm_ffn[i]), state_conv_ffn[i],
                          w_ffn_in[i], conv_ffn_w[i], conv_ffn_b[i], w_ffn_out[i])
        xp = xp + yp
        xs = xs + ys
        conv_p.append(cp)
        conv_s.append(cs)
    return (xp, xs,
            jnp.stack(kv_a_p), jnp.stack(kv_a_s),
            jnp.stack(kv_b_p), jnp.stack(kv_b_s),
            jnp.stack(kv_c_p[0]), jnp.stack(kv_c_s[0]),
            jnp.stack(kv_c_p[1]), jnp.stack(kv_c_s[1]),
            jnp.stack(kv_c_p[2]), jnp.stack(kv_c_s[2]),
            jnp.stack(conv_p), jnp.stack(conv_s))
```

```python
import functools
import math

import numpy as np
import jax
import jax.numpy as jnp
from jax import lax
from jax.experimental import pallas as pl
from jax.experimental.pallas import tpu as pltpu

F32 = jnp.float32
BF16 = jnp.bfloat16

D_MODEL = 1024
HEAD_DIM = 64
LANES = 128
SUBLANES = 8
RMS_EPS = 1e-6
NEG = -1e30
N_HEADS = 16
N_BUCKETS = 32
MAX_DISTANCE = 2048
PAGE_SIZE = 128
D_FF = 2816
CONV_W = 3
N_MIXERS = 3
A_KV = 2
B_KV = 2
C_KV = 4
B_WINDOW = 128
C_PAIRS = ((128, 1), (512, 4), (2048, 16))
BAND = 128
VMEM_LIMIT = 56 * 1024 * 1024


def _bucket_thresholds():
    n = np.arange(0, 1 << 15)
    x = np.log(np.maximum(n, 1) / (N_BUCKETS // 2)) / math.log(MAX_DISTANCE / (N_BUCKETS // 2))
    large = N_BUCKETS // 2 + (x * (N_BUCKETS - N_BUCKETS // 2)).astype(np.int64)
    bucket = np.where(n < N_BUCKETS // 2, n, np.minimum(large, N_BUCKETS - 1))
    return [int(np.argmax(bucket >= b)) for b in range(N_BUCKETS)]


THR = _bucket_thresholds()
N_TOEP = -(-(THR[-1] + BAND - 1) // BAND)
TOEP_TILES = N_TOEP + 2


def _cparams(sem, vmem=VMEM_LIMIT):
    return pltpu.CompilerParams(dimension_semantics=sem, vmem_limit_bytes=vmem)


def _dot(a, b):
    return jnp.dot(a, b, preferred_element_type=F32)


def _dot_nt(a, b):
    return lax.dot_general(a, b, (((1,), (1,)), ((), ())), preferred_element_type=F32)


def _low_half():
    return lax.broadcasted_iota(jnp.int32, (1, LANES), 1) < HEAD_DIM


def _bias_of(dist, col, tab_ref, lo, hi):
    b_lo = max(b for b in range(N_BUCKETS) if THR[b] <= max(lo, 0))
    b_hi = max(b for b in range(N_BUCKETS) if THR[b] <= max(hi, 0))
    val = jnp.full(dist.shape, tab_ref[b_hi, col], F32)
    for b in range(b_hi - 1, b_lo - 1, -1):
        val = jnp.where(dist < THR[b + 1], tab_ref[b, col], val)
    return val


def _toep_kernel(tab_ref, o_ref):
    t = pl.program_id(0)
    row = lax.broadcasted_iota(jnp.int32, (BAND, BAND), 0)
    col = lax.broadcasted_iota(jnp.int32, (BAND, BAND), 1)
    for delta in range(-1, N_TOEP + 1):
        @pl.when(t == delta + 1)
        def _(delta=delta):
            for c in range(N_HEADS):
                if delta < 0:
                    o_ref[c] = jnp.full((BAND, BAND), NEG, F32)
                elif delta == N_TOEP:
                    o_ref[c] = jnp.full((BAND, BAND), tab_ref[N_BUCKETS - 1, c], F32)
                else:
                    dist = delta * BAND + row - col
                    val = _bias_of(dist, c, tab_ref, delta * BAND - BAND + 1, delta * BAND + BAND - 1)
                    o_ref[c] = jnp.where(dist >= 0, val, NEG)


def _toep_table(table):
    return pl.pallas_call(
        _toep_kernel,
        grid=(TOEP_TILES,),
        in_specs=[pl.BlockSpec(memory_space=pltpu.SMEM)],
        out_specs=pl.BlockSpec((None, N_HEADS, BAND, BAND), lambda t: (t, 0, 0, 0)),
        out_shape=jax.ShapeDtypeStruct((TOEP_TILES, N_HEADS, BAND, BAND), F32),
        compiler_params=_cparams(("arbitrary",)),
        name="bias_toeplitz",
    )(table)


def _band_bias_kernel(tab_ref, o_ref, *, dil):
    row = lax.broadcasted_iota(jnp.int32, (BAND, 2 * BAND), 0)
    col = lax.broadcasted_iota(jnp.int32, (BAND, 2 * BAND), 1)
    steps = BAND + row - col
    valid = (steps >= 0) & (steps <= BAND)
    for c in range(N_HEADS):
        val = _bias_of(steps * dil, c, tab_ref, 0, BAND * dil)
        o_ref[c] = jnp.where(valid, val, NEG)


def _band_bias_table(table, dil):
    return pl.pallas_call(
        functools.partial(_band_bias_kernel, dil=dil),
        in_specs=[pl.BlockSpec(memory_space=pltpu.SMEM)],
        out_shape=jax.ShapeDtypeStruct((N_HEADS, BAND, 2 * BAND), F32),
        compiler_params=_cparams(None),
        name=f"bias_band_d{dil}",
    )(table)


def _decode_bias_kernel(tab_ref, past_ref, new_ref, *, cols, past_len, window, dil, tq):
    n_grp = len(cols)
    t_p = lax.broadcasted_iota(jnp.int32, (tq, past_len), 0)
    i_p = lax.broadcasted_iota(jnp.int32, (tq, past_len), 1)
    d_p = past_len + t_p - i_p
    ok_p = d_p <= window
    if dil > 1:
        ok_p = ok_p & ((d_p & (dil - 1)) == 0)
    t_n = lax.broadcasted_iota(jnp.int32, (tq, LANES), 0)
    i_n = lax.broadcasted_iota(jnp.int32, (tq, LANES), 1)
    d_n = t_n - i_n
    ok_n = (d_n >= 0) & (i_n < tq)
    if dil > 1:
        ok_n = ok_n & ((d_n & (dil - 1)) == 0)
    for r in range(n_grp):
        c = cols[r]
        vp = _bias_of(d_p, c, tab_ref, 1, min(window, past_len + tq - 1))
        past_ref[r * tq:(r + 1) * tq, :] = jnp.where(ok_p, vp, NEG)
        vn = _bias_of(d_n, c, tab_ref, 0, tq - 1)
        new_ref[r * tq:(r + 1) * tq, :] = jnp.where(ok_n, vn, NEG)


def _decode_bias_table(table, cols, past_len, window, dil, tq):
    outs = [pl.pallas_call(
        functools.partial(_decode_bias_kernel, cols=tuple(cc), past_len=past_len, window=window,
                          dil=dil, tq=tq),
        in_specs=[pl.BlockSpec(memory_space=pltpu.SMEM)],
        out_shape=(jax.ShapeDtypeStruct((len(cc) * tq, past_len), F32),
                   jax.ShapeDtypeStruct((len(cc) * tq, LANES), F32)),
        compiler_params=_cparams(None),
        name=f"bias_decode_L{past_len}_d{dil}",
    )(table) for cc in cols]
    return jnp.stack([o[0] for o in outs]), jnp.stack([o[1] for o in outs])


def _proj_kernel(x_ref, g_ref, w_ref, cg_ref, *out_refs, plan, col_chunk):
    x = x_ref[...]
    h = x * lax.rsqrt(jnp.mean(x * x, axis=-1, keepdims=True) + RMS_EPS) * g_ref[...]
    hb = h.astype(BF16)
    low = _low_half()
    n_total = w_ref.shape[1]
    for c0 in range(0, n_total, col_chunk):
        cw = min(col_chunk, n_total - c0)
        y = _dot(hb, w_ref[:, c0:c0 + cw])
        for j in range(cw // LANES):
            src = c0 + j * LANES
            normed, dsts = plan[src // LANES]
            blk = y[:, j * LANES:(j + 1) * LANES]
            if normed:
                sq = blk * blk
                ms_lo = jnp.sum(jnp.where(low, sq, 0.0), axis=-1, keepdims=True) * (1.0 / HEAD_DIM)
                ms_hi = jnp.sum(jnp.where(low, 0.0, sq), axis=-1, keepdims=True) * (1.0 / HEAD_DIM)
                inv = jnp.where(low, lax.rsqrt(ms_lo + RMS_EPS), lax.rsqrt(ms_hi + RMS_EPS))
                blk = blk * inv * cg_ref[:, src:src + LANES]
            for oi, dst in dsts:
                out_refs[oi][:, dst:dst + LANES] = blk.astype(out_refs[oi].dtype)


class _ProjLayout:
    def __init__(self):
        self.src = []
        self.gain = []
        self.plan = []
        self.outs = []

    def add_output(self, width, dtype):
        self.outs.append((width, dtype))
        return len(self.outs) - 1

    def add_chunk(self, src_cols, kind, dsts):
        assert len(src_cols) == LANES
        self.src.extend(src_cols)
        self.gain.extend([kind] * LANES)
        self.plan.append((kind != 0, list(dsts)))


def _cols(start, n):
    return list(range(start, start + n))


def _zeros(n):
    return [-1] * n


def _run_proj(x, norm_gain, w_in, q_gain, k_gain, layout, row_tile, col_chunk=1024):
    rows = x.shape[0]
    src = np.asarray(layout.src)
    w_ext = jnp.where(jnp.asarray(src >= 0)[None, :], jnp.take(w_in, jnp.asarray(np.maximum(src, 0)), axis=1), 0.0)
    w_ext = w_ext.astype(BF16)
    kind = np.asarray(layout.gain)
    n_ext = len(src)
    reps = n_ext // HEAD_DIM
    cg = jnp.where(jnp.asarray(kind == 1), jnp.tile(q_gain, reps) * (HEAD_DIM ** -0.5),
                   jnp.where(jnp.asarray(kind == 2), jnp.tile(k_gain, reps), 1.0)).reshape(1, n_ext)
    out_shape = tuple(jax.ShapeDtypeStruct((rows, w), dt) for w, dt in layout.outs)
    out_specs = tuple(pl.BlockSpec((row_tile, w), lambda i: (i, 0)) for w, _ in layout.outs)
    return pl.pallas_call(
        functools.partial(_proj_kernel, plan=tuple(layout.plan), col_chunk=col_chunk),
        grid=(rows // row_tile,),
        in_specs=[pl.BlockSpec((row_tile, D_MODEL), lambda i: (i, 0)),
                  pl.BlockSpec((1, D_MODEL), lambda i: (0, 0)),
                  pl.BlockSpec((D_MODEL, n_ext), lambda i: (0, 0), pipeline_mode=pl.Buffered(1)),
                  pl.BlockSpec((1, n_ext), lambda i: (0, 0))],
        out_specs=out_specs,
        out_shape=out_shape,
        compiler_params=_cparams(("parallel",)),
        name="proj",
    )(x, norm_gain.reshape(1, D_MODEL), w_ext, cg)


def _layout_a(q_dtype):
    lay = _ProjLayout()
    oq = lay.add_output(1024, q_dtype)
    okv = lay.add_output(512, F32)
    ok = lay.add_output(256, BF16)
    ov = lay.add_output(256, BF16)
    for c in range(8):
        lay.add_chunk(_cols(c * LANES, LANES), 1, [(oq, c * LANES)])
    for c in range(2):
        lay.add_chunk(_cols(1024 + c * LANES, LANES), 2, [(okv, c * LANES), (ok, c * LANES)])
    for c in range(2):
        lay.add_chunk(_cols(1280 + c * LANES, LANES), 0, [(okv, 256 + c * LANES), (ov, c * LANES)])
    return lay


def _layout_band_prompt(n_grp, n_kv):
    nq = n_grp * N_HEADS * HEAD_DIM
    nk = n_grp * n_kv * HEAD_DIM
    lay = _ProjLayout()
    oq = lay.add_output(nq, BF16)
    okv = lay.add_output(2 * nk, F32)
    okd = lay.add_output(2 * nk, BF16)
    ovd = lay.add_output(2 * nk, BF16)
    for c in range(nq // LANES):
        lay.add_chunk(_cols(c * LANES, LANES), 1, [(oq, c * LANES)])
    gk = n_kv * HEAD_DIM
    for g in range(n_grp):
        for c in range(gk // LANES):
            lay.add_chunk(_cols(nq + g * gk + c * LANES, LANES), 2, [(okv, g * 2 * gk + c * LANES)])
        for c in range(gk // LANES):
            lay.add_chunk(_cols(nq + nk + g * gk + c * LANES, LANES), 0, [(okv, g * 2 * gk + gk + c * LANES)])
    for g in range(n_grp):
        for kv in range(n_kv):
            base = nq + g * gk + kv * HEAD_DIM
            lay.add_chunk(_cols(base, HEAD_DIM) * 2, 2, [(okd, (g * n_kv + kv) * LANES)])
    for g in range(n_grp):
        for kv in range(n_kv):
            base = nq + nk + g * gk + kv * HEAD_DIM
            lay.add_chunk(_cols(base, HEAD_DIM) * 2, 0, [(ovd, (g * n_kv + kv) * LANES)])
    return lay


def _layout_band_sample(n_grp, n_kv):
    nq = n_grp * N_HEADS * HEAD_DIM
    nk = n_grp * n_kv * HEAD_DIM
    grp_heads = N_HEADS // n_kv
    lay = _ProjLayout()
    oq = lay.add_output(n_grp * N_HEADS * LANES, F32)
    okv = lay.add_output(2 * nk, F32)
    for g in range(n_grp):
        for h in range(N_HEADS):
            qc = _cols((g * N_HEADS + h) * HEAD_DIM, HEAD_DIM)
            odd = (h // grp_heads) % 2
            lay.add_chunk(_zeros(HEAD_DIM) + qc if odd else qc + _zeros(HEAD_DIM), 1,
                          [(oq, (g * N_HEADS + h) * LANES)])
    gk = n_kv * HEAD_DIM
    for g in range(n_grp):
        for c in range(gk // LANES):
            lay.add_chunk(_cols(nq + g * gk + c * LANES, LANES), 2, [(okv, g * 2 * gk + c * LANES)])
        for c in range(gk // LANES):
            lay.add_chunk(_cols(nq + nk + g * gk + c * LANES, LANES), 0, [(okv, g * 2 * gk + gk + c * LANES)])
    return lay


def _diff_finalize(acc0, l0, acc1, l1, lam_ref, subln_ref, lam_init):
    lp = lam_ref[...]
    lam = (jnp.exp(jnp.sum(lp[0:1] * lp[1:2], axis=-1, keepdims=True))
           - jnp.exp(jnp.sum(lp[2:3] * lp[3:4], axis=-1, keepdims=True)) + lam_init)
    o = acc0 / l0 - lam * (acc1 / l1)
    o = o * lax.rsqrt(jnp.mean(o * o, axis=-1, keepdims=True) + RMS_EPS) * subln_ref[...]
    return o * (1.0 - lam_init)


def _a_prompt_kernel(q_ref, k_ref, v_ref, toep_ref, lam_ref, subln_ref, o_ref,
                     qm_scr, p_scr, m_scr, l_scr, acc_scr, *, tq, tk, lam_init):
    qi = pl.program_id(2)
    n_sub = tk // BAND
    low = _low_half()
    for g in range(4):
        qg = q_ref[:, g * LANES:(g + 1) * LANES]
        qm_scr[(2 * g) * tq:(2 * g + 1) * tq, :] = jnp.where(low, qg, jnp.zeros_like(qg))
        qm_scr[(2 * g + 1) * tq:(2 * g + 2) * tq, :] = jnp.where(low, jnp.zeros_like(qg), qg)
    m_scr[...] = jnp.full(m_scr.shape, NEG, F32)
    l_scr[...] = jnp.zeros(l_scr.shape, F32)
    acc_scr[...] = jnp.zeros(acc_scr.shape, F32)
    n_chunks = (qi * tq + tq - 1) // tk + 1

    def body(kj, carry):
        k0 = pl.multiple_of(kj * tk, tk)
        kc = k_ref[pl.ds(k0, tk), :]
        vc = v_ref[pl.ds(k0, tk), :]
        s_all = _dot_nt(qm_scr[...], kc)
        for h in range(8):
            rows = []
            for r in range(tq // BAND):
                tiles = []
                for u in range(n_sub):
                    delta = (qi * (tq // BAND) + r) - (kj * n_sub + u)
                    tiles.append(toep_ref[jnp.clip(delta + 1, 0, N_TOEP + 1), h])
                rows.append(jnp.concatenate(tiles, axis=1) if n_sub > 1 else tiles[0])
            bias = jnp.concatenate(rows, axis=0) if len(rows) > 1 else rows[0]
            s = s_all[h * tq:(h + 1) * tq, :] + bias
            m_prev = m_scr[h]
            m_new = jnp.maximum(m_prev, jnp.max(s, axis=-1, keepdims=True))
            alpha = jnp.exp(m_prev - m_new)
            p = jnp.exp(s - m_new)
            l_scr[h] = alpha * l_scr[h] + jnp.sum(p, axis=-1, keepdims=True)
            m_scr[h] = m_new
            acc_scr[h * tq:(h + 1) * tq, :] = alpha * acc_scr[h * tq:(h + 1) * tq, :]
            p_scr[h * tq:(h + 1) * tq, :] = p.astype(BF16)
        acc_scr[...] += _dot(p_scr[...], vc)
        return carry

    lax.fori_loop(0, n_chunks, body, 0)
    for g in range(4):
        o = _diff_finalize(acc_scr[(2 * g) * tq:(2 * g + 1) * tq, :], l_scr[2 * g],
                           acc_scr[(2 * g + 1) * tq:(2 * g + 2) * tq, :], l_scr[2 * g + 1],
                           lam_ref, subln_ref, lam_init)
        o_ref[:, g * LANES:(g + 1) * LANES] = o.astype(o_ref.dtype)


def _a_prompt(q, k, v, toep, lam_p, subln, lam_init, tq=128, tk=512):
    b, t, _ = q.shape
    return pl.pallas_call(
        functools.partial(_a_prompt_kernel, tq=tq, tk=tk, lam_init=lam_init),
        grid=(b, A_KV, t // tq),
        in_specs=[pl.BlockSpec((None, tq, 512), lambda bi, kv, qi: (bi, qi, kv)),
                  pl.BlockSpec((None, t, LANES), lambda bi, kv, qi: (bi, 0, kv)),
                  pl.BlockSpec((None, t, LANES), lambda bi, kv, qi: (bi, 0, kv)),
                  pl.BlockSpec((TOEP_TILES, 8, BAND, BAND), lambda bi, kv, qi: (0, kv, 0, 0)),
                  pl.BlockSpec((4, HEAD_DIM), lambda bi, kv, qi: (0, 0)),
                  pl.BlockSpec((1, LANES), lambda bi, kv, qi: (0, 0))],
        out_specs=pl.BlockSpec((None, tq, 512), lambda bi, kv, qi: (bi, qi, kv)),
        out_shape=jax.ShapeDtypeStruct((b, t, 1024), BF16),
        scratch_shapes=[pltpu.VMEM((8 * tq, LANES), BF16),
                        pltpu.VMEM((8 * tq, tk), BF16),
                        pltpu.VMEM((8, tq, 1), F32),
                        pltpu.VMEM((8, tq, 1), F32),
                        pltpu.VMEM((8 * tq, LANES), F32)],
        compiler_params=_cparams(("parallel", "parallel", "arbitrary")),
        name="a_prompt",
    )(q, k, v, toep, lam_p, subln.reshape(1, LANES))


def _a_sample_kernel(pt_ref, q_ref, new_ref, bp_ref, bn_ref, lam_ref, subln_ref, *rest,
                     pages, tq, lam_init):
    page_refs = rest[:pages]
    o_ref = rest[pages]
    qm_scr, m_scr, l_scr, acc_scr = rest[pages + 1:]
    c = pl.program_id(1)
    low = _low_half()
    rows = 8 * tq

    @pl.when(c == 0)
    def _():
        for kv in range(A_KV):
            for g in range(4):
                qg = q_ref[:, (kv * 4 + g) * LANES:(kv * 4 + g + 1) * LANES]
                qm_scr[kv, (2 * g) * tq:(2 * g + 1) * tq, :] = jnp.where(low, qg, 0.0)
                qm_scr[kv, (2 * g + 1) * tq:(2 * g + 2) * tq, :] = jnp.where(low, 0.0, qg)
        m_scr[...] = jnp.full(m_scr.shape, NEG, F32)
        l_scr[...] = jnp.zeros(l_scr.shape, F32)
        acc_scr[...] = jnp.zeros(acc_scr.shape, F32)

    def update(kv, s, vals):
        m_prev = m_scr[kv]
        m_new = jnp.maximum(m_prev, jnp.max(s, axis=-1, keepdims=True))
        alpha = jnp.exp(m_prev - m_new)
        p = jnp.exp(s - m_new)
        l_scr[kv] = alpha * l_scr[kv] + jnp.sum(p, axis=-1, keepdims=True)
        m_scr[kv] = m_new
        acc_scr[kv] = alpha * acc_scr[kv] + _dot(p.astype(BF16), vals)

    for kv in range(A_KV):
        qm = qm_scr[kv].astype(BF16)
        keys = jnp.concatenate([pr[:, kv * LANES:(kv + 1) * LANES] for pr in page_refs], axis=0)
        vals = jnp.concatenate([pr[:, (A_KV + kv) * LANES:(A_KV + kv + 1) * LANES] for pr in page_refs], axis=0)
        s = _dot_nt(qm, keys.astype(BF16)) + bp_ref[kv]
        update(kv, s, vals.astype(BF16))

    @pl.when(c == pl.num_programs(1) - 1)
    def _():
        pad = jnp.zeros((LANES - tq, LANES), F32)
        for kv in range(A_KV):
            qm = qm_scr[kv].astype(BF16)
            kn = jnp.concatenate([new_ref[:, kv * LANES:(kv + 1) * LANES], pad], axis=0)
            vn = jnp.concatenate([new_ref[:, (A_KV + kv) * LANES:(A_KV + kv + 1) * LANES], pad], axis=0)
            s = _dot_nt(qm, kn.astype(BF16)) + bn_ref[kv]
            update(kv, s, vn.astype(BF16))
            acc = acc_scr[kv]
            l = l_scr[kv]
            for g in range(4):
                r0, r1 = (2 * g) * tq, (2 * g + 1) * tq
                o = _diff_finalize(acc[r0:r0 + tq], l[r0:r0 + tq], acc[r1:r1 + tq], l[r1:r1 + tq],
                                   lam_ref, subln_ref, lam_init)
                o_ref[:, (kv * 4 + g) * LANES:(kv * 4 + g + 1) * LANES] = o.astype(o_ref.dtype)


def _a_sample(q, new_kv, cache, layer, page_table, bias_past, bias_new, lam_p, subln, lam_init,
              pages=16):
    db, tq, _ = q.shape
    n_pages = page_table.shape[1]
    n_chunks = n_pages // pages
    cache2 = cache.reshape(cache.shape[0], cache.shape[1], PAGE_SIZE, 4 * LANES)
    rows = 8 * tq

    def page_spec(j):
        return pl.BlockSpec((None, None, PAGE_SIZE, 4 * LANES),
                            lambda bi, c, pt: (layer, pt[bi * n_pages + c * pages + j], 0, 0))

    grid_spec = pltpu.PrefetchScalarGridSpec(
        num_scalar_prefetch=1,
        grid=(db, n_chunks),
        in_specs=[pl.BlockSpec((None, tq, 1024), lambda bi, c, pt: (bi, 0, 0)),
                  pl.BlockSpec((None, tq, 512), lambda bi, c, pt: (bi, 0, 0)),
                  pl.BlockSpec((A_KV, rows, pages * PAGE_SIZE), lambda bi, c, pt: (0, 0, c)),
                  pl.BlockSpec((A_KV, rows, LANES), lambda bi, c, pt: (0, 0, 0)),
                  pl.BlockSpec((4, HEAD_DIM), lambda bi, c, pt: (0, 0)),
                  pl.BlockSpec((1, LANES), lambda bi, c, pt: (0, 0))]
                 + [page_spec(j) for j in range(pages)],
        out_specs=pl.BlockSpec((None, tq, 1024), lambda bi, c, pt: (bi, 0, 0)),
        scratch_shapes=[pltpu.VMEM((A_KV, rows, LANES), F32),
                        pltpu.VMEM((A_KV, rows, 1), F32),
                        pltpu.VMEM((A_KV, rows, 1), F32),
                        pltpu.VMEM((A_KV, rows, LANES), F32)])
    return pl.pallas_call(
        functools.partial(_a_sample_kernel, pages=pages, tq=tq, lam_init=lam_init),
        grid_spec=grid_spec,
        out_shape=jax.ShapeDtypeStruct((db, tq, 1024), F32),
        compiler_params=_cparams(("parallel", "arbitrary")),
        name="a_sample",
    )(page_table.reshape(-1), q, new_kv, bias_past, bias_new, lam_p, subln.reshape(1, LANES),
      *([cache2] * pages))


def _band_kernel(*refs, n_kv, use_sink, want_lse):
    q_ref, kp_ref, kc_ref, vp_ref, vc_ref, bias_ref = refs[:6]
    pos = 6
    sink_ref = None
    if use_sink:
        sink_ref = refs[pos]
        pos += 1
    o_ref = refs[pos]
    lse_ref = refs[pos + 1] if want_lse else None
    i = pl.program_id(2)
    grp = N_HEADS // n_kv
    low = _low_half()
    col = lax.broadcasted_iota(jnp.int32, (1, 2 * BAND), 1)
    prev_ok = (col >= BAND) | (i > 0)
    lane_id = lax.broadcasted_iota(jnp.int32, (1, LANES), 1)
    lse_acc = jnp.zeros((BAND, LANES), F32)
    for kv in range(n_kv):
        k2 = jnp.concatenate([kp_ref[:, kv * LANES:(kv + 1) * LANES], kc_ref[:, kv * LANES:(kv + 1) * LANES]], axis=0)
        v2 = jnp.concatenate([vp_ref[:, kv * LANES:(kv + 1) * LANES], vc_ref[:, kv * LANES:(kv + 1) * LANES]], axis=0)
        for hp in range(grp // 2):
            c = kv * (grp // 2) + hp
            qp = q_ref[:, c * LANES:(c + 1) * LANES]
            outs = []
            for par in range(2):
                h = 2 * c + par
                zero = jnp.zeros_like(qp)
                qm = jnp.where(low, qp, zero) if par == 0 else jnp.where(low, zero, qp)
                s = jnp.where(prev_ok, _dot_nt(qm, k2) + bias_ref[h], NEG)
                m = jnp.max(s, axis=-1, keepdims=True)
                if use_sink:
                    m = jnp.maximum(m, sink_ref[h])
                p = jnp.exp(s - m)
                l = jnp.sum(p, axis=-1, keepdims=True)
                if use_sink:
                    l = l + jnp.exp(sink_ref[h] - m)
                outs.append(_dot(p.astype(BF16), v2) / l)
                if want_lse:
                    lse_acc = jnp.where(lane_id == h, m + jnp.log(l), lse_acc)
            o_ref[:, c * LANES:(c + 1) * LANES] = jnp.where(low, outs[0], outs[1]).astype(o_ref.dtype)
    if want_lse:
        lse_ref[...] = lse_acc


def _band_prompt(q, kd, vd, bias, sinks, grp_idx, n_grp, n_kv, dil, want_lse, out_dtype):
    b, t, _ = q.shape
    ts = t // dil
    nb = ts // BAND
    ck = n_kv * LANES
    qv = q.reshape(b, ts, dil * n_grp * 1024)
    kv_ = kd.reshape(b, ts, dil * n_grp * ck)
    vv = vd.reshape(b, ts, dil * n_grp * ck)
    cur = lambda bi, r, i: (bi, i, r * n_grp + grp_idx)
    prev = lambda bi, r, i: (bi, jnp.maximum(i - 1, 0), r * n_grp + grp_idx)
    in_specs = [pl.BlockSpec((None, BAND, 1024), cur),
                pl.BlockSpec((None, BAND, ck), prev),
                pl.BlockSpec((None, BAND, ck), cur),
                pl.BlockSpec((None, BAND, ck), prev),
                pl.BlockSpec((None, BAND, ck), cur),
                pl.BlockSpec((N_HEADS, BAND, 2 * BAND), lambda bi, r, i: (0, 0, 0))]
    args = [qv, kv_, kv_, vv, vv, bias]
    if sinks is not None:
        in_specs.append(pl.BlockSpec(memory_space=pltpu.SMEM))
        args.append(sinks)
    out_specs = [pl.BlockSpec((None, BAND, 1024), lambda bi, r, i: (bi, i, r))]
    out_shape = [jax.ShapeDtypeStruct((b, ts, dil * 1024), out_dtype)]
    if want_lse:
        out_specs.append(pl.BlockSpec((None, BAND, LANES), lambda bi, r, i: (bi, i, r)))
        out_shape.append(jax.ShapeDtypeStruct((b, ts, dil * LANES), F32))
    res = pl.pallas_call(
        functools.partial(_band_kernel, n_kv=n_kv, use_sink=sinks is not None, want_lse=want_lse),
        grid=(b, dil, nb),
        in_specs=in_specs,
        out_specs=out_specs,
        out_shape=out_shape,
        compiler_params=_cparams(("parallel", "parallel", "arbitrary")),
        name=f"band_d{dil}",
    )(*args)
    o = res[0].reshape(b, t, 1024)
    if want_lse:
        return o, res[1].reshape(b, t, LANES)
    return o


def _decode_kernel(*refs, n_grp, n_kv, past_lens, use_sink, tq):
    q_ref, new_ref = refs[:2]
    pos = 2
    st_refs = refs[pos:pos + n_grp]; pos += n_grp
    bp_refs = refs[pos:pos + n_grp]; pos += n_grp
    bn_refs = refs[pos:pos + n_grp]; pos += n_grp
    sink_ref = None
    if use_sink:
        sink_ref = refs[pos]; pos += 1
    o_ref = refs[pos]; pos += 1
    nst_refs = refs[pos:pos + n_grp]
    width = 2 * n_kv * HEAD_DIM
    n_chunks = n_kv // 2
    heads_per_chunk = N_HEADS // n_chunks
    grp_heads = N_HEADS // n_kv
    low = _low_half()
    pad = jnp.zeros((LANES - tq, LANES), F32)
    for c in range(n_chunks):
        parts = []
        for g in range(n_grp):
            lp = past_lens[g]
            qs = jnp.concatenate(
                [q_ref[:, (g * N_HEADS + c * heads_per_chunk + hh) * LANES:(g * N_HEADS + c * heads_per_chunk + hh + 1) * LANES]
                 for hh in range(heads_per_chunk)], axis=0).astype(BF16)
            st = st_refs[g]
            kp = st[:, c * LANES:(c + 1) * LANES].astype(BF16)
            vp = st[:, n_kv * HEAD_DIM + c * LANES:n_kv * HEAD_DIM + (c + 1) * LANES].astype(BF16)
            kn = jnp.concatenate([new_ref[:, g * width + c * LANES:g * width + (c + 1) * LANES], pad], axis=0).astype(BF16)
            vn = jnp.concatenate([new_ref[:, g * width + n_kv * HEAD_DIM + c * LANES:g * width + n_kv * HEAD_DIM + (c + 1) * LANES], pad], axis=0).astype(BF16)
            sp = _dot_nt(qs, kp) + bp_refs[g][c]
            sn = _dot_nt(qs, kn) + bn_refs[g][c]
            m = jnp.maximum(jnp.max(sp, axis=-1, keepdims=True), jnp.max(sn, axis=-1, keepdims=True))
            if use_sink:
                m = jnp.maximum(m, sink_ref[c])
            pp = jnp.exp(sp - m)
            pn = jnp.exp(sn - m)
            l = jnp.sum(pp, axis=-1, keepdims=True) + jnp.sum(pn, axis=-1, keepdims=True)
            if use_sink:
                l = l + jnp.exp(sink_ref[c] - m)
            acc = _dot(pp.astype(BF16), vp) + _dot(pn.astype(BF16), vn)
            parts.append((m, l, acc))
        m_all = parts[0][0]
        for m, _, _ in parts[1:]:
            m_all = jnp.maximum(m_all, m)
        num = 0.0
        den = 0.0
        for m, l, acc in parts:
            w = jnp.exp(m - m_all)
            num = num + acc * w
            den = den + l * w
        o = num / den
        for hh in range(heads_per_chunk):
            h = c * heads_per_chunk + hh
            odd = (h // grp_heads) % 2
            oh = o[hh * tq:(hh + 1) * tq, :]
            o_ref[:, h * LANES:(h + 1) * LANES] = (jnp.where(low, 0.0, oh) if odd else jnp.where(low, oh, 0.0)).astype(o_ref.dtype)
    for g in range(n_grp):
        lp = past_lens[g]
        if lp > tq:
            nst_refs[g][0:lp - tq, :] = st_refs[g][tq:lp, :]
        nst_refs[g][lp - tq:lp, :] = new_ref[:, g * width:(g + 1) * width]


def _decode(q, new_kv, states, layer, bias_past, bias_new, sinks_rows, n_kv):
    db, tq, _ = q.shape
    n_grp = len(states)
    width = 2 * n_kv * HEAD_DIM
    n_chunks = n_kv // 2
    rows = tq * N_HEADS // n_chunks
    past_lens = tuple(s.shape[2] for s in states)
    in_specs = [pl.BlockSpec((None, tq, q.shape[2]), lambda bi: (bi, 0, 0)),
                pl.BlockSpec((None, tq, n_grp * width), lambda bi: (bi, 0, 0))]
    in_specs += [pl.BlockSpec((None, None, lp, width), lambda bi: (layer, bi, 0, 0)) for lp in past_lens]
    in_specs += [pl.BlockSpec((n_chunks, rows, lp), lambda bi: (0, 0, 0)) for lp in past_lens]
    in_specs += [pl.BlockSpec((n_chunks, rows, LANES), lambda bi: (0, 0, 0)) for _ in past_lens]
    args = [q, new_kv] + list(states) + list(bias_past) + list(bias_new)
    if sinks_rows is not None:
        in_specs.append(pl.BlockSpec((n_chunks, rows, 1), lambda bi: (0, 0, 0)))
        args.append(sinks_rows)
    out_specs = [pl.BlockSpec((None, tq, N_HEADS * LANES), lambda bi: (bi, 0, 0))]
    out_specs += [pl.BlockSpec((None, lp, width), lambda bi: (bi, 0, 0)) for lp in past_lens]
    out_shape = [jax.ShapeDtypeStruct((db, tq, N_HEADS * LANES), F32)]
    out_shape += [jax.ShapeDtypeStruct((db, lp, width), F32) for lp in past_lens]
    return pl.pallas_call(
        functools.partial(_decode_kernel, n_grp=n_grp, n_kv=n_kv, past_lens=past_lens,
                          use_sink=sinks_rows is not None, tq=tq),
        grid=(db,),
        in_specs=in_specs,
        out_specs=out_specs,
        out_shape=out_shape,
        compiler_params=_cparams(("parallel",)),
        name=f"decode_kv{n_kv}",
    )(*args)


def _ffn_kernel(*refs, n_attn, sample, tiles_per_seq, row_tile, ff_chunk, tq):
    x_ref = refs[0]
    pos = 1
    if n_attn == 1:
        o_in = refs[pos]; pos += 1
    else:
        o_refs = refs[pos:pos + n_attn]; pos += n_attn
        lse_refs = refs[pos:pos + n_attn]; pos += n_attn
    wo_ref, g_ref, win_ref, cw_ref, cb_ref, wout_ref = refs[pos:pos + 6]; pos += 6
    if sample:
        fix1_ref, fix2_ref = refs[pos:pos + 2]; pos += 2
    xo_ref, gate_ref = refs[pos:pos + 2]; pos += 2
    gp_scr, halo_scr = refs[pos:pos + 2]
    i = pl.program_id(0)
    low = _low_half()

    @pl.when(i == 0)
    def _():
        halo_scr[...] = jnp.zeros(halo_scr.shape, F32)

    if n_attn == 1:
        ob = o_in[...]
    else:
        lses = [r[...] for r in lse_refs]
        m_all = lses[0]
        for z in lses[1:]:
            m_all = jnp.maximum(m_all, z)
        es = [jnp.exp(z - m_all) for z in lses]
        den = es[0]
        for e in es[1:]:
            den = den + e
        ws = [e / den for e in es]
        chunks = []
        for c in range(1024 // LANES):
            acc = None
            for g in range(n_attn):
                wsel = jnp.where(low, ws[g][:, 2 * c:2 * c + 1], ws[g][:, 2 * c + 1:2 * c + 2])
                term = wsel * o_refs[g][:, c * LANES:(c + 1) * LANES]
                acc = term if acc is None else acc + term
            chunks.append(acc.astype(BF16))
        ob = jnp.concatenate(chunks, axis=1)
    x1 = x_ref[...] + _dot(ob, wo_ref[...])
    h = x1 * lax.rsqrt(jnp.mean(x1 * x1, axis=-1, keepdims=True) + RMS_EPS) * g_ref[...]
    hb = h.astype(BF16)

    if sample:
        t_in_seq = lax.broadcasted_iota(jnp.int32, (row_tile, 1), 0) & (tq - 1)
    y = jnp.zeros((row_tile, D_MODEL), F32)
    for f in range(D_FF // ff_chunk):
        f0 = f * ff_chunk
        gate = _dot(hb, win_ref[:, f0:f0 + ff_chunk])
        up = _dot(hb, win_ref[:, D_FF + f0:D_FF + f0 + ff_chunk])
        if sample:
            gp_scr[0:SUBLANES, :] = jnp.zeros((SUBLANES, ff_chunk), F32)
        else:
            first = (i % tiles_per_seq) == 0
            gp_scr[0:SUBLANES, :] = jnp.where(first, 0.0, halo_scr[f])
        gp_scr[SUBLANES:SUBLANES + row_tile, :] = gate
        g2 = gp_scr[SUBLANES - 2:SUBLANES - 2 + row_tile, :]
        g1 = gp_scr[SUBLANES - 1:SUBLANES - 1 + row_tile, :]
        if sample:
            g2 = jnp.where(t_in_seq >= 2, g2, fix2_ref[:, f0:f0 + ff_chunk])
            g1 = jnp.where(t_in_seq >= 1, g1, fix1_ref[:, f0:f0 + ff_chunk])
        gc = (cb_ref[:, f0:f0 + ff_chunk] + cw_ref[0:1, f0:f0 + ff_chunk] * g2
              + cw_ref[1:2, f0:f0 + ff_chunk] * g1 + cw_ref[2:3, f0:f0 + ff_chunk] * gate)
        act = gc * (1.0 / (1.0 + jnp.exp(-gc))) * up
        y = y + _dot(act.astype(BF16), wout_ref[f0:f0 + ff_chunk, :])
        if sample:
            gate_ref[:, f0:f0 + ff_chunk] = gate
        else:
            tail = gate[row_tile - SUBLANES:row_tile, :]
            halo_scr[f] = tail
            gate_ref[:, f0:f0 + ff_chunk] = tail
    xo_ref[...] = x1 + y


def _ffn(x, attn, lses, w_o, gain, w_in, conv_w, conv_b, w_out, fix, seq_len, row_tile=512,
         ff_chunk=1408):
    rows = x.shape[0]
    sample = fix is not None
    n_attn = len(attn)
    row_spec = lambda w: pl.BlockSpec((row_tile, w), lambda i: (i, 0))
    full = lambda a: pl.BlockSpec(a.shape, lambda i: (0,) * a.ndim, pipeline_mode=pl.Buffered(1))
    in_specs = [row_spec(D_MODEL)] + [row_spec(a.shape[1]) for a in attn]
    args = [x] + list(attn)
    if n_attn > 1:
        in_specs += [row_spec(LANES) for _ in lses]
        args += list(lses)
    consts = [w_o.astype(BF16), gain.reshape(1, D_MODEL), w_in.astype(BF16), conv_w,
              conv_b.reshape(1, D_FF), w_out.astype(BF16)]
    in_specs += [full(a) for a in consts]
    args += consts
    if sample:
        in_specs += [row_spec(D_FF), row_spec(D_FF)]
        args += list(fix)
        gate_spec = row_spec(D_FF)
        gate_shape = jax.ShapeDtypeStruct((rows, D_FF), F32)
        tiles_per_seq = 1
        tq = seq_len
    else:
        gate_spec = pl.BlockSpec((None, SUBLANES, D_FF), lambda i: (i, 0, 0))
        gate_shape = jax.ShapeDtypeStruct((rows // row_tile, SUBLANES, D_FF), F32)
        tiles_per_seq = seq_len // row_tile
        tq = 1
    return pl.pallas_call(
        functools.partial(_ffn_kernel, n_attn=n_attn, sample=sample, tiles_per_seq=tiles_per_seq,
                          row_tile=row_tile, ff_chunk=ff_chunk, tq=tq),
        grid=(rows // row_tile,),
        in_specs=in_specs,
        out_specs=[row_spec(D_MODEL), gate_spec],
        out_shape=[jax.ShapeDtypeStruct((rows, D_MODEL), F32), gate_shape],
        scratch_shapes=[pltpu.VMEM((SUBLANES + row_tile, ff_chunk), F32),
                        pltpu.VMEM((D_FF // ff_chunk, SUBLANES, ff_chunk), F32)],
        compiler_params=_cparams(("arbitrary",)),
        name="ffn_sample" if sample else "ffn_prompt",
    )(*args)


def _expand_wo_sample(w_o, n_kv):
    grp_heads = N_HEADS // n_kv
    src = []
    for h in range(N_HEADS):
        rows = _cols(h * HEAD_DIM, HEAD_DIM)
        src += (_zeros(HEAD_DIM) + rows) if (h // grp_heads) % 2 else (rows + _zeros(HEAD_DIM))
    src = np.asarray(src)
    return jnp.where(jnp.asarray(src >= 0)[:, None], jnp.take(w_o, jnp.asarray(np.maximum(src, 0)), axis=0), 0.0)


def kernel(x_prompt, x_sample, cache_kv_a, state_kv_b, state_kv_c1, state_kv_c2, state_kv_c3, state_conv_ffn, page_table, rel_bias_table, norm_mix, norm_ffn, w_in_a, q_norm_a, k_norm_a, lambda_a, subln_a, w_o_a, w_in_b, q_norm_b, k_norm_b, sinks_b, w_o_b, w_in_c, q_norm_c, k_norm_c, w_o_c, w_ffn_in, conv_ffn_w, conv_ffn_b, w_ffn_out):
    bp, tp, _ = x_prompt.shape
    bs, ts, _ = x_sample.shape
    depth = norm_mix.shape[0]
    past_len = page_table.shape[1] * PAGE_SIZE
    xp = x_prompt.reshape(bp * tp, D_MODEL)
    xs = x_sample.reshape(bs * ts, D_MODEL)
    state_c = (state_kv_c1, state_kv_c2, state_kv_c3)

    toep = _toep_table(rel_bias_table)
    band_bias = {d: _band_bias_table(rel_bias_table, d) for d in (1, 4, 16)}
    a_cols = [[kv * 8 + j for j in range(8)] for kv in range(A_KV)]
    a_bias_past, a_bias_new = _decode_bias_table(rel_bias_table, a_cols, past_len, past_len + ts, 1, ts)
    b_cols = [list(range(N_HEADS))]
    b_len = state_kv_b.shape[2]
    b_bias_past, b_bias_new = _decode_bias_table(rel_bias_table, b_cols, b_len, B_WINDOW, 1, ts)
    c_cols = [list(range(8)), list(range(8, 16))]
    c_bias = [_decode_bias_table(rel_bias_table, c_cols, state_c[g].shape[2], w, d, ts)
              for g, (w, d) in enumerate(C_PAIRS)]

    kv_a_p, kv_a_s, kv_b_p, kv_b_s = [], [], [], []
    kv_c_p = [[] for _ in C_PAIRS]
    kv_c_s = [[] for _ in C_PAIRS]
    conv_p, conv_s = [], []
    for i in range(depth):
        n = i // N_MIXERS
        if i % N_MIXERS == 0:
            lam_init = 0.8 - 0.6 * math.exp(-0.3 * i)
            q, kvn, k, v = _run_proj(xp, norm_mix[i], w_in_a[n], q_norm_a[n], k_norm_a[n], _layout_a(BF16), 512)
            op = _a_prompt(q.reshape(bp, tp, 1024), k.reshape(bp, tp, 256), v.reshape(bp, tp, 256),
                           toep, lambda_a[n], subln_a[n], lam_init)
            attn_p, lse_p = [op.reshape(bp * tp, 1024)], None
            kv_a_p.append(kvn.reshape(bp, tp, 2, A_KV, 2 * HEAD_DIM))
            qs, kvs, _, _ = _run_proj(xs, norm_mix[i], w_in_a[n], q_norm_a[n], k_norm_a[n], _layout_a(F32), 512)
            os_ = _a_sample(qs.reshape(bs, ts, 1024), kvs.reshape(bs, ts, 512), cache_kv_a, n, page_table,
                            a_bias_past, a_bias_new, lambda_a[n], subln_a[n], lam_init)
            attn_s = [os_.reshape(bs * ts, 1024).astype(BF16)]
            kv_a_s.append(kvs.reshape(bs, ts, 2, A_KV, 2 * HEAD_DIM))
            w_o_p = w_o_a[n]
            w_o_s = w_o_a[n]
        elif i % N_MIXERS == 1:
            q, kvn, kd, vd = _run_proj(xp, norm_mix[i], w_in_b[n], q_norm_b[n], k_norm_b[n],
                                       _layout_band_prompt(1, B_KV), 512)
            op = _band_prompt(q.reshape(bp, tp, 1024), kd.reshape(bp, tp, B_KV * LANES),
                              vd.reshape(bp, tp, B_KV * LANES), band_bias[1], sinks_b[n], 0, 1, B_KV, 1,
                              False, BF16)
            attn_p, lse_p = [op.reshape(bp * tp, 1024)], None
            kvn = kvn.reshape(bp, tp, 2, B_KV, HEAD_DIM)
            kv_b_p.append(kvn[:, tp - min(B_WINDOW, tp):])
            qs, kvs = _run_proj(xs, norm_mix[i], w_in_b[n], q_norm_b[n], k_norm_b[n],
                                _layout_band_sample(1, B_KV), 512)
            sink_rows = jnp.repeat(sinks_b[n], ts).reshape(1, N_HEADS * ts, 1)
            st = state_kv_b.reshape(state_kv_b.shape[:3] + (2 * B_KV * HEAD_DIM,))
            os_, nst = _decode(qs.reshape(bs, ts, -1), kvs.reshape(bs, ts, -1), [st], n,
                               [b_bias_past], [b_bias_new], sink_rows, B_KV)
            attn_s = [os_.reshape(bs * ts, N_HEADS * LANES).astype(BF16)]
            kv_b_s.append(nst.reshape(bs, b_len, 2, B_KV, HEAD_DIM))
            w_o_p = w_o_b[n]
            w_o_s = _expand_wo_sample(w_o_b[n], B_KV)
        else:
            q, kvn, kd, vd = _run_proj(xp, norm_mix[i], w_in_c[n], q_norm_c[n], k_norm_c[n],
                                       _layout_band_prompt(3, C_KV), 256)
            q = q.reshape(bp, tp, 3 * 1024)
            kd = kd.reshape(bp, tp, 3 * C_KV * LANES)
            vd = vd.reshape(bp, tp, 3 * C_KV * LANES)
            attn_p, lse_p = [], []
            for g, (w, d) in enumerate(C_PAIRS):
                o_g, lse_g = _band_prompt(q, kd, vd, band_bias[d], None, g, 3, C_KV, d, True, F32)
                attn_p.append(o_g.reshape(bp * tp, 1024))
                lse_p.append(lse_g.reshape(bp * tp, LANES))
            kvn = kvn.reshape(bp, tp, 3, 2, C_KV, HEAD_DIM)
            for g, (w, d) in enumerate(C_PAIRS):
                kv_c_p[g].append(kvn[:, tp - min(w, tp):, g])
            qs, kvs = _run_proj(xs, norm_mix[i], w_in_c[n], q_norm_c[n], k_norm_c[n],
                                _layout_band_sample(3, C_KV), 256)
            sts = [s.reshape(s.shape[:3] + (2 * C_KV * HEAD_DIM,)) for s in state_c]
            res = _decode(qs.reshape(bs, ts, -1), kvs.reshape(bs, ts, -1), sts, n,
                          [cb[0] for cb in c_bias], [cb[1] for cb in c_bias], None, C_KV)
            attn_s = [res[0].reshape(bs * ts, N_HEADS * LANES).astype(BF16)]
            for g in range(3):
                kv_c_s[g].append(res[1 + g].reshape(bs, sts[g].shape[2], 2, C_KV, HEAD_DIM))
            w_o_p = w_o_c[n]
            w_o_s = _expand_wo_sample(w_o_c[n], C_KV)

        xp, gate_tail = _ffn(xp, attn_p, lse_p, w_o_p, norm_ffn[i], w_ffn_in[i], conv_ffn_w[i],
                             conv_ffn_b[i], w_ffn_out[i], None, tp)
        tiles = tp // 512
        conv_p.append(gate_tail.reshape(bp, tiles, SUBLANES, D_FF)[:, -1, SUBLANES - (CONV_W - 1):])
        hist = state_conv_ffn[i]
        zero = jnp.zeros((bs, ts - 1, D_FF), F32)
        fix1 = jnp.concatenate([hist[:, 1:2], zero], axis=1).reshape(bs * ts, D_FF)
        fix2 = jnp.concatenate([hist[:, 0:1], hist[:, 1:2], zero[:, 1:]], axis=1).reshape(bs * ts, D_FF)
        xs, gate_s = _ffn(xs, attn_s, None, w_o_s, norm_ffn[i], w_ffn_in[i], conv_ffn_w[i],
                          conv_ffn_b[i], w_ffn_out[i], (fix1, fix2), ts, row_tile=256)
        conv_s.append(gate_s.reshape(bs, ts, D_FF)[:, ts - (CONV_W - 1):])

    return (xp.reshape(bp, tp, D_MODEL), xs.reshape(bs, ts, D_MODEL),
            jnp.stack(kv_a_p), jnp.stack(kv_a_s),
            jnp.stack(kv_b_p), jnp.stack(kv_b_s),
            jnp.stack(kv_c_p[0]), jnp.stack(kv_c_s[0]),
            jnp.stack(kv_c_p[1]), jnp.stack(kv_c_s[1]),
            jnp.stack(kv_c_p[2]), jnp.stack(kv_c_s[2]),
            jnp.stack(conv_p), jnp.stack(conv_s))
```

```python
import functools
import math

import numpy as np
import jax
import jax.numpy as jnp
from jax import lax
from jax.experimental import pallas as pl
from jax.experimental.pallas import tpu as pltpu

F32 = jnp.float32
BF16 = jnp.bfloat16

D_MODEL = 1024
HEAD_DIM = 64
LANES = 128
SUBLANES = 8
RMS_EPS = 1e-6
NEG = -1e30
LOG2E = math.log2(math.e)
N_HEADS = 16
N_BUCKETS = 32
MAX_DISTANCE = 2048
PAGE_SIZE = 128
D_FF = 2816
CONV_W = 3
N_MIXERS = 3
A_KV = 2
B_KV = 2
C_KV = 4
B_WINDOW = 128
C_PAIRS = ((128, 1), (512, 4), (2048, 16))
BAND = 128
VMEM_LIMIT = 56 * 1024 * 1024


def _bucket_thresholds():
    n = np.arange(0, 1 << 15)
    x = np.log(np.maximum(n, 1) / (N_BUCKETS // 2)) / math.log(MAX_DISTANCE / (N_BUCKETS // 2))
    large = N_BUCKETS // 2 + (x * (N_BUCKETS - N_BUCKETS // 2)).astype(np.int64)
    bucket = np.where(n < N_BUCKETS // 2, n, np.minimum(large, N_BUCKETS - 1))
    return [int(np.argmax(bucket >= b)) for b in range(N_BUCKETS)]


THR = _bucket_thresholds()
N_TOEP = -(-(THR[-1] + BAND - 1) // BAND)
TOEP_TILES = N_TOEP + 2


def _cparams(sem, vmem=VMEM_LIMIT):
    return pltpu.CompilerParams(dimension_semantics=sem, vmem_limit_bytes=vmem)


def _dot(a, b):
    return jnp.dot(a, b, preferred_element_type=F32)


def _dot_nt(a, b):
    return lax.dot_general(a, b, (((1,), (1,)), ((), ())), preferred_element_type=F32)


def _low_half():
    return lax.broadcasted_iota(jnp.int32, (1, LANES), 1) < HEAD_DIM


def _bias_of(dist, col, tab_ref, lo, hi):
    b_lo = max(b for b in range(N_BUCKETS) if THR[b] <= max(lo, 0))
    b_hi = max(b for b in range(N_BUCKETS) if THR[b] <= max(hi, 0))
    val = jnp.full(dist.shape, tab_ref[b_hi, col], F32)
    for b in range(b_hi - 1, b_lo - 1, -1):
        val = jnp.where(dist < THR[b + 1], tab_ref[b, col], val)
    return val


def _toep_kernel(tab_ref, o_ref):
    t = pl.program_id(0)
    key = lax.broadcasted_iota(jnp.int32, (BAND, BAND), 0)
    qry = lax.broadcasted_iota(jnp.int32, (BAND, BAND), 1)
    for delta in range(-1, N_TOEP + 1):
        @pl.when(t == delta + 1)
        def _(delta=delta):
            for c in range(N_HEADS):
                if delta < 0:
                    o_ref[c] = jnp.full((BAND, BAND), NEG, F32)
                elif delta == N_TOEP:
                    o_ref[c] = jnp.full((BAND, BAND), tab_ref[N_BUCKETS - 1, c] * LOG2E, F32)
                else:
                    dist = delta * BAND + qry - key
                    val = _bias_of(dist, c, tab_ref, delta * BAND - BAND + 1, delta * BAND + BAND - 1)
                    o_ref[c] = jnp.where(dist >= 0, val * LOG2E, NEG)


def _toep_table(table):
    return pl.pallas_call(
        _toep_kernel,
        grid=(TOEP_TILES,),
        in_specs=[pl.BlockSpec(memory_space=pltpu.SMEM)],
        out_specs=pl.BlockSpec((None, N_HEADS, BAND, BAND), lambda t: (t, 0, 0, 0)),
        out_shape=jax.ShapeDtypeStruct((TOEP_TILES, N_HEADS, BAND, BAND), F32),
        compiler_params=_cparams(("arbitrary",)),
        name="bias_toeplitz",
    )(table)


def _band_bias_kernel(tab_ref, o_ref, *, dil):
    row = lax.broadcasted_iota(jnp.int32, (BAND, 2 * BAND), 0)
    col = lax.broadcasted_iota(jnp.int32, (BAND, 2 * BAND), 1)
    steps = BAND + row - col
    valid = (steps >= 0) & (steps <= BAND)
    for c in range(N_HEADS):
        val = _bias_of(steps * dil, c, tab_ref, 0, BAND * dil)
        o_ref[c] = jnp.where(valid, val, NEG)


def _band_bias_table(table, dil):
    return pl.pallas_call(
        functools.partial(_band_bias_kernel, dil=dil),
        in_specs=[pl.BlockSpec(memory_space=pltpu.SMEM)],
        out_shape=jax.ShapeDtypeStruct((N_HEADS, BAND, 2 * BAND), F32),
        compiler_params=_cparams(None),
        name=f"bias_band_d{dil}",
    )(table)


def _decode_bias_kernel(tab_ref, past_ref, new_ref, *, cols, past_len, window, dil, tq):
    n_grp = len(cols)
    t_p = lax.broadcasted_iota(jnp.int32, (tq, past_len), 0)
    i_p = lax.broadcasted_iota(jnp.int32, (tq, past_len), 1)
    d_p = past_len + t_p - i_p
    ok_p = d_p <= window
    if dil > 1:
        ok_p = ok_p & ((d_p & (dil - 1)) == 0)
    t_n = lax.broadcasted_iota(jnp.int32, (tq, LANES), 0)
    i_n = lax.broadcasted_iota(jnp.int32, (tq, LANES), 1)
    d_n = t_n - i_n
    ok_n = (d_n >= 0) & (i_n < tq)
    if dil > 1:
        ok_n = ok_n & ((d_n & (dil - 1)) == 0)
    for r in range(n_grp):
        c = cols[r]
        vp = _bias_of(d_p, c, tab_ref, 1, min(window, past_len + tq - 1))
        past_ref[r * tq:(r + 1) * tq, :] = jnp.where(ok_p, vp, NEG)
        vn = _bias_of(d_n, c, tab_ref, 0, tq - 1)
        new_ref[r * tq:(r + 1) * tq, :] = jnp.where(ok_n, vn, NEG)


def _decode_bias_table(table, cols, past_len, window, dil, tq):
    outs = [pl.pallas_call(
        functools.partial(_decode_bias_kernel, cols=tuple(cc), past_len=past_len, window=window,
                          dil=dil, tq=tq),
        in_specs=[pl.BlockSpec(memory_space=pltpu.SMEM)],
        out_shape=(jax.ShapeDtypeStruct((len(cc) * tq, past_len), F32),
                   jax.ShapeDtypeStruct((len(cc) * tq, LANES), F32)),
        compiler_params=_cparams(None),
        name=f"bias_decode_L{past_len}_d{dil}",
    )(table) for cc in cols]
    return jnp.stack([o[0] for o in outs]), jnp.stack([o[1] for o in outs])


def _proj_kernel(x_ref, g_ref, w_ref, cg_ref, *rest, plan, col_chunk, transposed):
    x = x_ref[...]
    h = x * lax.rsqrt(jnp.mean(x * x, axis=-1, keepdims=True) + RMS_EPS) * g_ref[...]
    hb = h.astype(BF16)
    low = _low_half()
    out_refs = rest
    if transposed:
        wt_ref, out_refs, ot_ref = rest[0], rest[1:-1], rest[-1]
        ot_ref[...] = _dot_nt(wt_ref[...], hb).astype(ot_ref.dtype)
    n_total = w_ref.shape[1]
    for c0 in range(0, n_total, col_chunk):
        cw = min(col_chunk, n_total - c0)
        y = _dot(hb, w_ref[:, c0:c0 + cw])
        for j in range(cw // LANES):
            src = c0 + j * LANES
            normed, dsts = plan[src // LANES]
            blk = y[:, j * LANES:(j + 1) * LANES]
            if normed:
                sq = blk * blk
                ms_lo = jnp.sum(jnp.where(low, sq, 0.0), axis=-1, keepdims=True) * (1.0 / HEAD_DIM)
                ms_hi = jnp.sum(jnp.where(low, 0.0, sq), axis=-1, keepdims=True) * (1.0 / HEAD_DIM)
                inv = jnp.where(low, lax.rsqrt(ms_lo + RMS_EPS), lax.rsqrt(ms_hi + RMS_EPS))
                blk = blk * inv * cg_ref[:, src:src + LANES]
            for oi, dst in dsts:
                out_refs[oi][:, dst:dst + LANES] = blk.astype(out_refs[oi].dtype)


class _ProjLayout:
    def __init__(self):
        self.src = []
        self.gain = []
        self.plan = []
        self.outs = []

    def add_output(self, width, dtype):
        self.outs.append((width, dtype))
        return len(self.outs) - 1

    def add_chunk(self, src_cols, kind, dsts):
        assert len(src_cols) == LANES
        self.src.extend(src_cols)
        self.gain.extend([kind] * LANES)
        self.plan.append((kind != 0, list(dsts)))


def _cols(start, n):
    return list(range(start, start + n))


def _zeros(n):
    return [-1] * n


def _run_proj(x, norm_gain, w_in, q_gain, k_gain, layout, row_tile, col_chunk=1024, q_scale=1.0,
              t_cols=None, seq_len=None):
    rows = x.shape[0]
    src = np.asarray(layout.src)
    w_ext = jnp.where(jnp.asarray(src >= 0)[None, :], jnp.take(w_in, jnp.asarray(np.maximum(src, 0)), axis=1), 0.0)
    w_ext = w_ext.astype(BF16)
    kind = np.asarray(layout.gain)
    n_ext = len(src)
    reps = n_ext // HEAD_DIM
    cg = jnp.where(jnp.asarray(kind == 1), jnp.tile(q_gain, reps) * (HEAD_DIM ** -0.5 * q_scale),
                   jnp.where(jnp.asarray(kind == 2), jnp.tile(k_gain, reps), 1.0)).reshape(1, n_ext)
    out_shape = [jax.ShapeDtypeStruct((rows, w), dt) for w, dt in layout.outs]
    out_specs = [pl.BlockSpec((row_tile, w), lambda i: (i, 0)) for w, _ in layout.outs]
    in_specs = [pl.BlockSpec((row_tile, D_MODEL), lambda i: (i, 0)),
                pl.BlockSpec((1, D_MODEL), lambda i: (0, 0)),
                pl.BlockSpec((D_MODEL, n_ext), lambda i: (0, 0), pipeline_mode=pl.Buffered(1)),
                pl.BlockSpec((1, n_ext), lambda i: (0, 0))]
    args = [x, norm_gain.reshape(1, D_MODEL), w_ext, cg]
    if t_cols is not None:
        n_t = len(t_cols)
        tiles = seq_len // row_tile
        in_specs.append(pl.BlockSpec((n_t, D_MODEL), lambda i: (0, 0), pipeline_mode=pl.Buffered(1)))
        args.append(jnp.take(w_in, jnp.asarray(np.asarray(t_cols)), axis=1).T.astype(BF16))
        out_shape.append(jax.ShapeDtypeStruct((rows // seq_len, n_t, seq_len), BF16))
        out_specs.append(pl.BlockSpec((None, n_t, row_tile), lambda i: (i // tiles, 0, i % tiles)))
    return pl.pallas_call(
        functools.partial(_proj_kernel, plan=tuple(layout.plan), col_chunk=col_chunk,
                          transposed=t_cols is not None),
        grid=(rows // row_tile,),
        in_specs=in_specs,
        out_specs=out_specs,
        out_shape=out_shape,
        compiler_params=_cparams(("parallel",)),
        name="proj",
    )(*args)


def _layout_a(prompt):
    lay = _ProjLayout()
    oq = lay.add_output(1024, BF16 if prompt else F32)
    okv = lay.add_output(512, F32)
    ok = lay.add_output(256, BF16) if prompt else None
    for c in range(8):
        lay.add_chunk(_cols(c * LANES, LANES), 1, [(oq, c * LANES)])
    for c in range(2):
        lay.add_chunk(_cols(1024 + c * LANES, LANES), 2,
                      [(okv, c * LANES)] + ([(ok, c * LANES)] if prompt else []))
    for c in range(2):
        lay.add_chunk(_cols(1280 + c * LANES, LANES), 0, [(okv, 256 + c * LANES)])
    return lay


def _layout_band_prompt(n_grp, n_kv):
    nq = n_grp * N_HEADS * HEAD_DIM
    nk = n_grp * n_kv * HEAD_DIM
    lay = _ProjLayout()
    oq = lay.add_output(nq, BF16)
    okv = lay.add_output(2 * nk, F32)
    okd = lay.add_output(2 * nk, BF16)
    ovd = lay.add_output(2 * nk, BF16)
    for c in range(nq // LANES):
        lay.add_chunk(_cols(c * LANES, LANES), 1, [(oq, c * LANES)])
    gk = n_kv * HEAD_DIM
    for g in range(n_grp):
        for c in range(gk // LANES):
            lay.add_chunk(_cols(nq + g * gk + c * LANES, LANES), 2, [(okv, g * 2 * gk + c * LANES)])
        for c in range(gk // LANES):
            lay.add_chunk(_cols(nq + nk + g * gk + c * LANES, LANES), 0, [(okv, g * 2 * gk + gk + c * LANES)])
    for g in range(n_grp):
        for kv in range(n_kv):
            base = nq + g * gk + kv * HEAD_DIM
            lay.add_chunk(_cols(base, HEAD_DIM) * 2, 2, [(okd, (g * n_kv + kv) * LANES)])
    for g in range(n_grp):
        for kv in range(n_kv):
            base = nq + nk + g * gk + kv * HEAD_DIM
            lay.add_chunk(_cols(base, HEAD_DIM) * 2, 0, [(ovd, (g * n_kv + kv) * LANES)])
    return lay


def _layout_band_sample(n_grp, n_kv):
    nq = n_grp * N_HEADS * HEAD_DIM
    nk = n_grp * n_kv * HEAD_DIM
    grp_heads = N_HEADS // n_kv
    lay = _ProjLayout()
    oq = lay.add_output(n_grp * N_HEADS * LANES, F32)
    okv = lay.add_output(2 * nk, F32)
    for g in range(n_grp):
        for h in range(N_HEADS):
            qc = _cols((g * N_HEADS + h) * HEAD_DIM, HEAD_DIM)
            odd = (h // grp_heads) % 2
            lay.add_chunk(_zeros(HEAD_DIM) + qc if odd else qc + _zeros(HEAD_DIM), 1,
                          [(oq, (g * N_HEADS + h) * LANES)])
    gk = n_kv * HEAD_DIM
    for g in range(n_grp):
        for c in range(gk // LANES):
            lay.add_chunk(_cols(nq + g * gk + c * LANES, LANES), 2, [(okv, g * 2 * gk + c * LANES)])
        for c in range(gk // LANES):
            lay.add_chunk(_cols(nq + nk + g * gk + c * LANES, LANES), 0, [(okv, g * 2 * gk + gk + c * LANES)])
    return lay


def _diff_finalize(acc0, l0, acc1, l1, lam_ref, subln_ref, lam_init):
    lp = lam_ref[...]
    lam = (jnp.exp(jnp.sum(lp[0:1] * lp[1:2], axis=-1, keepdims=True))
           - jnp.exp(jnp.sum(lp[2:3] * lp[3:4], axis=-1, keepdims=True)) + lam_init)
    o = acc0 / l0 - lam * (acc1 / l1)
    o = o * lax.rsqrt(jnp.mean(o * o, axis=-1, keepdims=True) + RMS_EPS) * subln_ref[...]
    return o * (1.0 - lam_init)


def _a_prompt_kernel(q_ref, k_ref, vt_ref, toep_ref, lam_ref, subln_ref, o_ref,
                     qm_scr, ta_scr, tb_scr, p_scr, cm_scr, m_scr, l_scr, acc_scr, *, tq, tk, lam_init, rb):
    qi = pl.program_id(2)
    low = _low_half()
    nh = 8
    w = nh * tq
    for g in range(4):
        qg = q_ref[:, g * LANES:(g + 1) * LANES]
        qm_scr[(2 * g) * tq:(2 * g + 1) * tq, :] = jnp.where(low, qg, jnp.zeros_like(qg))
        qm_scr[(2 * g + 1) * tq:(2 * g + 2) * tq, :] = jnp.where(low, jnp.zeros_like(qg), qg)
    m_scr[...] = jnp.full(m_scr.shape, NEG, F32)
    l_scr[...] = jnp.zeros(l_scr.shape, F32)
    acc_scr[...] = jnp.zeros(acc_scr.shape, F32)
    n_chunks = (qi * tq + tq - 1) // tk + 1
    last = k_ref.shape[0] // tk - 1

    def stage_a(kj, t_ref):
        k0 = pl.multiple_of(jnp.minimum(kj, last) * tk, tk)
        s = _dot_nt(k_ref[pl.ds(k0, tk), :], qm_scr[...])
        macc = jnp.full((SUBLANES, w), NEG, F32)
        for u in range(tk // BAND):
            idx = [jnp.clip((qi * (tq // BAND) + r) - (kj * (tk // BAND) + u) + 1, 0, N_TOEP + 1)
                   for r in range(tq // BAND)]
            for b0 in range(0, BAND, rb):
                bias = jnp.concatenate([toep_ref[idx[r], h, b0:b0 + rb, :] for h in range(nh)
                                        for r in range(tq // BAND)], axis=1)
                r0 = u * BAND + b0
                t = s[r0:r0 + rb, :] + bias
                t_ref[r0:r0 + rb, :] = t
                for g8 in range(rb // SUBLANES):
                    macc = jnp.maximum(macc, t[g8 * SUBLANES:(g8 + 1) * SUBLANES, :])
        return jnp.max(macc, axis=0, keepdims=True)

    def stage_b(kj, t_ref, m_new, alpha):
        k0 = pl.multiple_of(jnp.minimum(kj, last) * tk, tk)
        vtc = vt_ref[:, pl.ds(k0, tk)]
        mb = jnp.broadcast_to(m_new, (SUBLANES, w))
        lacc = jnp.zeros((SUBLANES, w), F32)
        for r0 in range(0, tk, rb):
            t = t_ref[r0:r0 + rb, :]
            ps = []
            for g8 in range(rb // SUBLANES):
                p8 = jnp.exp2(t[g8 * SUBLANES:(g8 + 1) * SUBLANES, :] - mb)
                lacc = lacc + p8
                ps.append(p8)
            p_scr[r0:r0 + rb, :] = jnp.concatenate(ps, axis=0).astype(BF16)
        l_scr[...] = alpha * l_scr[...] + jnp.sum(lacc, axis=0, keepdims=True)
        acc_scr[...] = alpha * acc_scr[...] + _dot(vtc, p_scr[...])

    cm_scr[...] = stage_a(0, ta_scr)

    def body(j, carry):
        cmax1 = stage_a(2 * j + 1, tb_scr)
        m_prev = m_scr[...]
        m0 = jnp.maximum(m_prev, cm_scr[...])
        stage_b(2 * j, ta_scr, m0, jnp.exp2(m_prev - m0))
        cm_scr[...] = stage_a(2 * j + 2, ta_scr)
        m1 = jnp.maximum(m0, cmax1)
        stage_b(2 * j + 1, tb_scr, m1, jnp.exp2(m0 - m1))
        m_scr[...] = m1
        return carry

    lax.fori_loop(0, (n_chunks + 1) // 2, body, 0)
    lp = lam_ref[...]
    lam = (jnp.exp(jnp.sum(lp[0:1] * lp[1:2], axis=-1, keepdims=True))
           - jnp.exp(jnp.sum(lp[2:3] * lp[3:4], axis=-1, keepdims=True)) + lam_init)
    on = acc_scr[...] / l_scr[...]
    for g in range(4):
        o = on[:, (2 * g) * tq:(2 * g + 1) * tq] - lam * on[:, (2 * g + 1) * tq:(2 * g + 2) * tq]
        o = o * lax.rsqrt(jnp.mean(o * o, axis=0, keepdims=True) + RMS_EPS) * subln_ref[...]
        o = o * (1.0 - lam_init)
        o_ref[:, g * LANES:(g + 1) * LANES] = o.T.astype(o_ref.dtype)


def _a_prompt(q, k, vt, toep_t, lam_p, subln, lam_init, tq=256, tk=256, rb=32):
    b, t, _ = q.shape
    return pl.pallas_call(
        functools.partial(_a_prompt_kernel, tq=tq, tk=tk, lam_init=lam_init, rb=rb),
        grid=(b, A_KV, t // tq),
        in_specs=[pl.BlockSpec((None, tq, 512), lambda bi, kv, qi: (bi, qi, kv)),
                  pl.BlockSpec((None, t, LANES), lambda bi, kv, qi: (bi, 0, kv)),
                  pl.BlockSpec((None, LANES, t), lambda bi, kv, qi: (bi, kv, 0)),
                  pl.BlockSpec((TOEP_TILES, 8, BAND, BAND), lambda bi, kv, qi: (0, kv, 0, 0)),
                  pl.BlockSpec((4, HEAD_DIM), lambda bi, kv, qi: (0, 0)),
                  pl.BlockSpec((LANES, 1), lambda bi, kv, qi: (0, 0))],
        out_specs=pl.BlockSpec((None, tq, 512), lambda bi, kv, qi: (bi, qi, kv)),
        out_shape=jax.ShapeDtypeStruct((b, t, 1024), BF16),
        scratch_shapes=[pltpu.VMEM((8 * tq, LANES), BF16),
                        pltpu.VMEM((tk, 8 * tq), F32),
                        pltpu.VMEM((tk, 8 * tq), F32),
                        pltpu.VMEM((tk, 8 * tq), BF16),
                        pltpu.VMEM((1, 8 * tq), F32),
                        pltpu.VMEM((1, 8 * tq), F32),
                        pltpu.VMEM((1, 8 * tq), F32),
                        pltpu.VMEM((LANES, 8 * tq), F32)],
        compiler_params=_cparams(("parallel", "parallel", "arbitrary")),
        name="a_prompt",
    )(q, k, vt, toep_t, lam_p, subln.reshape(LANES, 1))


def _a_sample_kernel(pt_ref, q_ref, new_ref, bp_ref, bn_ref, lam_ref, subln_ref, *rest,
                     pages, tq, lam_init):
    page_refs = rest[:pages]
    o_ref = rest[pages]
    qm_scr, m_scr, l_scr, acc_scr = rest[pages + 1:]
    c = pl.program_id(1)
    low = _low_half()
    rows = 8 * tq

    @pl.when(c == 0)
    def _():
        for kv in range(A_KV):
            for g in range(4):
                qg = q_ref[:, (kv * 4 + g) * LANES:(kv * 4 + g + 1) * LANES]
                qm_scr[kv, (2 * g) * tq:(2 * g + 1) * tq, :] = jnp.where(low, qg, 0.0)
                qm_scr[kv, (2 * g + 1) * tq:(2 * g + 2) * tq, :] = jnp.where(low, 0.0, qg)
        m_scr[...] = jnp.full(m_scr.shape, NEG, F32)
        l_scr[...] = jnp.zeros(l_scr.shape, F32)
        acc_scr[...] = jnp.zeros(acc_scr.shape, F32)

    def update(kv, s, vals):
        m_prev = m_scr[kv]
        m_new = jnp.maximum(m_prev, jnp.max(s, axis=-1, keepdims=True))
        alpha = jnp.exp(m_prev - m_new)
        p = jnp.exp(s - m_new)
        l_scr[kv] = alpha * l_scr[kv] + jnp.sum(p, axis=-1, keepdims=True)
        m_scr[kv] = m_new
        acc_scr[kv] = alpha * acc_scr[kv] + _dot(p.astype(BF16), vals)

    n_slot = 2 * A_KV
    for kv in range(A_KV):
        qm = qm_scr[kv].astype(BF16)
        keys = jnp.concatenate([pr[pl.ds(kv, PAGE_SIZE, stride=n_slot), :] for pr in page_refs], axis=0)
        vals = jnp.concatenate([pr[pl.ds(A_KV + kv, PAGE_SIZE, stride=n_slot), :] for pr in page_refs], axis=0)
        s = _dot_nt(qm, keys.astype(BF16)) + bp_ref[kv]
        update(kv, s, vals.astype(BF16))

    @pl.when(c == pl.num_programs(1) - 1)
    def _():
        pad = jnp.zeros((LANES - tq, LANES), F32)
        for kv in range(A_KV):
            qm = qm_scr[kv].astype(BF16)
            kn = jnp.concatenate([new_ref[:, kv * LANES:(kv + 1) * LANES], pad], axis=0)
            vn = jnp.concatenate([new_ref[:, (A_KV + kv) * LANES:(A_KV + kv + 1) * LANES], pad], axis=0)
            s = _dot_nt(qm, kn.astype(BF16)) + bn_ref[kv]
            update(kv, s, vn.astype(BF16))
            acc = acc_scr[kv]
            l = l_scr[kv]
            for g in range(4):
                r0, r1 = (2 * g) * tq, (2 * g + 1) * tq
                o = _diff_finalize(acc[r0:r0 + tq], l[r0:r0 + tq], acc[r1:r1 + tq], l[r1:r1 + tq],
                                   lam_ref, subln_ref, lam_init)
                o_ref[:, (kv * 4 + g) * LANES:(kv * 4 + g + 1) * LANES] = o.astype(o_ref.dtype)


def _a_sample(q, new_kv, cache, layer, page_table, bias_past, bias_new, lam_p, subln, lam_init,
              pages=32):
    db, tq, _ = q.shape
    n_pages = page_table.shape[1]
    n_chunks = n_pages // pages
    cache2 = cache.reshape(cache.shape[0], cache.shape[1], PAGE_SIZE * 2 * A_KV, LANES)
    rows = 8 * tq

    def page_spec(j):
        return pl.BlockSpec((None, None, PAGE_SIZE * 2 * A_KV, LANES),
                            lambda bi, c, pt: (layer, pt[bi * n_pages + c * pages + j], 0, 0))

    grid_spec = pltpu.PrefetchScalarGridSpec(
        num_scalar_prefetch=1,
        grid=(db, n_chunks),
        in_specs=[pl.BlockSpec((None, tq, 1024), lambda bi, c, pt: (bi, 0, 0)),
                  pl.BlockSpec((None, tq, 512), lambda bi, c, pt: (bi, 0, 0)),
                  pl.BlockSpec((A_KV, rows, pages * PAGE_SIZE), lambda bi, c, pt: (0, 0, c)),
                  pl.BlockSpec((A_KV, rows, LANES), lambda bi, c, pt: (0, 0, 0)),
                  pl.BlockSpec((4, HEAD_DIM), lambda bi, c, pt: (0, 0)),
                  pl.BlockSpec((1, LANES), lambda bi, c, pt: (0, 0))]
                 + [page_spec(j) for j in range(pages)],
        out_specs=pl.BlockSpec((None, tq, 1024), lambda bi, c, pt: (bi, 0, 0)),
        scratch_shapes=[pltpu.VMEM((A_KV, rows, LANES), F32),
                        pltpu.VMEM((A_KV, rows, 1), F32),
                        pltpu.VMEM((A_KV, rows, 1), F32),
                        pltpu.VMEM((A_KV, rows, LANES), F32)])
    return pl.pallas_call(
        functools.partial(_a_sample_kernel, pages=pages, tq=tq, lam_init=lam_init),
        grid_spec=grid_spec,
        out_shape=jax.ShapeDtypeStruct((db, tq, 1024), F32),
        compiler_params=_cparams(("parallel", "arbitrary")),
        name="a_sample",
    )(page_table.reshape(-1), q, new_kv, bias_past, bias_new, lam_p, subln.reshape(1, LANES),
      *([cache2] * pages))


def _band_kernel(*refs, n_kv, use_sink, want_lse):
    q_ref, kp_ref, kc_ref, vp_ref, vc_ref, bias_ref = refs[:6]
    pos = 6
    sink_ref = None
    if use_sink:
        sink_ref = refs[pos]
        pos += 1
    o_ref = refs[pos]
    lse_ref = refs[pos + 1] if want_lse else None
    i = pl.program_id(2)
    grp = N_HEADS // n_kv
    low = _low_half()
    col = lax.broadcasted_iota(jnp.int32, (1, 2 * BAND), 1)
    prev_ok = (col >= BAND) | (i > 0)
    lane_id = lax.broadcasted_iota(jnp.int32, (1, LANES), 1)
    lse_acc = jnp.zeros((BAND, LANES), F32)
    for kv in range(n_kv):
        k2 = jnp.concatenate([kp_ref[:, kv * LANES:(kv + 1) * LANES], kc_ref[:, kv * LANES:(kv + 1) * LANES]], axis=0)
        v2 = jnp.concatenate([vp_ref[:, kv * LANES:(kv + 1) * LANES], vc_ref[:, kv * LANES:(kv + 1) * LANES]], axis=0)
        for hp in range(grp // 2):
            c = kv * (grp // 2) + hp
            qp = q_ref[:, c * LANES:(c + 1) * LANES]
            outs = []
            for par in range(2):
                h = 2 * c + par
                zero = jnp.zeros_like(qp)
                qm = jnp.where(low, qp, zero) if par == 0 else jnp.where(low, zero, qp)
                s = jnp.where(prev_ok, _dot_nt(qm, k2) + bias_ref[h], NEG)
                m = jnp.max(s, axis=-1, keepdims=True)
                if use_sink:
                    m = jnp.maximum(m, sink_ref[h])
                p = jnp.exp(s - m)
                l = jnp.sum(p, axis=-1, keepdims=True)
                if use_sink:
                    l = l + jnp.exp(sink_ref[h] - m)
                outs.append(_dot(p.astype(BF16), v2) / l)
                if want_lse:
                    lse_acc = jnp.where(lane_id == h, m + jnp.log(l), lse_acc)
            o_ref[:, c * LANES:(c + 1) * LANES] = jnp.where(low, outs[0], outs[1]).astype(o_ref.dtype)
    if want_lse:
        lse_ref[...] = lse_acc


def _band_prompt(q, kd, vd, bias, sinks, grp_idx, n_grp, n_kv, dil, want_lse, out_dtype):
    b, t, _ = q.shape
    ts = t // dil
    nb = ts // BAND
    ck = n_kv * LANES
    qv = q.reshape(b, ts, dil * n_grp * 1024)
    kv_ = kd.reshape(b, ts, dil * n_grp * ck)
    vv = vd.reshape(b, ts, dil * n_grp * ck)
    cur = lambda bi, r, i: (bi, i, r * n_grp + grp_idx)
    prev = lambda bi, r, i: (bi, jnp.maximum(i - 1, 0), r * n_grp + grp_idx)
    in_specs = [pl.BlockSpec((None, BAND, 1024), cur),
                pl.BlockSpec((None, BAND, ck), prev),
                pl.BlockSpec((None, BAND, ck), cur),
                pl.BlockSpec((None, BAND, ck), prev),
                pl.BlockSpec((None, BAND, ck), cur),
                pl.BlockSpec((N_HEADS, BAND, 2 * BAND), lambda bi, r, i: (0, 0, 0))]
    args = [qv, kv_, kv_, vv, vv, bias]
    if sinks is not None:
        in_specs.append(pl.BlockSpec(memory_space=pltpu.SMEM))
        args.append(sinks)
    out_specs = [pl.BlockSpec((None, BAND, 1024), lambda bi, r, i: (bi, i, r))]
    out_shape = [jax.ShapeDtypeStruct((b, ts, dil * 1024), out_dtype)]
    if want_lse:
        out_specs.append(pl.BlockSpec((None, BAND, LANES), lambda bi, r, i: (bi, i, r)))
        out_shape.append(jax.ShapeDtypeStruct((b, ts, dil * LANES), F32))
    res = pl.pallas_call(
        functools.partial(_band_kernel, n_kv=n_kv, use_sink=sinks is not None, want_lse=want_lse),
        grid=(b, dil, nb),
        in_specs=in_specs,
        out_specs=out_specs,
        out_shape=out_shape,
        compiler_params=_cparams(("parallel", "parallel", "arbitrary")),
        name=f"band_d{dil}",
    )(*args)
    o = res[0].reshape(b, t, 1024)
    if want_lse:
        return o, res[1].reshape(b, t, LANES)
    return o


def _decode_kernel(*refs, n_grp, n_kv, past_lens, use_sink, tq):
    q_ref, new_ref = refs[:2]
    pos = 2
    st_refs = refs[pos:pos + n_grp]; pos += n_grp
    bp_refs = refs[pos:pos + n_grp]; pos += n_grp
    bn_refs = refs[pos:pos + n_grp]; pos += n_grp
    sink_ref = None
    if use_sink:
        sink_ref = refs[pos]; pos += 1
    o_ref = refs[pos]
    width = 2 * n_kv * HEAD_DIM
    n_chunks = n_kv // 2
    heads_per_chunk = N_HEADS // n_chunks
    grp_heads = N_HEADS // n_kv
    low = _low_half()
    pad = jnp.zeros((LANES - tq, LANES), F32)
    for c in range(n_chunks):
        parts = []
        for g in range(n_grp):
            qs = jnp.concatenate(
                [q_ref[:, (g * N_HEADS + c * heads_per_chunk + hh) * LANES:(g * N_HEADS + c * heads_per_chunk + hh + 1) * LANES]
                 for hh in range(heads_per_chunk)], axis=0).astype(BF16)
            st = st_refs[g]
            kt = st[0, c * LANES:(c + 1) * LANES, :].astype(BF16)
            vt = st[1, c * LANES:(c + 1) * LANES, :].astype(BF16)
            kn = jnp.concatenate([new_ref[:, g * width + c * LANES:g * width + (c + 1) * LANES], pad], axis=0).astype(BF16)
            vn = jnp.concatenate([new_ref[:, g * width + n_kv * HEAD_DIM + c * LANES:g * width + n_kv * HEAD_DIM + (c + 1) * LANES], pad], axis=0).astype(BF16)
            sp = _dot(qs, kt) + bp_refs[g][c]
            sn = _dot_nt(qs, kn) + bn_refs[g][c]
            m = jnp.maximum(jnp.max(sp, axis=-1, keepdims=True), jnp.max(sn, axis=-1, keepdims=True))
            if use_sink:
                m = jnp.maximum(m, sink_ref[c])
            pp = jnp.exp(sp - m)
            pn = jnp.exp(sn - m)
            l = jnp.sum(pp, axis=-1, keepdims=True) + jnp.sum(pn, axis=-1, keepdims=True)
            if use_sink:
                l = l + jnp.exp(sink_ref[c] - m)
            acc = _dot_nt(pp.astype(BF16), vt) + _dot(pn.astype(BF16), vn)
            parts.append((m, l, acc))
        m_all = parts[0][0]
        for m, _, _ in parts[1:]:
            m_all = jnp.maximum(m_all, m)
        num = 0.0
        den = 0.0
        for m, l, acc in parts:
            w = jnp.exp(m - m_all)
            num = num + acc * w
            den = den + l * w
        o = num / den
        for hh in range(heads_per_chunk):
            h = c * heads_per_chunk + hh
            odd = (h // grp_heads) % 2
            oh = o[hh * tq:(hh + 1) * tq, :]
            o_ref[:, h * LANES:(h + 1) * LANES] = (jnp.where(low, 0.0, oh) if odd else jnp.where(low, oh, 0.0)).astype(o_ref.dtype)


def _decode(q, new_kv, states, layer, bias_past, bias_new, sinks_rows, n_kv):
    db, tq, _ = q.shape
    n_grp = len(states)
    width = 2 * n_kv * HEAD_DIM
    n_chunks = n_kv // 2
    rows = tq * N_HEADS // n_chunks
    past_lens = tuple(s.shape[2] for s in states)
    st_t = [jnp.transpose(s, (0, 1, 3, 4, 5, 2)).reshape(s.shape[0], db, 2, n_kv * HEAD_DIM, s.shape[2])
            for s in states]
    in_specs = [pl.BlockSpec((None, tq, q.shape[2]), lambda bi: (bi, 0, 0)),
                pl.BlockSpec((None, tq, n_grp * width), lambda bi: (bi, 0, 0))]
    in_specs += [pl.BlockSpec((None, None, 2, n_kv * HEAD_DIM, lp), lambda bi: (layer, bi, 0, 0, 0))
                 for lp in past_lens]
    in_specs += [pl.BlockSpec((n_chunks, rows, lp), lambda bi: (0, 0, 0)) for lp in past_lens]
    in_specs += [pl.BlockSpec((n_chunks, rows, LANES), lambda bi: (0, 0, 0)) for _ in past_lens]
    args = [q, new_kv] + st_t + list(bias_past) + list(bias_new)
    if sinks_rows is not None:
        in_specs.append(pl.BlockSpec((n_chunks, rows, 1), lambda bi: (0, 0, 0)))
        args.append(sinks_rows)
    return pl.pallas_call(
        functools.partial(_decode_kernel, n_grp=n_grp, n_kv=n_kv, past_lens=past_lens,
                          use_sink=sinks_rows is not None, tq=tq),
        grid=(db,),
        in_specs=in_specs,
        out_specs=pl.BlockSpec((None, tq, N_HEADS * LANES), lambda bi: (bi, 0, 0)),
        out_shape=jax.ShapeDtypeStruct((db, tq, N_HEADS * LANES), F32),
        compiler_params=_cparams(("parallel",)),
        name=f"decode_kv{n_kv}",
    )(*args)


def _ffn_kernel(*refs, n_attn, sample, tiles_per_seq, row_tile, ff_chunk, tq):
    x_ref = refs[0]
    pos = 1
    if n_attn == 1:
        o_in = refs[pos]; pos += 1
    else:
        o_refs = refs[pos:pos + n_attn]; pos += n_attn
        lse_refs = refs[pos:pos + n_attn]; pos += n_attn
    wo_ref, g_ref, win_ref, cw_ref, cb_ref, wout_ref = refs[pos:pos + 6]; pos += 6
    if sample:
        fix1_ref, fix2_ref = refs[pos:pos + 2]; pos += 2
    xo_ref, gate_ref = refs[pos:pos + 2]; pos += 2
    gp_scr, halo_scr = refs[pos:pos + 2]
    i = pl.program_id(0)
    low = _low_half()

    @pl.when(i == 0)
    def _():
        halo_scr[...] = jnp.zeros(halo_scr.shape, F32)

    if n_attn == 1:
        ob = o_in[...]
    else:
        lses = [r[...] for r in lse_refs]
        m_all = lses[0]
        for z in lses[1:]:
            m_all = jnp.maximum(m_all, z)
        es = [jnp.exp(z - m_all) for z in lses]
        den = es[0]
        for e in es[1:]:
            den = den + e
        ws = [e / den for e in es]
        chunks = []
        for c in range(1024 // LANES):
            acc = None
            for g in range(n_attn):
                wsel = jnp.where(low, ws[g][:, 2 * c:2 * c + 1], ws[g][:, 2 * c + 1:2 * c + 2])
                term = wsel * o_refs[g][:, c * LANES:(c + 1) * LANES]
                acc = term if acc is None else acc + term
            chunks.append(acc.astype(BF16))
        ob = jnp.concatenate(chunks, axis=1)
    x1 = x_ref[...] + _dot(ob, wo_ref[...])
    h = x1 * lax.rsqrt(jnp.mean(x1 * x1, axis=-1, keepdims=True) + RMS_EPS) * g_ref[...]
    hb = h.astype(BF16)

    if sample:
        t_in_seq = lax.broadcasted_iota(jnp.int32, (row_tile, 1), 0) & (tq - 1)
    y = jnp.zeros((row_tile, D_MODEL), F32)
    for f in range(D_FF // ff_chunk):
        f0 = f * ff_chunk
        gate = _dot(hb, win_ref[:, f0:f0 + ff_chunk])
        up = _dot(hb, win_ref[:, D_FF + f0:D_FF + f0 + ff_chunk])
        if sample:
            gp_scr[0:SUBLANES, :] = jnp.zeros((SUBLANES, ff_chunk), F32)
        else:
            first = (i % tiles_per_seq) == 0
            gp_scr[0:SUBLANES, :] = jnp.where(first, 0.0, halo_scr[f])
        gp_scr[SUBLANES:SUBLANES + row_tile, :] = gate
        g2 = gp_scr[SUBLANES - 2:SUBLANES - 2 + row_tile, :]
        g1 = gp_scr[SUBLANES - 1:SUBLANES - 1 + row_tile, :]
        if sample:
            g2 = jnp.where(t_in_seq >= 2, g2, fix2_ref[:, f0:f0 + ff_chunk])
            g1 = jnp.where(t_in_seq >= 1, g1, fix1_ref[:, f0:f0 + ff_chunk])
        gc = (cb_ref[:, f0:f0 + ff_chunk] + cw_ref[0:1, f0:f0 + ff_chunk] * g2
              + cw_ref[1:2, f0:f0 + ff_chunk] * g1 + cw_ref[2:3, f0:f0 + ff_chunk] * gate)
        act = gc * (1.0 / (1.0 + jnp.exp(-gc))) * up
        y = y + _dot(act.astype(BF16), wout_ref[f0:f0 + ff_chunk, :])
        if sample:
            gate_ref[:, f0:f0 + ff_chunk] = gate
        else:
            tail = gate[row_tile - SUBLANES:row_tile, :]
            halo_scr[f] = tail
            gate_ref[:, f0:f0 + ff_chunk] = tail
    xo_ref[...] = x1 + y


def _ffn(x, attn, lses, w_o, gain, w_in, conv_w, conv_b, w_out, fix, seq_len, row_tile=512,
         ff_chunk=1408):
    rows = x.shape[0]
    sample = fix is not None
    n_attn = len(attn)
    row_spec = lambda w: pl.BlockSpec((row_tile, w), lambda i: (i, 0))
    full = lambda a: pl.BlockSpec(a.shape, lambda i: (0,) * a.ndim, pipeline_mode=pl.Buffered(1))
    in_specs = [row_spec(D_MODEL)] + [row_spec(a.shape[1]) for a in attn]
    args = [x] + list(attn)
    if n_attn > 1:
        in_specs += [row_spec(LANES) for _ in lses]
        args += list(lses)
    consts = [w_o.astype(BF16), gain.reshape(1, D_MODEL), w_in.astype(BF16), conv_w,
              conv_b.reshape(1, D_FF), w_out.astype(BF16)]
    in_specs += [full(a) for a in consts]
    args += consts
    if sample:
        in_specs += [row_spec(D_FF), row_spec(D_FF)]
        args += list(fix)
        gate_spec = row_spec(D_FF)
        gate_shape = jax.ShapeDtypeStruct((rows, D_FF), F32)
        tiles_per_seq = 1
        tq = seq_len
    else:
        gate_spec = pl.BlockSpec((None, SUBLANES, D_FF), lambda i: (i, 0, 0))
        gate_shape = jax.ShapeDtypeStruct((rows // row_tile, SUBLANES, D_FF), F32)
        tiles_per_seq = seq_len // row_tile
        tq = 1
    return pl.pallas_call(
        functools.partial(_ffn_kernel, n_attn=n_attn, sample=sample, tiles_per_seq=tiles_per_seq,
                          row_tile=row_tile, ff_chunk=ff_chunk, tq=tq),
        grid=(rows // row_tile,),
        in_specs=in_specs,
        out_specs=[row_spec(D_MODEL), gate_spec],
        out_shape=[jax.ShapeDtypeStruct((rows, D_MODEL), F32), gate_shape],
        scratch_shapes=[pltpu.VMEM((SUBLANES + row_tile, ff_chunk), F32),
                        pltpu.VMEM((D_FF // ff_chunk, SUBLANES, ff_chunk), F32)],
        compiler_params=_cparams(("arbitrary",)),
        name="ffn_sample" if sample else "ffn_prompt",
    )(*args)


def _expand_wo_sample(w_o, n_kv):
    grp_heads = N_HEADS // n_kv
    src = []
    for h in range(N_HEADS):
        rows = _cols(h * HEAD_DIM, HEAD_DIM)
        src += (_zeros(HEAD_DIM) + rows) if (h // grp_heads) % 2 else (rows + _zeros(HEAD_DIM))
    src = np.asarray(src)
    return jnp.where(jnp.asarray(src >= 0)[:, None], jnp.take(w_o, jnp.asarray(np.maximum(src, 0)), axis=0), 0.0)


def kernel(x_prompt, x_sample, cache_kv_a, state_kv_b, state_kv_c1, state_kv_c2, state_kv_c3, state_conv_ffn, page_table, rel_bias_table, norm_mix, norm_ffn, w_in_a, q_norm_a, k_norm_a, lambda_a, subln_a, w_o_a, w_in_b, q_norm_b, k_norm_b, sinks_b, w_o_b, w_in_c, q_norm_c, k_norm_c, w_o_c, w_ffn_in, conv_ffn_w, conv_ffn_b, w_ffn_out):
    bp, tp, _ = x_prompt.shape
    bs, ts, _ = x_sample.shape
    depth = norm_mix.shape[0]
    past_len = page_table.shape[1] * PAGE_SIZE
    xp = x_prompt.reshape(bp * tp, D_MODEL)
    xs = x_sample.reshape(bs * ts, D_MODEL)
    state_c = (state_kv_c1, state_kv_c2, state_kv_c3)

    toep = _toep_table(rel_bias_table)
    band_bias = {d: _band_bias_table(rel_bias_table, d) for d in (1, 4, 16)}
    a_cols = [[kv * 8 + j for j in range(8)] for kv in range(A_KV)]
    a_bias_past, a_bias_new = _decode_bias_table(rel_bias_table, a_cols, past_len, past_len + ts, 1, ts)
    b_cols = [list(range(N_HEADS))]
    b_len = state_kv_b.shape[2]
    b_bias_past, b_bias_new = _decode_bias_table(rel_bias_table, b_cols, b_len, B_WINDOW, 1, ts)
    c_cols = [list(range(8)), list(range(8, 16))]
    c_bias = [_decode_bias_table(rel_bias_table, c_cols, state_c[g].shape[2], w, d, ts)
              for g, (w, d) in enumerate(C_PAIRS)]

    kv_a_p, kv_a_s, kv_b_p, kv_b_s = [], [], [], []
    kv_c_p = [[] for _ in C_PAIRS]
    kv_c_s = [[] for _ in C_PAIRS]
    conv_p, conv_s = [], []
    for i in range(depth):
        n = i // N_MIXERS
        if i % N_MIXERS == 0:
            lam_init = 0.8 - 0.6 * math.exp(-0.3 * i)
            q, kvn, k, vt = _run_proj(xp, norm_mix[i], w_in_a[n], q_norm_a[n], k_norm_a[n], _layout_a(True), 512,
                                      q_scale=LOG2E, t_cols=_cols(1280, 256), seq_len=tp)
            op = _a_prompt(q.reshape(bp, tp, 1024), k.reshape(bp, tp, 256), vt, toep, lambda_a[n], subln_a[n],
                           lam_init)
            attn_p, lse_p = [op.reshape(bp * tp, 1024)], None
            kv_a_p.append(kvn.reshape(bp, tp, 2, A_KV, 2 * HEAD_DIM))
            qs, kvs = _run_proj(xs, norm_mix[i], w_in_a[n], q_norm_a[n], k_norm_a[n], _layout_a(False), 512)
            os_ = _a_sample(qs.reshape(bs, ts, 1024), kvs.reshape(bs, ts, 512), cache_kv_a, n, page_table,
                            a_bias_past, a_bias_new, lambda_a[n], subln_a[n], lam_init)
            attn_s = [os_.reshape(bs * ts, 1024).astype(BF16)]
            kv_a_s.append(kvs.reshape(bs, ts, 2, A_KV, 2 * HEAD_DIM))
            w_o_p = w_o_a[n]
            w_o_s = w_o_a[n]
        elif i % N_MIXERS == 1:
            q, kvn, kd, vd = _run_proj(xp, norm_mix[i], w_in_b[n], q_norm_b[n], k_norm_b[n],
                                       _layout_band_prompt(1, B_KV), 512)
            op = _band_prompt(q.reshape(bp, tp, 1024), kd.reshape(bp, tp, B_KV * LANES),
                              vd.reshape(bp, tp, B_KV * LANES), band_bias[1], sinks_b[n], 0, 1, B_KV, 1,
                              False, BF16)
            attn_p, lse_p = [op.reshape(bp * tp, 1024)], None
            kvn = kvn.reshape(bp, tp, 2, B_KV, HEAD_DIM)
            kv_b_p.append(kvn[:, tp - min(B_WINDOW, tp):])
            qs, kvs = _run_proj(xs, norm_mix[i], w_in_b[n], q_norm_b[n], k_norm_b[n],
                                _layout_band_sample(1, B_KV), 512)
            sink_rows = jnp.repeat(sinks_b[n], ts).reshape(1, N_HEADS * ts, 1)
            os_ = _decode(qs.reshape(bs, ts, -1), kvs.reshape(bs, ts, -1), [state_kv_b], n,
                          [b_bias_past], [b_bias_new], sink_rows, B_KV)
            attn_s = [os_.reshape(bs * ts, N_HEADS * LANES).astype(BF16)]
            kv_b_s.append(jnp.concatenate([state_kv_b[n][:, ts:], kvs.reshape(bs, ts, 2, B_KV, HEAD_DIM)], axis=1))
            w_o_p = w_o_b[n]
            w_o_s = _expand_wo_sample(w_o_b[n], B_KV)
        else:
            q, kvn, kd, vd = _run_proj(xp, norm_mix[i], w_in_c[n], q_norm_c[n], k_norm_c[n],
                                       _layout_band_prompt(3, C_KV), 256)
            q = q.reshape(bp, tp, 3 * 1024)
            kd = kd.reshape(bp, tp, 3 * C_KV * LANES)
            vd = vd.reshape(bp, tp, 3 * C_KV * LANES)
            attn_p, lse_p = [], []
            for g, (w, d) in enumerate(C_PAIRS):
                o_g, lse_g = _band_prompt(q, kd, vd, band_bias[d], None, g, 3, C_KV, d, True, F32)
                attn_p.append(o_g.reshape(bp * tp, 1024))
                lse_p.append(lse_g.reshape(bp * tp, LANES))
            kvn = kvn.reshape(bp, tp, 3, 2, C_KV, HEAD_DIM)
            for g, (w, d) in enumerate(C_PAIRS):
                kv_c_p[g].append(kvn[:, tp - min(w, tp):, g])
            qs, kvs = _run_proj(xs, norm_mix[i], w_in_c[n], q_norm_c[n], k_norm_c[n],
                                _layout_band_sample(3, C_KV), 256)
            os_ = _decode(qs.reshape(bs, ts, -1), kvs.reshape(bs, ts, -1), list(state_c), n,
                          [cb[0] for cb in c_bias], [cb[1] for cb in c_bias], None, C_KV)
            attn_s = [os_.reshape(bs * ts, N_HEADS * LANES).astype(BF16)]
            kvs = kvs.reshape(bs, ts, 3, 2, C_KV, HEAD_DIM)
            for g in range(3):
                kv_c_s[g].append(jnp.concatenate([state_c[g][n][:, ts:], kvs[:, :, g]], axis=1))
            w_o_p = w_o_c[n]
            w_o_s = _expand_wo_sample(w_o_c[n], C_KV)

        xp, gate_tail = _ffn(xp, attn_p, lse_p, w_o_p, norm_ffn[i], w_ffn_in[i], conv_ffn_w[i],
                             conv_ffn_b[i], w_ffn_out[i], None, tp)
        tiles = tp // 512
        conv_p.append(gate_tail.reshape(bp, tiles, SUBLANES, D_FF)[:, -1, SUBLANES - (CONV_W - 1):])
        hist = state_conv_ffn[i]
        zero = jnp.zeros((bs, ts - 1, D_FF), F32)
        fix1 = jnp.concatenate([hist[:, 1:2], zero], axis=1).reshape(bs * ts, D_FF)
        fix2 = jnp.concatenate([hist[:, 0:1], hist[:, 1:2], zero[:, 1:]], axis=1).reshape(bs * ts, D_FF)
        xs, gate_s = _ffn(xs, attn_s, None, w_o_s, norm_ffn[i], w_ffn_in[i], conv_ffn_w[i],
                          conv_ffn_b[i], w_ffn_out[i], (fix1, fix2), ts, row_tile=256)
        conv_s.append(gate_s.reshape(bs, ts, D_FF)[:, ts - (CONV_W - 1):])

    return (xp.reshape(bp, tp, D_MODEL), xs.reshape(bs, ts, D_MODEL),
            jnp.stack(kv_a_p), jnp.stack(kv_a_s),
            jnp.stack(kv_b_p), jnp.stack(kv_b_s),
            jnp.stack(kv_c_p[0]), jnp.stack(kv_c_s[0]),
            jnp.stack(kv_c_p[1]), jnp.stack(kv_c_s[1]),
            jnp.stack(kv_c_p[2]), jnp.stack(kv_c_s[2]),
            jnp.stack(conv_p), jnp.stack(conv_s))
```

```python
import functools
import math

import numpy as np
import jax
import jax.numpy as jnp
from jax import lax
from jax.experimental import pallas as pl
from jax.experimental.pallas import tpu as pltpu

F32 = jnp.float32
BF16 = jnp.bfloat16

D_MODEL = 1024
HEAD_DIM = 64
LANES = 128
SUBLANES = 8
RMS_EPS = 1e-6
NEG = -1e30
LOG2E = math.log2(math.e)
N_HEADS = 16
N_BUCKETS = 32
MAX_DISTANCE = 2048
PAGE_SIZE = 128
D_FF = 2816
CONV_W = 3
N_MIXERS = 3
A_KV = 2
B_KV = 2
C_KV = 4
B_WINDOW = 128
C_PAIRS = ((128, 1), (512, 4), (2048, 16))
BAND = 128
ONES_ROWS = 16
VMEM_LIMIT = 56 * 1024 * 1024


def _bucket_thresholds():
    n = np.arange(0, 1 << 15)
    x = np.log(np.maximum(n, 1) / (N_BUCKETS // 2)) / math.log(MAX_DISTANCE / (N_BUCKETS // 2))
    large = N_BUCKETS // 2 + (x * (N_BUCKETS - N_BUCKETS // 2)).astype(np.int64)
    bucket = np.where(n < N_BUCKETS // 2, n, np.minimum(large, N_BUCKETS - 1))
    return [int(np.argmax(bucket >= b)) for b in range(N_BUCKETS)]


THR = _bucket_thresholds()
N_TOEP = -(-(THR[-1] + BAND - 1) // BAND)
TOEP_TILES = N_TOEP + 2


def _cparams(sem, vmem=VMEM_LIMIT):
    return pltpu.CompilerParams(dimension_semantics=sem, vmem_limit_bytes=vmem)


def _dot(a, b):
    return jnp.dot(a, b, preferred_element_type=F32)


def _dot_nt(a, b):
    return lax.dot_general(a, b, (((1,), (1,)), ((), ())), preferred_element_type=F32)


def _low_half():
    return lax.broadcasted_iota(jnp.int32, (1, LANES), 1) < HEAD_DIM


def _bias_of(dist, col, tab_ref, lo, hi):
    b_lo = max(b for b in range(N_BUCKETS) if THR[b] <= max(lo, 0))
    b_hi = max(b for b in range(N_BUCKETS) if THR[b] <= max(hi, 0))
    val = jnp.full(dist.shape, tab_ref[b_hi, col], F32)
    for b in range(b_hi - 1, b_lo - 1, -1):
        val = jnp.where(dist < THR[b + 1], tab_ref[b, col], val)
    return val


def _toep_kernel(tab_ref, o_ref):
    t = pl.program_id(0)
    key = lax.broadcasted_iota(jnp.int32, (BAND, BAND), 0)
    qry = lax.broadcasted_iota(jnp.int32, (BAND, BAND), 1)
    for delta in range(-1, N_TOEP + 1):
        @pl.when(t == delta + 1)
        def _(delta=delta):
            for c in range(N_HEADS):
                if delta < 0:
                    o_ref[c] = jnp.full((BAND, BAND), NEG, F32)
                elif delta == N_TOEP:
                    o_ref[c] = jnp.full((BAND, BAND), tab_ref[N_BUCKETS - 1, c] * LOG2E, F32)
                else:
                    dist = delta * BAND + qry - key
                    val = _bias_of(dist, c, tab_ref, delta * BAND - BAND + 1, delta * BAND + BAND - 1)
                    o_ref[c] = jnp.where(dist >= 0, val * LOG2E, NEG)


def _toep_table(table):
    return pl.pallas_call(
        _toep_kernel,
        grid=(TOEP_TILES,),
        in_specs=[pl.BlockSpec(memory_space=pltpu.SMEM)],
        out_specs=pl.BlockSpec((None, N_HEADS, BAND, BAND), lambda t: (t, 0, 0, 0)),
        out_shape=jax.ShapeDtypeStruct((TOEP_TILES, N_HEADS, BAND, BAND), F32),
        compiler_params=_cparams(("arbitrary",)),
        name="bias_toeplitz",
    )(table)


def _band_bias_kernel(tab_ref, o_ref, *, dil):
    row = lax.broadcasted_iota(jnp.int32, (BAND, 2 * BAND), 0)
    col = lax.broadcasted_iota(jnp.int32, (BAND, 2 * BAND), 1)
    steps = BAND + row - col
    valid = (steps >= 0) & (steps <= BAND)
    for c in range(N_HEADS):
        val = _bias_of(steps * dil, c, tab_ref, 0, BAND * dil)
        o_ref[c] = jnp.where(valid, val, NEG)


def _band_bias_table(table, dil):
    return pl.pallas_call(
        functools.partial(_band_bias_kernel, dil=dil),
        in_specs=[pl.BlockSpec(memory_space=pltpu.SMEM)],
        out_shape=jax.ShapeDtypeStruct((N_HEADS, BAND, 2 * BAND), F32),
        compiler_params=_cparams(None),
        name=f"bias_band_d{dil}",
    )(table)


def _decode_bias_kernel(tab_ref, past_ref, new_ref, *, cols, past_len, window, dil, tq):
    n_grp = len(cols)
    t_p = lax.broadcasted_iota(jnp.int32, (tq, past_len), 0)
    i_p = lax.broadcasted_iota(jnp.int32, (tq, past_len), 1)
    d_p = past_len + t_p - i_p
    ok_p = d_p <= window
    if dil > 1:
        ok_p = ok_p & ((d_p & (dil - 1)) == 0)
    t_n = lax.broadcasted_iota(jnp.int32, (tq, LANES), 0)
    i_n = lax.broadcasted_iota(jnp.int32, (tq, LANES), 1)
    d_n = t_n - i_n
    ok_n = (d_n >= 0) & (i_n < tq)
    if dil > 1:
        ok_n = ok_n & ((d_n & (dil - 1)) == 0)
    for r in range(n_grp):
        c = cols[r]
        vp = _bias_of(d_p, c, tab_ref, 1, min(window, past_len + tq - 1))
        past_ref[r * tq:(r + 1) * tq, :] = jnp.where(ok_p, vp, NEG)
        vn = _bias_of(d_n, c, tab_ref, 0, tq - 1)
        new_ref[r * tq:(r + 1) * tq, :] = jnp.where(ok_n, vn, NEG)


def _decode_bias_table(table, cols, past_len, window, dil, tq):
    outs = [pl.pallas_call(
        functools.partial(_decode_bias_kernel, cols=tuple(cc), past_len=past_len, window=window,
                          dil=dil, tq=tq),
        in_specs=[pl.BlockSpec(memory_space=pltpu.SMEM)],
        out_shape=(jax.ShapeDtypeStruct((len(cc) * tq, past_len), F32),
                   jax.ShapeDtypeStruct((len(cc) * tq, LANES), F32)),
        compiler_params=_cparams(None),
        name=f"bias_decode_L{past_len}_d{dil}",
    )(table) for cc in cols]
    return jnp.stack([o[0] for o in outs]), jnp.stack([o[1] for o in outs])


def _proj_kernel(x_ref, g_ref, w_ref, cg_ref, *rest, plan, col_chunk, transposed, dils):
    x = x_ref[...]
    h = x * lax.rsqrt(jnp.mean(x * x, axis=-1, keepdims=True) + RMS_EPS) * g_ref[...]
    hb = h.astype(BF16)
    low = _low_half()
    row_tile = x_ref.shape[0]
    dl_scr = None
    if max(dils) > 1:
        rest, dl_scr = rest[:-1], rest[-1]
    out_refs = rest
    if transposed:
        wt_ref, out_refs, ot_ref = rest[0], rest[1:-1], rest[-1]
        ot_ref[...] = _dot_nt(wt_ref[...], hb).astype(ot_ref.dtype)
    n_total = w_ref.shape[1]
    for c0 in range(0, n_total, col_chunk):
        cw = min(col_chunk, n_total - c0)
        y = _dot(hb, w_ref[:, c0:c0 + cw])
        for j in range(cw // LANES):
            src = c0 + j * LANES
            normed, dsts = plan[src // LANES]
            blk = y[:, j * LANES:(j + 1) * LANES]
            if normed:
                sq = blk * blk
                ms_lo = jnp.sum(jnp.where(low, sq, 0.0), axis=-1, keepdims=True) * (1.0 / HEAD_DIM)
                ms_hi = jnp.sum(jnp.where(low, 0.0, sq), axis=-1, keepdims=True) * (1.0 / HEAD_DIM)
                inv = jnp.where(low, lax.rsqrt(ms_lo + RMS_EPS), lax.rsqrt(ms_hi + RMS_EPS))
                blk = blk * inv * cg_ref[:, src:src + LANES]
            if any(dils[oi] > 1 for oi, _ in dsts):
                dl_scr[...] = blk
            for oi, dst in dsts:
                if dils[oi] == 1:
                    out_refs[oi][:, dst:dst + LANES] = blk.astype(out_refs[oi].dtype)
                else:
                    for r in range(dils[oi]):
                        part = dl_scr[pl.ds(r, row_tile // dils[oi], stride=dils[oi]), :]
                        out_refs[oi][r, :, dst:dst + LANES] = part.astype(out_refs[oi].dtype)


class _ProjLayout:
    def __init__(self):
        self.src = []
        self.gain = []
        self.plan = []
        self.outs = []

    def add_output(self, width, dtype, dil=1):
        self.outs.append((width, dtype, dil))
        return len(self.outs) - 1

    def add_chunk(self, src_cols, kind, dsts):
        assert len(src_cols) == LANES
        self.src.extend(src_cols)
        self.gain.extend([kind] * LANES)
        self.plan.append((kind != 0, list(dsts)))


def _cols(start, n):
    return list(range(start, start + n))


def _zeros(n):
    return [-1] * n


def _run_proj(x, norm_gain, w_in, q_gain, k_gain, layout, row_tile, col_chunk=1024, q_scale=1.0,
              t_cols=None, seq_len=None):
    rows = x.shape[0]
    src = np.asarray(layout.src)
    w_ext = jnp.where(jnp.asarray(src >= 0)[None, :], jnp.take(w_in, jnp.asarray(np.maximum(src, 0)), axis=1), 0.0)
    w_ext = w_ext.astype(BF16)
    kind = np.asarray(layout.gain)
    n_ext = len(src)
    reps = n_ext // HEAD_DIM
    cg = jnp.where(jnp.asarray(kind == 1), jnp.tile(q_gain, reps) * (HEAD_DIM ** -0.5 * q_scale),
                   jnp.where(jnp.asarray(kind == 2), jnp.tile(k_gain, reps), 1.0)).reshape(1, n_ext)
    dils = tuple(d for _, _, d in layout.outs)
    tiles = None if seq_len is None else seq_len // row_tile
    out_shape, out_specs = [], []
    for w, dt, d in layout.outs:
        if d == 1:
            out_shape.append(jax.ShapeDtypeStruct((rows, w), dt))
            out_specs.append(pl.BlockSpec((row_tile, w), lambda i: (i, 0)))
        else:
            out_shape.append(jax.ShapeDtypeStruct((rows // seq_len, d, seq_len // d, w), dt))
            out_specs.append(pl.BlockSpec((None, d, row_tile // d, w), lambda i: (i // tiles, 0, i % tiles, 0)))
    in_specs = [pl.BlockSpec((row_tile, D_MODEL), lambda i: (i, 0)),
                pl.BlockSpec((1, D_MODEL), lambda i: (0, 0)),
                pl.BlockSpec((D_MODEL, n_ext), lambda i: (0, 0), pipeline_mode=pl.Buffered(1)),
                pl.BlockSpec((1, n_ext), lambda i: (0, 0))]
    args = [x, norm_gain.reshape(1, D_MODEL), w_ext, cg]
    if t_cols is not None:
        n_t = len(t_cols)
        in_specs.append(pl.BlockSpec((n_t, D_MODEL), lambda i: (0, 0), pipeline_mode=pl.Buffered(1)))
        args.append(jnp.take(w_in, jnp.asarray(np.asarray(t_cols)), axis=1).T.astype(BF16))
        out_shape.append(jax.ShapeDtypeStruct((rows // seq_len, n_t, seq_len), BF16))
        out_specs.append(pl.BlockSpec((None, n_t, row_tile), lambda i: (i // tiles, 0, i % tiles)))
    return pl.pallas_call(
        functools.partial(_proj_kernel, plan=tuple(layout.plan), col_chunk=col_chunk,
                          transposed=t_cols is not None, dils=dils),
        grid=(rows // row_tile,),
        in_specs=in_specs,
        out_specs=out_specs,
        out_shape=out_shape,
        scratch_shapes=[pltpu.VMEM((row_tile, LANES), F32)] if max(dils) > 1 else [],
        compiler_params=_cparams(("parallel",)),
        name="proj",
    )(*args)


def _layout_a(prompt):
    lay = _ProjLayout()
    oq = lay.add_output(1024, BF16 if prompt else F32)
    okv = lay.add_output(512, F32)
    ok = lay.add_output(256, BF16) if prompt else None
    for c in range(8):
        lay.add_chunk(_cols(c * LANES, LANES), 1, [(oq, c * LANES)])
    for c in range(2):
        lay.add_chunk(_cols(1024 + c * LANES, LANES), 2,
                      [(okv, c * LANES)] + ([(ok, c * LANES)] if prompt else []))
    for c in range(2):
        lay.add_chunk(_cols(1280 + c * LANES, LANES), 0, [(okv, 256 + c * LANES)])
    return lay


def _layout_band_prompt(dils, n_kv):
    n_grp = len(dils)
    nq = n_grp * N_HEADS * HEAD_DIM
    nk = n_grp * n_kv * HEAD_DIM
    gk = n_kv * HEAD_DIM
    lay = _ProjLayout()
    okv = lay.add_output(2 * nk, F32)
    for g, d in enumerate(dils):
        oq = lay.add_output(N_HEADS * HEAD_DIM, BF16, d)
        okd = lay.add_output(n_kv * LANES, BF16, d)
        ovd = lay.add_output(n_kv * LANES, BF16, d)
        for c in range(N_HEADS * HEAD_DIM // LANES):
            lay.add_chunk(_cols(g * N_HEADS * HEAD_DIM + c * LANES, LANES), 1, [(oq, c * LANES)])
        for c in range(gk // LANES):
            lay.add_chunk(_cols(nq + g * gk + c * LANES, LANES), 2, [(okv, g * 2 * gk + c * LANES)])
        for c in range(gk // LANES):
            lay.add_chunk(_cols(nq + nk + g * gk + c * LANES, LANES), 0, [(okv, g * 2 * gk + gk + c * LANES)])
        for kv in range(n_kv):
            lay.add_chunk(_cols(nq + g * gk + kv * HEAD_DIM, HEAD_DIM) * 2, 2, [(okd, kv * LANES)])
        for kv in range(n_kv):
            lay.add_chunk(_cols(nq + nk + g * gk + kv * HEAD_DIM, HEAD_DIM) * 2, 0, [(ovd, kv * LANES)])
    return lay


def _layout_band_sample(n_grp, n_kv):
    nq = n_grp * N_HEADS * HEAD_DIM
    nk = n_grp * n_kv * HEAD_DIM
    grp_heads = N_HEADS // n_kv
    lay = _ProjLayout()
    oq = lay.add_output(n_grp * N_HEADS * LANES, F32)
    okv = lay.add_output(2 * nk, F32)
    for g in range(n_grp):
        for h in range(N_HEADS):
            qc = _cols((g * N_HEADS + h) * HEAD_DIM, HEAD_DIM)
            odd = (h // grp_heads) % 2
            lay.add_chunk(_zeros(HEAD_DIM) + qc if odd else qc + _zeros(HEAD_DIM), 1,
                          [(oq, (g * N_HEADS + h) * LANES)])
    gk = n_kv * HEAD_DIM
    for g in range(n_grp):
        for c in range(gk // LANES):
            lay.add_chunk(_cols(nq + g * gk + c * LANES, LANES), 2, [(okv, g * 2 * gk + c * LANES)])
        for c in range(gk // LANES):
            lay.add_chunk(_cols(nq + nk + g * gk + c * LANES, LANES), 0, [(okv, g * 2 * gk + gk + c * LANES)])
    return lay


def _diff_finalize(acc0, l0, acc1, l1, lam_ref, subln_ref, lam_init):
    lp = lam_ref[...]
    lam = (jnp.exp(jnp.sum(lp[0:1] * lp[1:2], axis=-1, keepdims=True))
           - jnp.exp(jnp.sum(lp[2:3] * lp[3:4], axis=-1, keepdims=True)) + lam_init)
    o = acc0 / l0 - lam * (acc1 / l1)
    o = o * lax.rsqrt(jnp.mean(o * o, axis=-1, keepdims=True) + RMS_EPS) * subln_ref[...]
    return o * (1.0 - lam_init)


def _a_prompt_kernel(q_ref, k_ref, vt_ref, toep_ref, lam_ref, subln_ref, o_ref,
                     qm_scr, ta_scr, tb_scr, p_scr, cm_scr, m_scr, acc_scr, *, tq, tk, lam_init, rb):
    qi = pl.program_id(2)
    low = _low_half()
    nh = 8
    w = nh * tq
    for g in range(4):
        qg = q_ref[:, g * LANES:(g + 1) * LANES]
        qm_scr[(2 * g) * tq:(2 * g + 1) * tq, :] = jnp.where(low, qg, jnp.zeros_like(qg))
        qm_scr[(2 * g + 1) * tq:(2 * g + 2) * tq, :] = jnp.where(low, jnp.zeros_like(qg), qg)
    m_scr[...] = jnp.full(m_scr.shape, NEG, F32)
    acc_scr[...] = jnp.zeros(acc_scr.shape, F32)
    n_chunks = (qi * tq + tq - 1) // tk + 1
    last = k_ref.shape[0] // tk - 1

    def stage_a(kj, t_ref):
        k0 = pl.multiple_of(jnp.minimum(kj, last) * tk, tk)
        s = _dot_nt(k_ref[pl.ds(k0, tk), :], qm_scr[...])
        macc = jnp.full((SUBLANES, w), NEG, F32)
        for u in range(tk // BAND):
            idx = [jnp.clip((qi * (tq // BAND) + r) - (kj * (tk // BAND) + u) + 1, 0, N_TOEP + 1)
                   for r in range(tq // BAND)]
            for b0 in range(0, BAND, rb):
                bias = jnp.concatenate([toep_ref[idx[r], h, b0:b0 + rb, :] for h in range(nh)
                                        for r in range(tq // BAND)], axis=1)
                r0 = u * BAND + b0
                t = s[r0:r0 + rb, :] + bias
                t_ref[r0:r0 + rb, :] = t
                for g8 in range(rb // SUBLANES):
                    macc = jnp.maximum(macc, t[g8 * SUBLANES:(g8 + 1) * SUBLANES, :])
        return jnp.max(macc, axis=0, keepdims=True)

    def stage_b(kj, t_ref, m_new, alpha):
        k0 = pl.multiple_of(jnp.minimum(kj, last) * tk, tk)
        vtc = jnp.concatenate([vt_ref[:, pl.ds(k0, tk)], jnp.ones((ONES_ROWS, tk), BF16)], axis=0)
        mb = jnp.broadcast_to(m_new, (SUBLANES, w))
        for r0 in range(0, tk, rb):
            t = t_ref[r0:r0 + rb, :]
            ps = []
            for g8 in range(rb // SUBLANES):
                p8 = jnp.exp2(t[g8 * SUBLANES:(g8 + 1) * SUBLANES, :] - mb)
                ps.append(p8)
            p_scr[r0:r0 + rb, :] = jnp.concatenate(ps, axis=0).astype(BF16)
        acc_scr[...] = alpha * acc_scr[...] + _dot(vtc, p_scr[...])

    cm_scr[...] = stage_a(0, ta_scr)

    def body(j, carry):
        cmax1 = stage_a(2 * j + 1, tb_scr)
        m_prev = m_scr[...]
        m0 = jnp.maximum(m_prev, cm_scr[...])
        stage_b(2 * j, ta_scr, m0, jnp.exp2(m_prev - m0))
        cm_scr[...] = stage_a(2 * j + 2, ta_scr)
        m1 = jnp.maximum(m0, cmax1)
        stage_b(2 * j + 1, tb_scr, m1, jnp.exp2(m0 - m1))
        m_scr[...] = m1
        return carry

    lax.fori_loop(0, (n_chunks + 1) // 2, body, 0)
    lp = lam_ref[...]
    lam = (jnp.exp(jnp.sum(lp[0:1] * lp[1:2], axis=-1, keepdims=True))
           - jnp.exp(jnp.sum(lp[2:3] * lp[3:4], axis=-1, keepdims=True)) + lam_init)
    on = acc_scr[0:LANES, :] / acc_scr[LANES:LANES + 1, :]
    for g in range(4):
        o = on[:, (2 * g) * tq:(2 * g + 1) * tq] - lam * on[:, (2 * g + 1) * tq:(2 * g + 2) * tq]
        o = o * lax.rsqrt(jnp.mean(o * o, axis=0, keepdims=True) + RMS_EPS) * subln_ref[...]
        o = o * (1.0 - lam_init)
        o_ref[:, g * LANES:(g + 1) * LANES] = o.T.astype(o_ref.dtype)


def _a_prompt(q, k, vt, toep_t, lam_p, subln, lam_init, tq=256, tk=256, rb=32):
    b, t, _ = q.shape
    return pl.pallas_call(
        functools.partial(_a_prompt_kernel, tq=tq, tk=tk, lam_init=lam_init, rb=rb),
        grid=(b, A_KV, t // tq),
        in_specs=[pl.BlockSpec((None, tq, 512), lambda bi, kv, qi: (bi, qi, kv)),
                  pl.BlockSpec((None, t, LANES), lambda bi, kv, qi: (bi, 0, kv)),
                  pl.BlockSpec((None, LANES, t), lambda bi, kv, qi: (bi, kv, 0)),
                  pl.BlockSpec((TOEP_TILES, 8, BAND, BAND), lambda bi, kv, qi: (0, kv, 0, 0)),
                  pl.BlockSpec((4, HEAD_DIM), lambda bi, kv, qi: (0, 0)),
                  pl.BlockSpec((LANES, 1), lambda bi, kv, qi: (0, 0))],
        out_specs=pl.BlockSpec((None, tq, 512), lambda bi, kv, qi: (bi, qi, kv)),
        out_shape=jax.ShapeDtypeStruct((b, t, 1024), BF16),
        scratch_shapes=[pltpu.VMEM((8 * tq, LANES), BF16),
                        pltpu.VMEM((tk, 8 * tq), F32),
                        pltpu.VMEM((tk, 8 * tq), F32),
                        pltpu.VMEM((tk, 8 * tq), BF16),
                        pltpu.VMEM((1, 8 * tq), F32),
                        pltpu.VMEM((1, 8 * tq), F32),
                        pltpu.VMEM((LANES + ONES_ROWS, 8 * tq), F32)],
        compiler_params=_cparams(("parallel", "parallel", "arbitrary")),
        name="a_prompt",
    )(q, k, vt, toep_t, lam_p, subln.reshape(LANES, 1))


def _a_sample_kernel(pt_ref, q_ref, new_ref, bp_ref, bn_ref, lam_ref, subln_ref, *rest,
                     pages, tq, lam_init):
    page_refs = rest[:pages]
    o_ref = rest[pages]
    qm_scr, m_scr, l_scr, acc_scr = rest[pages + 1:]
    c = pl.program_id(1)
    low = _low_half()
    rows = 8 * tq

    @pl.when(c == 0)
    def _():
        for kv in range(A_KV):
            for g in range(4):
                qg = q_ref[:, (kv * 4 + g) * LANES:(kv * 4 + g + 1) * LANES]
                qm_scr[kv, (2 * g) * tq:(2 * g + 1) * tq, :] = jnp.where(low, qg, 0.0)
                qm_scr[kv, (2 * g + 1) * tq:(2 * g + 2) * tq, :] = jnp.where(low, 0.0, qg)
        m_scr[...] = jnp.full(m_scr.shape, NEG, F32)
        l_scr[...] = jnp.zeros(l_scr.shape, F32)
        acc_scr[...] = jnp.zeros(acc_scr.shape, F32)

    def update(kv, s, vals):
        m_prev = m_scr[kv]
        m_new = jnp.maximum(m_prev, jnp.max(s, axis=-1, keepdims=True))
        alpha = jnp.exp(m_prev - m_new)
        p = jnp.exp(s - m_new)
        l_scr[kv] = alpha * l_scr[kv] + jnp.sum(p, axis=-1, keepdims=True)
        m_scr[kv] = m_new
        acc_scr[kv] = alpha * acc_scr[kv] + _dot(p.astype(BF16), vals)

    n_slot = 2 * A_KV
    for kv in range(A_KV):
        qm = qm_scr[kv].astype(BF16)
        keys = jnp.concatenate([pr[pl.ds(kv, PAGE_SIZE, stride=n_slot), :] for pr in page_refs], axis=0)
        vals = jnp.concatenate([pr[pl.ds(A_KV + kv, PAGE_SIZE, stride=n_slot), :] for pr in page_refs], axis=0)
        s = _dot_nt(qm, keys.astype(BF16)) + bp_ref[kv]
        update(kv, s, vals.astype(BF16))

    @pl.when(c == pl.num_programs(1) - 1)
    def _():
        pad = jnp.zeros((LANES - tq, LANES), F32)
        for kv in range(A_KV):
            qm = qm_scr[kv].astype(BF16)
            kn = jnp.concatenate([new_ref[:, kv * LANES:(kv + 1) * LANES], pad], axis=0)
            vn = jnp.concatenate([new_ref[:, (A_KV + kv) * LANES:(A_KV + kv + 1) * LANES], pad], axis=0)
            s = _dot_nt(qm, kn.astype(BF16)) + bn_ref[kv]
            update(kv, s, vn.astype(BF16))
            acc = acc_scr[kv]
            l = l_scr[kv]
            for g in range(4):
                r0, r1 = (2 * g) * tq, (2 * g + 1) * tq
                o = _diff_finalize(acc[r0:r0 + tq], l[r0:r0 + tq], acc[r1:r1 + tq], l[r1:r1 + tq],
                                   lam_ref, subln_ref, lam_init)
                o_ref[:, (kv * 4 + g) * LANES:(kv * 4 + g + 1) * LANES] = o.astype(o_ref.dtype)


def _a_sample(q, new_kv, cache, layer, page_table, bias_past, bias_new, lam_p, subln, lam_init,
              pages=32):
    db, tq, _ = q.shape
    n_pages = page_table.shape[1]
    n_chunks = n_pages // pages
    cache2 = cache.reshape(cache.shape[0], cache.shape[1], PAGE_SIZE * 2 * A_KV, LANES)
    rows = 8 * tq

    def page_spec(j):
        return pl.BlockSpec((None, None, PAGE_SIZE * 2 * A_KV, LANES),
                            lambda bi, c, pt: (layer, pt[bi * n_pages + c * pages + j], 0, 0))

    grid_spec = pltpu.PrefetchScalarGridSpec(
        num_scalar_prefetch=1,
        grid=(db, n_chunks),
        in_specs=[pl.BlockSpec((None, tq, 1024), lambda bi, c, pt: (bi, 0, 0)),
                  pl.BlockSpec((None, tq, 512), lambda bi, c, pt: (bi, 0, 0)),
                  pl.BlockSpec((A_KV, rows, pages * PAGE_SIZE), lambda bi, c, pt: (0, 0, c)),
                  pl.BlockSpec((A_KV, rows, LANES), lambda bi, c, pt: (0, 0, 0)),
                  pl.BlockSpec((4, HEAD_DIM), lambda bi, c, pt: (0, 0)),
                  pl.BlockSpec((1, LANES), lambda bi, c, pt: (0, 0))]
                 + [page_spec(j) for j in range(pages)],
        out_specs=pl.BlockSpec((None, tq, 1024), lambda bi, c, pt: (bi, 0, 0)),
        scratch_shapes=[pltpu.VMEM((A_KV, rows, LANES), F32),
                        pltpu.VMEM((A_KV, rows, 1), F32),
                        pltpu.VMEM((A_KV, rows, 1), F32),
                        pltpu.VMEM((A_KV, rows, LANES), F32)])
    return pl.pallas_call(
        functools.partial(_a_sample_kernel, pages=pages, tq=tq, lam_init=lam_init),
        grid_spec=grid_spec,
        out_shape=jax.ShapeDtypeStruct((db, tq, 1024), F32),
        compiler_params=_cparams(("parallel", "arbitrary")),
        name="a_sample",
    )(page_table.reshape(-1), q, new_kv, bias_past, bias_new, lam_p, subln.reshape(1, LANES),
      *([cache2] * pages))


def _band_kernel(*refs, n_kv, use_sink, want_lse):
    q_ref, kp_ref, kc_ref, vp_ref, vc_ref, bias_ref = refs[:6]
    pos = 6
    sink_ref = None
    if use_sink:
        sink_ref = refs[pos]
        pos += 1
    o_ref = refs[pos]
    lse_ref = refs[pos + 1] if want_lse else None
    i = pl.program_id(2)
    grp = N_HEADS // n_kv
    low = _low_half()
    col = lax.broadcasted_iota(jnp.int32, (1, 2 * BAND), 1)
    prev_ok = (col >= BAND) | (i > 0)
    lane_id = lax.broadcasted_iota(jnp.int32, (1, LANES), 1)
    lse_acc = jnp.zeros((BAND, LANES), F32)
    for kv in range(n_kv):
        k2 = jnp.concatenate([kp_ref[:, kv * LANES:(kv + 1) * LANES], kc_ref[:, kv * LANES:(kv + 1) * LANES]], axis=0)
        v2 = jnp.concatenate([vp_ref[:, kv * LANES:(kv + 1) * LANES], vc_ref[:, kv * LANES:(kv + 1) * LANES]], axis=0)
        for hp in range(grp // 2):
            c = kv * (grp // 2) + hp
            qp = q_ref[:, c * LANES:(c + 1) * LANES]
            outs = []
            for par in range(2):
                h = 2 * c + par
                zero = jnp.zeros_like(qp)
                qm = jnp.where(low, qp, zero) if par == 0 else jnp.where(low, zero, qp)
                s = jnp.where(prev_ok, _dot_nt(qm, k2) + bias_ref[h], NEG)
                m = jnp.max(s, axis=-1, keepdims=True)
                if use_sink:
                    m = jnp.maximum(m, sink_ref[h])
                p = jnp.exp(s - m)
                l = jnp.sum(p, axis=-1, keepdims=True)
                if use_sink:
                    l = l + jnp.exp(sink_ref[h] - m)
                outs.append(_dot(p.astype(BF16), v2) / l)
                if want_lse:
                    lse_acc = jnp.where(lane_id == h, m + jnp.log(l), lse_acc)
            o_ref[:, c * LANES:(c + 1) * LANES] = jnp.where(low, outs[0], outs[1]).astype(o_ref.dtype)
    if want_lse:
        lse_ref[...] = lse_acc


def _band_prompt(q, kd, vd, bias, sinks, n_kv, want_lse, out_dtype):
    b, dil, ts, _ = q.shape
    nb = ts // BAND
    ck = n_kv * LANES
    cur = lambda bi, r, i: (bi, r, i, 0)
    prev = lambda bi, r, i: (bi, r, jnp.maximum(i - 1, 0), 0)
    in_specs = [pl.BlockSpec((None, None, BAND, 1024), cur),
                pl.BlockSpec((None, None, BAND, ck), prev),
                pl.BlockSpec((None, None, BAND, ck), cur),
                pl.BlockSpec((None, None, BAND, ck), prev),
                pl.BlockSpec((None, None, BAND, ck), cur),
                pl.BlockSpec((N_HEADS, BAND, 2 * BAND), lambda bi, r, i: (0, 0, 0))]
    args = [q, kd, kd, vd, vd, bias]
    if sinks is not None:
        in_specs.append(pl.BlockSpec(memory_space=pltpu.SMEM))
        args.append(sinks)
    out_specs = [pl.BlockSpec((None, None, BAND, 1024), cur)]
    out_shape = [jax.ShapeDtypeStruct((b, dil, ts, 1024), out_dtype)]
    if want_lse:
        out_specs.append(pl.BlockSpec((None, None, BAND, LANES), cur))
        out_shape.append(jax.ShapeDtypeStruct((b, dil, ts, LANES), F32))
    return pl.pallas_call(
        functools.partial(_band_kernel, n_kv=n_kv, use_sink=sinks is not None, want_lse=want_lse),
        grid=(b, dil, nb),
        in_specs=in_specs,
        out_specs=out_specs,
        out_shape=out_shape,
        compiler_params=_cparams(("parallel", "parallel", "arbitrary")),
        name=f"band_d{dil}",
    )(*args)


def _decode_kernel(*refs, n_grp, n_kv, past_lens, use_sink, tq):
    q_ref, new_ref = refs[:2]
    pos = 2
    st_refs = refs[pos:pos + n_grp]; pos += n_grp
    bp_refs = refs[pos:pos + n_grp]; pos += n_grp
    bn_refs = refs[pos:pos + n_grp]; pos += n_grp
    sink_ref = None
    if use_sink:
        sink_ref = refs[pos]; pos += 1
    o_ref = refs[pos]; pos += 1
    nst_refs = refs[pos:pos + n_grp]
    width = 2 * n_kv * HEAD_DIM
    n_chunks = n_kv // 2
    heads_per_chunk = N_HEADS // n_chunks
    grp_heads = N_HEADS // n_kv
    low = _low_half()
    pad = jnp.zeros((LANES - tq, LANES), F32)
    for c in range(n_chunks):
        parts = []
        for g in range(n_grp):
            qs = jnp.concatenate(
                [q_ref[:, (g * N_HEADS + c * heads_per_chunk + hh) * LANES:(g * N_HEADS + c * heads_per_chunk + hh + 1) * LANES]
                 for hh in range(heads_per_chunk)], axis=0).astype(BF16)
            st = st_refs[g]
            kt = st[0, c * LANES:(c + 1) * LANES, :].astype(BF16)
            vt = st[1, c * LANES:(c + 1) * LANES, :].astype(BF16)
            kn = jnp.concatenate([new_ref[:, g * width + c * LANES:g * width + (c + 1) * LANES], pad], axis=0).astype(BF16)
            vn = jnp.concatenate([new_ref[:, g * width + n_kv * HEAD_DIM + c * LANES:g * width + n_kv * HEAD_DIM + (c + 1) * LANES], pad], axis=0).astype(BF16)
            sp = _dot(qs, kt) + bp_refs[g][c]
            sn = _dot_nt(qs, kn) + bn_refs[g][c]
            m = jnp.maximum(jnp.max(sp, axis=-1, keepdims=True), jnp.max(sn, axis=-1, keepdims=True))
            if use_sink:
                m = jnp.maximum(m, sink_ref[c])
            pp = jnp.exp(sp - m)
            pn = jnp.exp(sn - m)
            l = jnp.sum(pp, axis=-1, keepdims=True) + jnp.sum(pn, axis=-1, keepdims=True)
            if use_sink:
                l = l + jnp.exp(sink_ref[c] - m)
            acc = _dot_nt(pp.astype(BF16), vt) + _dot(pn.astype(BF16), vn)
            parts.append((m, l, acc))
        m_all = parts[0][0]
        for m, _, _ in parts[1:]:
            m_all = jnp.maximum(m_all, m)
        num = 0.0
        den = 0.0
        for m, l, acc in parts:
            w = jnp.exp(m - m_all)
            num = num + acc * w
            den = den + l * w
        o = num / den
        for hh in range(heads_per_chunk):
            h = c * heads_per_chunk + hh
            odd = (h // grp_heads) % 2
            oh = o[hh * tq:(hh + 1) * tq, :]
            o_ref[:, h * LANES:(h + 1) * LANES] = (jnp.where(low, 0.0, oh) if odd else jnp.where(low, oh, 0.0)).astype(o_ref.dtype)
    n_rows = n_kv * HEAD_DIM
    for g in range(n_grp):
        lp = past_lens[g]
        for kv_slot in range(2):
            c0 = g * width + kv_slot * n_rows
            new_t = jnp.concatenate([new_ref[:, c0:c0 + n_rows], jnp.zeros((LANES - tq, n_rows), F32)], axis=0).T
            for r0 in range(0, n_rows, HEAD_DIM):
                if lp > tq:
                    nst_refs[g][kv_slot, r0:r0 + HEAD_DIM, 0:lp - tq] = st_refs[g][kv_slot, r0:r0 + HEAD_DIM, tq:lp]
                nst_refs[g][kv_slot, r0:r0 + HEAD_DIM, lp - tq:lp] = new_t[r0:r0 + HEAD_DIM, 0:tq]


def _decode(q, new_kv, states, layer, bias_past, bias_new, sinks_rows, n_kv):
    db, tq, _ = q.shape
    n_grp = len(states)
    width = 2 * n_kv * HEAD_DIM
    n_chunks = n_kv // 2
    rows = tq * N_HEADS // n_chunks
    past_lens = tuple(s.shape[2] for s in states)
    st_t = [jnp.transpose(s, (0, 1, 3, 4, 5, 2)).reshape(s.shape[0], db, 2, n_kv * HEAD_DIM, s.shape[2])
            for s in states]
    in_specs = [pl.BlockSpec((None, tq, q.shape[2]), lambda bi: (bi, 0, 0)),
                pl.BlockSpec((None, tq, n_grp * width), lambda bi: (bi, 0, 0))]
    in_specs += [pl.BlockSpec((None, None, 2, n_kv * HEAD_DIM, lp), lambda bi: (layer, bi, 0, 0, 0))
                 for lp in past_lens]
    in_specs += [pl.BlockSpec((n_chunks, rows, lp), lambda bi: (0, 0, 0)) for lp in past_lens]
    in_specs += [pl.BlockSpec((n_chunks, rows, LANES), lambda bi: (0, 0, 0)) for _ in past_lens]
    args = [q, new_kv] + st_t + list(bias_past) + list(bias_new)
    if sinks_rows is not None:
        in_specs.append(pl.BlockSpec((n_chunks, rows, 1), lambda bi: (0, 0, 0)))
        args.append(sinks_rows)
    res = pl.pallas_call(
        functools.partial(_decode_kernel, n_grp=n_grp, n_kv=n_kv, past_lens=past_lens,
                          use_sink=sinks_rows is not None, tq=tq),
        grid=(db,),
        in_specs=in_specs,
        out_specs=[pl.BlockSpec((None, tq, N_HEADS * LANES), lambda bi: (bi, 0, 0))]
        + [pl.BlockSpec((None, 2, n_kv * HEAD_DIM, lp), lambda bi: (bi, 0, 0, 0)) for lp in past_lens],
        out_shape=[jax.ShapeDtypeStruct((db, tq, N_HEADS * LANES), F32)]
        + [jax.ShapeDtypeStruct((db, 2, n_kv * HEAD_DIM, lp), F32) for lp in past_lens],
        compiler_params=_cparams(("parallel",)),
        name=f"decode_kv{n_kv}",
    )(*args)
    new_states = [jnp.transpose(r.reshape(db, 2, n_kv, HEAD_DIM, lp), (0, 4, 1, 2, 3))
                  for r, lp in zip(res[1:], past_lens)]
    return res[0], new_states


def _ffn_kernel(*refs, n_attn, sample, tiles_per_seq, row_tile, ff_chunk, tq, dils):
    x_ref = refs[0]
    pos = 1
    if n_attn == 1:
        o_in = refs[pos]; pos += 1
    else:
        o_refs = refs[pos:pos + n_attn]; pos += n_attn
        lse_refs = refs[pos:pos + n_attn]; pos += n_attn
    wo_ref, g_ref, win_ref, cw_ref, cb_ref, wout_ref = refs[pos:pos + 6]; pos += 6
    if sample:
        fix1_ref, fix2_ref = refs[pos:pos + 2]; pos += 2
    xo_ref, gate_ref = refs[pos:pos + 2]; pos += 2
    gp_scr, halo_scr = refs[pos:pos + 2]; pos += 2
    il_scr = refs[pos:]
    i = pl.program_id(0)
    low = _low_half()

    @pl.when(i == 0)
    def _():
        halo_scr[...] = jnp.zeros(halo_scr.shape, F32)

    if n_attn == 1:
        ob = o_in[...]
    else:
        def natural(ref4, g, col0):
            d = dils[g]
            if d == 1:
                return ref4[0, :, col0:col0 + LANES]
            scr = il_scr[sum(1 for x in dils[:g] if x > 1)]
            for r in range(d):
                scr[pl.ds(r, row_tile // d, stride=d), :] = ref4[r, :, col0:col0 + LANES]
            return scr[...]

        lses = [natural(lse_refs[g], g, 0) for g in range(n_attn)]
        m_all = lses[0]
        for z in lses[1:]:
            m_all = jnp.maximum(m_all, z)
        es = [jnp.exp(z - m_all) for z in lses]
        den = es[0]
        for e in es[1:]:
            den = den + e
        ws = [e / den for e in es]
        chunks = []
        for c in range(1024 // LANES):
            acc = None
            for g in range(n_attn):
                wsel = jnp.where(low, ws[g][:, 2 * c:2 * c + 1], ws[g][:, 2 * c + 1:2 * c + 2])
                term = wsel * natural(o_refs[g], g, c * LANES)
                acc = term if acc is None else acc + term
            chunks.append(acc.astype(BF16))
        ob = jnp.concatenate(chunks, axis=1)
    x1 = x_ref[...] + _dot(ob, wo_ref[...])
    h = x1 * lax.rsqrt(jnp.mean(x1 * x1, axis=-1, keepdims=True) + RMS_EPS) * g_ref[...]
    hb = h.astype(BF16)

    if sample:
        t_in_seq = lax.broadcasted_iota(jnp.int32, (row_tile, 1), 0) & (tq - 1)
    y = jnp.zeros((row_tile, D_MODEL), F32)
    for f in range(D_FF // ff_chunk):
        f0 = f * ff_chunk
        gate = _dot(hb, win_ref[:, f0:f0 + ff_chunk])
        up = _dot(hb, win_ref[:, D_FF + f0:D_FF + f0 + ff_chunk])
        if sample:
            gp_scr[0:SUBLANES, :] = jnp.zeros((SUBLANES, ff_chunk), F32)
        else:
            first = (i % tiles_per_seq) == 0
            gp_scr[0:SUBLANES, :] = jnp.where(first, 0.0, halo_scr[f])
        gp_scr[SUBLANES:SUBLANES + row_tile, :] = gate
        g2 = gp_scr[SUBLANES - 2:SUBLANES - 2 + row_tile, :]
        g1 = gp_scr[SUBLANES - 1:SUBLANES - 1 + row_tile, :]
        if sample:
            g2 = jnp.where(t_in_seq >= 2, g2, fix2_ref[:, f0:f0 + ff_chunk])
            g1 = jnp.where(t_in_seq >= 1, g1, fix1_ref[:, f0:f0 + ff_chunk])
        gc = (cb_ref[:, f0:f0 + ff_chunk] + cw_ref[0:1, f0:f0 + ff_chunk] * g2
              + cw_ref[1:2, f0:f0 + ff_chunk] * g1 + cw_ref[2:3, f0:f0 + ff_chunk] * gate)
        act = gc * (1.0 / (1.0 + jnp.exp(-gc))) * up
        y = y + _dot(act.astype(BF16), wout_ref[f0:f0 + ff_chunk, :])
        if sample:
            gate_ref[:, f0:f0 + ff_chunk] = gate
        else:
            tail = gate[row_tile - SUBLANES:row_tile, :]
            halo_scr[f] = tail
            gate_ref[:, f0:f0 + ff_chunk] = tail
    xo_ref[...] = x1 + y


def _ffn(x, attn, lses, w_o, gain, w_in, conv_w, conv_b, w_out, fix, seq_len, row_tile=512,
         ff_chunk=1408):
    rows = x.shape[0]
    sample = fix is not None
    n_attn = len(attn)
    row_spec = lambda w: pl.BlockSpec((row_tile, w), lambda i: (i, 0))
    full = lambda a: pl.BlockSpec(a.shape, lambda i: (0,) * a.ndim, pipeline_mode=pl.Buffered(1))
    tiles = seq_len // row_tile
    res_spec = lambda a: pl.BlockSpec((None, a.shape[1], row_tile // a.shape[1], a.shape[3]),
                                      lambda i: (i // tiles, 0, i % tiles, 0))
    dils = (1,)
    if n_attn == 1:
        in_specs = [row_spec(D_MODEL), row_spec(attn[0].shape[1])]
        args = [x] + list(attn)
    else:
        dils = tuple(a.shape[1] for a in attn)
        in_specs = [row_spec(D_MODEL)] + [res_spec(a) for a in attn] + [res_spec(a) for a in lses]
        args = [x] + list(attn) + list(lses)
    consts = [w_o.astype(BF16), gain.reshape(1, D_MODEL), w_in.astype(BF16), conv_w,
              conv_b.reshape(1, D_FF), w_out.astype(BF16)]
    in_specs += [full(a) for a in consts]
    args += consts
    if sample:
        in_specs += [row_spec(D_FF), row_spec(D_FF)]
        args += list(fix)
        gate_spec = row_spec(D_FF)
        gate_shape = jax.ShapeDtypeStruct((rows, D_FF), F32)
        tiles_per_seq = 1
        tq = seq_len
    else:
        gate_spec = pl.BlockSpec((None, SUBLANES, D_FF), lambda i: (i, 0, 0))
        gate_shape = jax.ShapeDtypeStruct((rows // row_tile, SUBLANES, D_FF), F32)
        tiles_per_seq = seq_len // row_tile
        tq = 1
    return pl.pallas_call(
        functools.partial(_ffn_kernel, n_attn=n_attn, sample=sample, tiles_per_seq=tiles_per_seq,
                          row_tile=row_tile, ff_chunk=ff_chunk, tq=tq, dils=dils),
        grid=(rows // row_tile,),
        in_specs=in_specs,
        out_specs=[row_spec(D_MODEL), gate_spec],
        out_shape=[jax.ShapeDtypeStruct((rows, D_MODEL), F32), gate_shape],
        scratch_shapes=[pltpu.VMEM((SUBLANES + row_tile, ff_chunk), F32),
                        pltpu.VMEM((D_FF // ff_chunk, SUBLANES, ff_chunk), F32)]
        + [pltpu.VMEM((row_tile, LANES), F32) for d in dils if d > 1],
        compiler_params=_cparams(("arbitrary",)),
        name="ffn_sample" if sample else "ffn_prompt",
    )(*args)


def _expand_wo_sample(w_o, n_kv):
    grp_heads = N_HEADS // n_kv
    src = []
    for h in range(N_HEADS):
        rows = _cols(h * HEAD_DIM, HEAD_DIM)
        src += (_zeros(HEAD_DIM) + rows) if (h // grp_heads) % 2 else (rows + _zeros(HEAD_DIM))
    src = np.asarray(src)
    return jnp.where(jnp.asarray(src >= 0)[:, None], jnp.take(w_o, jnp.asarray(np.maximum(src, 0)), axis=0), 0.0)


def kernel(x_prompt, x_sample, cache_kv_a, state_kv_b, state_kv_c1, state_kv_c2, state_kv_c3, state_conv_ffn, page_table, rel_bias_table, norm_mix, norm_ffn, w_in_a, q_norm_a, k_norm_a, lambda_a, subln_a, w_o_a, w_in_b, q_norm_b, k_norm_b, sinks_b, w_o_b, w_in_c, q_norm_c, k_norm_c, w_o_c, w_ffn_in, conv_ffn_w, conv_ffn_b, w_ffn_out):
    bp, tp, _ = x_prompt.shape
    bs, ts, _ = x_sample.shape
    depth = norm_mix.shape[0]
    past_len = page_table.shape[1] * PAGE_SIZE
    xp = x_prompt.reshape(bp * tp, D_MODEL)
    xs = x_sample.reshape(bs * ts, D_MODEL)
    state_c = (state_kv_c1, state_kv_c2, state_kv_c3)

    toep = _toep_table(rel_bias_table)
    band_bias = {d: _band_bias_table(rel_bias_table, d) for d in (1, 4, 16)}
    a_cols = [[kv * 8 + j for j in range(8)] for kv in range(A_KV)]
    a_bias_past, a_bias_new = _decode_bias_table(rel_bias_table, a_cols, past_len, past_len + ts, 1, ts)
    b_cols = [list(range(N_HEADS))]
    b_len = state_kv_b.shape[2]
    b_bias_past, b_bias_new = _decode_bias_table(rel_bias_table, b_cols, b_len, B_WINDOW, 1, ts)
    c_cols = [list(range(8)), list(range(8, 16))]
    c_bias = [_decode_bias_table(rel_bias_table, c_cols, state_c[g].shape[2], w, d, ts)
              for g, (w, d) in enumerate(C_PAIRS)]

    kv_a_p, kv_a_s, kv_b_p, kv_b_s = [], [], [], []
    kv_c_p = [[] for _ in C_PAIRS]
    kv_c_s = [[] for _ in C_PAIRS]
    conv_p, conv_s = [], []
    for i in range(depth):
        n = i // N_MIXERS
        if i % N_MIXERS == 0:
            lam_init = 0.8 - 0.6 * math.exp(-0.3 * i)
            q, kvn, k, vt = _run_proj(xp, norm_mix[i], w_in_a[n], q_norm_a[n], k_norm_a[n], _layout_a(True), 512,
                                      q_scale=LOG2E, t_cols=_cols(1280, 256), seq_len=tp)
            op = _a_prompt(q.reshape(bp, tp, 1024), k.reshape(bp, tp, 256), vt, toep, lambda_a[n], subln_a[n],
                           lam_init)
            attn_p, lse_p = [op.reshape(bp * tp, 1024)], None
            kv_a_p.append(kvn.reshape(bp, tp, 2, A_KV, 2 * HEAD_DIM))
            qs, kvs = _run_proj(xs, norm_mix[i], w_in_a[n], q_norm_a[n], k_norm_a[n], _layout_a(False), 512)
            os_ = _a_sample(qs.reshape(bs, ts, 1024), kvs.reshape(bs, ts, 512), cache_kv_a, n, page_table,
                            a_bias_past, a_bias_new, lambda_a[n], subln_a[n], lam_init)
            attn_s = [os_.reshape(bs * ts, 1024).astype(BF16)]
            kv_a_s.append(kvs.reshape(bs, ts, 2, A_KV, 2 * HEAD_DIM))
            w_o_p = w_o_a[n]
            w_o_s = w_o_a[n]
        elif i % N_MIXERS == 1:
            kvn, q, kd, vd = _run_proj(xp, norm_mix[i], w_in_b[n], q_norm_b[n], k_norm_b[n],
                                       _layout_band_prompt((1,), B_KV), 512)
            op = _band_prompt(q.reshape(bp, 1, tp, 1024), kd.reshape(bp, 1, tp, B_KV * LANES),
                              vd.reshape(bp, 1, tp, B_KV * LANES), band_bias[1], sinks_b[n], B_KV, False, BF16)[0]
            attn_p, lse_p = [op.reshape(bp * tp, 1024)], None
            kvn = kvn.reshape(bp, tp, 2, B_KV, HEAD_DIM)
            kv_b_p.append(kvn[:, tp - min(B_WINDOW, tp):])
            qs, kvs = _run_proj(xs, norm_mix[i], w_in_b[n], q_norm_b[n], k_norm_b[n],
                                _layout_band_sample(1, B_KV), 512)
            sink_rows = jnp.repeat(sinks_b[n], ts).reshape(1, N_HEADS * ts, 1)
            os_, nst = _decode(qs.reshape(bs, ts, -1), kvs.reshape(bs, ts, -1), [state_kv_b], n,
                               [b_bias_past], [b_bias_new], sink_rows, B_KV)
            attn_s = [os_.reshape(bs * ts, N_HEADS * LANES).astype(BF16)]
            kv_b_s.append(nst[0])
            w_o_p = w_o_b[n]
            w_o_s = _expand_wo_sample(w_o_b[n], B_KV)
        else:
            c_dils = tuple(d for _, d in C_PAIRS)
            outs = _run_proj(xp, norm_mix[i], w_in_c[n], q_norm_c[n], k_norm_c[n],
                             _layout_band_prompt(c_dils, C_KV), 256, seq_len=tp)
            kvn = outs[0]
            attn_p, lse_p = [], []
            for g, (w, d) in enumerate(C_PAIRS):
                q, kd, vd = (a.reshape(bp, d, tp // d, a.shape[-1]) for a in outs[1 + 3 * g:4 + 3 * g])
                o_g, lse_g = _band_prompt(q, kd, vd, band_bias[d], None, C_KV, True, F32)
                attn_p.append(o_g)
                lse_p.append(lse_g)
            kvn = kvn.reshape(bp, tp, 3, 2, C_KV, HEAD_DIM)
            for g, (w, d) in enumerate(C_PAIRS):
                kv_c_p[g].append(kvn[:, tp - min(w, tp):, g])
            qs, kvs = _run_proj(xs, norm_mix[i], w_in_c[n], q_norm_c[n], k_norm_c[n],
                                _layout_band_sample(3, C_KV), 256)
            os_, nst = _decode(qs.reshape(bs, ts, -1), kvs.reshape(bs, ts, -1), list(state_c), n,
                               [cb[0] for cb in c_bias], [cb[1] for cb in c_bias], None, C_KV)
            attn_s = [os_.reshape(bs * ts, N_HEADS * LANES).astype(BF16)]
            for g in range(3):
                kv_c_s[g].append(nst[g])
            w_o_p = w_o_c[n]
            w_o_s = _expand_wo_sample(w_o_c[n], C_KV)

        xp, gate_tail = _ffn(xp, attn_p, lse_p, w_o_p, norm_ffn[i], w_ffn_in[i], conv_ffn_w[i],
                             conv_ffn_b[i], w_ffn_out[i], None, tp)
        tiles = tp // 512
        conv_p.append(gate_tail.reshape(bp, tiles, SUBLANES, D_FF)[:, -1, SUBLANES - (CONV_W - 1):])
        hist = state_conv_ffn[i]
        zero = jnp.zeros((bs, ts - 1, D_FF), F32)
        fix1 = jnp.concatenate([hist[:, 1:2], zero], axis=1).reshape(bs * ts, D_FF)
        fix2 = jnp.concatenate([hist[:, 0:1], hist[:, 1:2], zero[:, 1:]], axis=1).reshape(bs * ts, D_FF)
        xs, gate_s = _ffn(xs, attn_s, None, w_o_s, norm_ffn[i], w_ffn_in[i], conv_ffn_w[i],
                          conv_ffn_b[i], w_ffn_out[i], (fix1, fix2), ts, row_tile=256)
        conv_s.append(gate_s.reshape(bs, ts, D_FF)[:, ts - (CONV_W - 1):])

    return (xp.reshape(bp, tp, D_MODEL), xs.reshape(bs, ts, D_MODEL),
            jnp.stack(kv_a_p), jnp.stack(kv_a_s),
            jnp.stack(kv_b_p), jnp.stack(kv_b_s),
            jnp.stack(kv_c_p[0]), jnp.stack(kv_c_s[0]),
            jnp.stack(kv_c_p[1]), jnp.stack(kv_c_s[1]),
            jnp.stack(kv_c_p[2]), jnp.stack(kv_c_s[2]),
            jnp.stack(conv_p), jnp.stack(conv_s))
```

```python
import functools
import math

import numpy as np
import jax
import jax.numpy as jnp
from jax import lax
from jax.experimental import pallas as pl
from jax.experimental.pallas import tpu as pltpu

F32 = jnp.float32
BF16 = jnp.bfloat16

D_MODEL = 1024
HEAD_DIM = 64
LANES = 128
SUBLANES = 8
RMS_EPS = 1e-6
NEG = -1e30
LOG2E = math.log2(math.e)
N_HEADS = 16
N_BUCKETS = 32
MAX_DISTANCE = 2048
PAGE_SIZE = 128
D_FF = 2816
CONV_W = 3
N_MIXERS = 3
A_KV = 2
B_KV = 2
C_KV = 4
B_WINDOW = 128
C_PAIRS = ((128, 1), (512, 4), (2048, 16))
BAND = 128
ONES_ROWS = 16
VMEM_LIMIT = 56 * 1024 * 1024


def _bucket_thresholds():
    n = np.arange(0, 1 << 15)
    x = np.log(np.maximum(n, 1) / (N_BUCKETS // 2)) / math.log(MAX_DISTANCE / (N_BUCKETS // 2))
    large = N_BUCKETS // 2 + (x * (N_BUCKETS - N_BUCKETS // 2)).astype(np.int64)
    bucket = np.where(n < N_BUCKETS // 2, n, np.minimum(large, N_BUCKETS - 1))
    return [int(np.argmax(bucket >= b)) for b in range(N_BUCKETS)]


THR = _bucket_thresholds()
N_TOEP = -(-(THR[-1] + BAND - 1) // BAND)
TOEP_TILES = N_TOEP + 2


def _cparams(sem, vmem=VMEM_LIMIT):
    return pltpu.CompilerParams(dimension_semantics=sem, vmem_limit_bytes=vmem)


def _dot(a, b):
    return jnp.dot(a, b, preferred_element_type=F32)


def _dot_nt(a, b):
    return lax.dot_general(a, b, (((1,), (1,)), ((), ())), preferred_element_type=F32)


def _low_half():
    return lax.broadcasted_iota(jnp.int32, (1, LANES), 1) < HEAD_DIM


def _bias_of(dist, col, tab_ref, lo, hi):
    b_lo = max(b for b in range(N_BUCKETS) if THR[b] <= max(lo, 0))
    b_hi = max(b for b in range(N_BUCKETS) if THR[b] <= max(hi, 0))
    val = jnp.full(dist.shape, tab_ref[b_hi, col], F32)
    for b in range(b_hi - 1, b_lo - 1, -1):
        val = jnp.where(dist < THR[b + 1], tab_ref[b, col], val)
    return val


def _toep_kernel(tab_ref, o_ref):
    t = pl.program_id(0)
    key = lax.broadcasted_iota(jnp.int32, (BAND, BAND), 0)
    qry = lax.broadcasted_iota(jnp.int32, (BAND, BAND), 1)
    for delta in range(-1, N_TOEP + 1):
        @pl.when(t == delta + 1)
        def _(delta=delta):
            for c in range(N_HEADS):
                if delta < 0:
                    o_ref[c] = jnp.full((BAND, BAND), NEG, F32)
                elif delta == N_TOEP:
                    o_ref[c] = jnp.full((BAND, BAND), tab_ref[N_BUCKETS - 1, c] * LOG2E, F32)
                else:
                    dist = delta * BAND + qry - key
                    val = _bias_of(dist, c, tab_ref, delta * BAND - BAND + 1, delta * BAND + BAND - 1)
                    o_ref[c] = jnp.where(dist >= 0, val * LOG2E, NEG)


def _toep_table(table):
    return pl.pallas_call(
        _toep_kernel,
        grid=(TOEP_TILES,),
        in_specs=[pl.BlockSpec(memory_space=pltpu.SMEM)],
        out_specs=pl.BlockSpec((None, N_HEADS, BAND, BAND), lambda t: (t, 0, 0, 0)),
        out_shape=jax.ShapeDtypeStruct((TOEP_TILES, N_HEADS, BAND, BAND), F32),
        compiler_params=_cparams(("arbitrary",)),
        name="bias_toeplitz",
    )(table)


def _band_bias_kernel(tab_ref, o_ref, *, dil):
    row = lax.broadcasted_iota(jnp.int32, (BAND, 2 * BAND), 0)
    col = lax.broadcasted_iota(jnp.int32, (BAND, 2 * BAND), 1)
    steps = BAND + row - col
    valid = (steps >= 0) & (steps <= BAND)
    for c in range(N_HEADS):
        val = _bias_of(steps * dil, c, tab_ref, 0, BAND * dil)
        o_ref[c] = jnp.where(valid, val, NEG)


def _band_bias_table(table, dil):
    return pl.pallas_call(
        functools.partial(_band_bias_kernel, dil=dil),
        in_specs=[pl.BlockSpec(memory_space=pltpu.SMEM)],
        out_shape=jax.ShapeDtypeStruct((N_HEADS, BAND, 2 * BAND), F32),
        compiler_params=_cparams(None),
        name=f"bias_band_d{dil}",
    )(table)


def _decode_bias_kernel(tab_ref, past_ref, new_ref, *, cols, past_len, window, dil, tq):
    n_grp = len(cols)
    t_p = lax.broadcasted_iota(jnp.int32, (tq, past_len), 0)
    i_p = lax.broadcasted_iota(jnp.int32, (tq, past_len), 1)
    d_p = past_len + t_p - i_p
    ok_p = d_p <= window
    if dil > 1:
        ok_p = ok_p & ((d_p & (dil - 1)) == 0)
    t_n = lax.broadcasted_iota(jnp.int32, (tq, LANES), 0)
    i_n = lax.broadcasted_iota(jnp.int32, (tq, LANES), 1)
    d_n = t_n - i_n
    ok_n = (d_n >= 0) & (i_n < tq)
    if dil > 1:
        ok_n = ok_n & ((d_n & (dil - 1)) == 0)
    for r in range(n_grp):
        c = cols[r]
        vp = _bias_of(d_p, c, tab_ref, 1, min(window, past_len + tq - 1))
        past_ref[r * tq:(r + 1) * tq, :] = jnp.where(ok_p, vp, NEG)
        vn = _bias_of(d_n, c, tab_ref, 0, tq - 1)
        new_ref[r * tq:(r + 1) * tq, :] = jnp.where(ok_n, vn, NEG)


def _decode_bias_table(table, cols, past_len, window, dil, tq):
    outs = [pl.pallas_call(
        functools.partial(_decode_bias_kernel, cols=tuple(cc), past_len=past_len, window=window,
                          dil=dil, tq=tq),
        in_specs=[pl.BlockSpec(memory_space=pltpu.SMEM)],
        out_shape=(jax.ShapeDtypeStruct((len(cc) * tq, past_len), F32),
                   jax.ShapeDtypeStruct((len(cc) * tq, LANES), F32)),
        compiler_params=_cparams(None),
        name=f"bias_decode_L{past_len}_d{dil}",
    )(table) for cc in cols]
    return jnp.stack([o[0] for o in outs]), jnp.stack([o[1] for o in outs])


def _proj_kernel(x_ref, g_ref, w_ref, cg_ref, *rest, plan, col_chunk, transposed, dils):
    x = x_ref[...]
    h = x * lax.rsqrt(jnp.mean(x * x, axis=-1, keepdims=True) + RMS_EPS) * g_ref[...]
    hb = h.astype(BF16)
    low = _low_half()
    row_tile = x_ref.shape[0]
    dl_scr = None
    if max(dils) > 1:
        rest, dl_scr = rest[:-1], rest[-1]
    out_refs = rest
    if transposed:
        wt_ref, out_refs, ot_ref = rest[0], rest[1:-1], rest[-1]
        ot_ref[...] = _dot_nt(wt_ref[...], hb).astype(ot_ref.dtype)
    n_total = w_ref.shape[1]
    for c0 in range(0, n_total, col_chunk):
        cw = min(col_chunk, n_total - c0)
        y = _dot(hb, w_ref[:, c0:c0 + cw])
        for j in range(cw // LANES):
            src = c0 + j * LANES
            normed, dsts = plan[src // LANES]
            blk = y[:, j * LANES:(j + 1) * LANES]
            if normed:
                sq = blk * blk
                ms_lo = jnp.sum(jnp.where(low, sq, 0.0), axis=-1, keepdims=True) * (1.0 / HEAD_DIM)
                ms_hi = jnp.sum(jnp.where(low, 0.0, sq), axis=-1, keepdims=True) * (1.0 / HEAD_DIM)
                inv = jnp.where(low, lax.rsqrt(ms_lo + RMS_EPS), lax.rsqrt(ms_hi + RMS_EPS))
                blk = blk * inv * cg_ref[:, src:src + LANES]
            if any(dils[oi] > 1 for oi, _ in dsts):
                dl_scr[...] = blk
            for oi, dst in dsts:
                if dils[oi] == 1:
                    out_refs[oi][:, dst:dst + LANES] = blk.astype(out_refs[oi].dtype)
                else:
                    for r in range(dils[oi]):
                        part = dl_scr[pl.ds(r, row_tile // dils[oi], stride=dils[oi]), :]
                        out_refs[oi][r, :, dst:dst + LANES] = part.astype(out_refs[oi].dtype)


class _ProjLayout:
    def __init__(self):
        self.src = []
        self.gain = []
        self.plan = []
        self.outs = []

    def add_output(self, width, dtype, dil=1):
        self.outs.append((width, dtype, dil))
        return len(self.outs) - 1

    def add_chunk(self, src_cols, kind, dsts):
        assert len(src_cols) == LANES
        self.src.extend(src_cols)
        self.gain.extend([kind] * LANES)
        self.plan.append((kind != 0, list(dsts)))


def _cols(start, n):
    return list(range(start, start + n))


def _zeros(n):
    return [-1] * n


def _take_cols(w, src):
    parts, i = [], 0
    while i < len(src):
        j = i
        if src[i] < 0:
            while j < len(src) and src[j] < 0:
                j += 1
            parts.append(jnp.zeros((w.shape[0], j - i), w.dtype))
        else:
            while j + 1 < len(src) and src[j + 1] == src[j] + 1:
                j += 1
            j += 1
            parts.append(w[:, src[i]:src[i] + j - i])
        i = j
    return jnp.concatenate(parts, axis=1) if len(parts) > 1 else parts[0]


def _run_proj(x, norm_gain, w_in, q_gain, k_gain, layout, row_tile, col_chunk=1024, q_scale=1.0,
              t_cols=None, seq_len=None):
    rows = x.shape[0]
    src = np.asarray(layout.src)
    w_ext = _take_cols(w_in, list(layout.src)).astype(BF16)
    kind = np.asarray(layout.gain)
    n_ext = len(src)
    reps = n_ext // HEAD_DIM
    cg = jnp.where(jnp.asarray(kind == 1), jnp.tile(q_gain, reps) * (HEAD_DIM ** -0.5 * q_scale),
                   jnp.where(jnp.asarray(kind == 2), jnp.tile(k_gain, reps), 1.0)).reshape(1, n_ext)
    dils = tuple(d for _, _, d in layout.outs)
    tiles = None if seq_len is None else seq_len // row_tile
    out_shape, out_specs = [], []
    for w, dt, d in layout.outs:
        if d == 1:
            out_shape.append(jax.ShapeDtypeStruct((rows, w), dt))
            out_specs.append(pl.BlockSpec((row_tile, w), lambda i: (i, 0)))
        else:
            out_shape.append(jax.ShapeDtypeStruct((rows // seq_len, d, seq_len // d, w), dt))
            out_specs.append(pl.BlockSpec((None, d, row_tile // d, w), lambda i: (i // tiles, 0, i % tiles, 0)))
    in_specs = [pl.BlockSpec((row_tile, D_MODEL), lambda i: (i, 0)),
                pl.BlockSpec((1, D_MODEL), lambda i: (0, 0)),
                pl.BlockSpec((D_MODEL, n_ext), lambda i: (0, 0), pipeline_mode=pl.Buffered(1)),
                pl.BlockSpec((1, n_ext), lambda i: (0, 0))]
    args = [x, norm_gain.reshape(1, D_MODEL), w_ext, cg]
    if t_cols is not None:
        n_t = len(t_cols)
        in_specs.append(pl.BlockSpec((n_t, D_MODEL), lambda i: (0, 0), pipeline_mode=pl.Buffered(1)))
        args.append(_take_cols(w_in, list(t_cols)).T.astype(BF16))
        out_shape.append(jax.ShapeDtypeStruct((rows // seq_len, n_t, seq_len), BF16))
        out_specs.append(pl.BlockSpec((None, n_t, row_tile), lambda i: (i // tiles, 0, i % tiles)))
    return pl.pallas_call(
        functools.partial(_proj_kernel, plan=tuple(layout.plan), col_chunk=col_chunk,
                          transposed=t_cols is not None, dils=dils),
        grid=(rows // row_tile,),
        in_specs=in_specs,
        out_specs=out_specs,
        out_shape=out_shape,
        scratch_shapes=[pltpu.VMEM((row_tile, LANES), F32)] if max(dils) > 1 else [],
        compiler_params=_cparams(("parallel",)),
        name="proj",
    )(*args)


def _layout_a(prompt):
    lay = _ProjLayout()
    oq = lay.add_output(1024, BF16 if prompt else F32)
    okv = lay.add_output(512, F32)
    ok = lay.add_output(256, BF16) if prompt else None
    for c in range(8):
        lay.add_chunk(_cols(c * LANES, LANES), 1, [(oq, c * LANES)])
    for c in range(2):
        lay.add_chunk(_cols(1024 + c * LANES, LANES), 2,
                      [(okv, c * LANES)] + ([(ok, c * LANES)] if prompt else []))
    for c in range(2):
        lay.add_chunk(_cols(1280 + c * LANES, LANES), 0, [(okv, 256 + c * LANES)])
    return lay


def _layout_band_prompt(dils, n_kv):
    n_grp = len(dils)
    nq = n_grp * N_HEADS * HEAD_DIM
    nk = n_grp * n_kv * HEAD_DIM
    gk = n_kv * HEAD_DIM
    lay = _ProjLayout()
    okv = lay.add_output(2 * nk, F32)
    for g, d in enumerate(dils):
        oq = lay.add_output(N_HEADS * HEAD_DIM, BF16, d)
        okd = lay.add_output(n_kv * LANES, BF16, d)
        ovd = lay.add_output(n_kv * LANES, BF16, d)
        for c in range(N_HEADS * HEAD_DIM // LANES):
            lay.add_chunk(_cols(g * N_HEADS * HEAD_DIM + c * LANES, LANES), 1, [(oq, c * LANES)])
        for c in range(gk // LANES):
            lay.add_chunk(_cols(nq + g * gk + c * LANES, LANES), 2, [(okv, g * 2 * gk + c * LANES)])
        for c in range(gk // LANES):
            lay.add_chunk(_cols(nq + nk + g * gk + c * LANES, LANES), 0, [(okv, g * 2 * gk + gk + c * LANES)])
        for kv in range(n_kv):
            lay.add_chunk(_cols(nq + g * gk + kv * HEAD_DIM, HEAD_DIM) * 2, 2, [(okd, kv * LANES)])
        for kv in range(n_kv):
            lay.add_chunk(_cols(nq + nk + g * gk + kv * HEAD_DIM, HEAD_DIM) * 2, 0, [(ovd, kv * LANES)])
    return lay


def _layout_band_sample(n_grp, n_kv):
    nq = n_grp * N_HEADS * HEAD_DIM
    nk = n_grp * n_kv * HEAD_DIM
    grp_heads = N_HEADS // n_kv
    lay = _ProjLayout()
    oq = lay.add_output(n_grp * N_HEADS * LANES, F32)
    okv = lay.add_output(2 * nk, F32)
    for g in range(n_grp):
        for h in range(N_HEADS):
            qc = _cols((g * N_HEADS + h) * HEAD_DIM, HEAD_DIM)
            odd = (h // grp_heads) % 2
            lay.add_chunk(_zeros(HEAD_DIM) + qc if odd else qc + _zeros(HEAD_DIM), 1,
                          [(oq, (g * N_HEADS + h) * LANES)])
    gk = n_kv * HEAD_DIM
    for g in range(n_grp):
        for c in range(gk // LANES):
            lay.add_chunk(_cols(nq + g * gk + c * LANES, LANES), 2, [(okv, g * 2 * gk + c * LANES)])
        for c in range(gk // LANES):
            lay.add_chunk(_cols(nq + nk + g * gk + c * LANES, LANES), 0, [(okv, g * 2 * gk + gk + c * LANES)])
    return lay


def _diff_finalize(acc0, l0, acc1, l1, lam_ref, subln_ref, lam_init):
    lp = lam_ref[...]
    lam = (jnp.exp(jnp.sum(lp[0:1] * lp[1:2], axis=-1, keepdims=True))
           - jnp.exp(jnp.sum(lp[2:3] * lp[3:4], axis=-1, keepdims=True)) + lam_init)
    o = acc0 / l0 - lam * (acc1 / l1)
    o = o * lax.rsqrt(jnp.mean(o * o, axis=-1, keepdims=True) + RMS_EPS) * subln_ref[...]
    return o * (1.0 - lam_init)


def _a_prompt_kernel(q_ref, k_ref, vt_ref, toep_ref, lam_ref, subln_ref, o_ref,
                     qm_scr, ta_scr, tb_scr, p_scr, cm_scr, m_scr, acc_scr, *, tq, tk, lam_init, rb):
    qi = pl.program_id(2)
    low = _low_half()
    nh = 8
    w = nh * tq
    for g in range(4):
        qg = q_ref[:, g * LANES:(g + 1) * LANES]
        qm_scr[(2 * g) * tq:(2 * g + 1) * tq, :] = jnp.where(low, qg, jnp.zeros_like(qg))
        qm_scr[(2 * g + 1) * tq:(2 * g + 2) * tq, :] = jnp.where(low, jnp.zeros_like(qg), qg)
    m_scr[...] = jnp.full(m_scr.shape, NEG, F32)
    acc_scr[...] = jnp.zeros(acc_scr.shape, F32)
    n_chunks = (qi * tq + tq - 1) // tk + 1
    last = k_ref.shape[0] // tk - 1

    def stage_a(kj, t_ref):
        k0 = pl.multiple_of(jnp.minimum(kj, last) * tk, tk)
        s = _dot_nt(k_ref[pl.ds(k0, tk), :], qm_scr[...])
        macc = jnp.full((SUBLANES, w), NEG, F32)
        for u in range(tk // BAND):
            idx = [jnp.clip((qi * (tq // BAND) + r) - (kj * (tk // BAND) + u) + 1, 0, N_TOEP + 1)
                   for r in range(tq // BAND)]
            for b0 in range(0, BAND, rb):
                bias = jnp.concatenate([toep_ref[idx[r], h, b0:b0 + rb, :] for h in range(nh)
                                        for r in range(tq // BAND)], axis=1)
                r0 = u * BAND + b0
                t = s[r0:r0 + rb, :] + bias
                t_ref[r0:r0 + rb, :] = t
                for g8 in range(rb // SUBLANES):
                    macc = jnp.maximum(macc, t[g8 * SUBLANES:(g8 + 1) * SUBLANES, :])
        return jnp.max(macc, axis=0, keepdims=True)

    def stage_b(kj, t_ref, m_new, alpha):
        k0 = pl.multiple_of(jnp.minimum(kj, last) * tk, tk)
        vtc = jnp.concatenate([vt_ref[:, pl.ds(k0, tk)], jnp.ones((ONES_ROWS, tk), BF16)], axis=0)
        mb = jnp.broadcast_to(m_new, (SUBLANES, w))
        for r0 in range(0, tk, rb):
            t = t_ref[r0:r0 + rb, :]
            ps = []
            for g8 in range(rb // SUBLANES):
                p8 = jnp.exp2(t[g8 * SUBLANES:(g8 + 1) * SUBLANES, :] - mb)
                ps.append(p8)
            p_scr[r0:r0 + rb, :] = jnp.concatenate(ps, axis=0).astype(BF16)
        acc_scr[...] = alpha * acc_scr[...] + _dot(vtc, p_scr[...])

    cm_scr[...] = stage_a(0, ta_scr)

    def body(j, carry):
        cmax1 = stage_a(2 * j + 1, tb_scr)
        m_prev = m_scr[...]
        m0 = jnp.maximum(m_prev, cm_scr[...])
        stage_b(2 * j, ta_scr, m0, jnp.exp2(m_prev - m0))
        cm_scr[...] = stage_a(2 * j + 2, ta_scr)
        m1 = jnp.maximum(m0, cmax1)
        stage_b(2 * j + 1, tb_scr, m1, jnp.exp2(m0 - m1))
        m_scr[...] = m1
        return carry

    lax.fori_loop(0, (n_chunks + 1) // 2, body, 0)
    lp = lam_ref[...]
    lam = (jnp.exp(jnp.sum(lp[0:1] * lp[1:2], axis=-1, keepdims=True))
           - jnp.exp(jnp.sum(lp[2:3] * lp[3:4], axis=-1, keepdims=True)) + lam_init)
    on = acc_scr[0:LANES, :] / acc_scr[LANES:LANES + 1, :]
    for g in range(4):
        o = on[:, (2 * g) * tq:(2 * g + 1) * tq] - lam * on[:, (2 * g + 1) * tq:(2 * g + 2) * tq]
        o = o * lax.rsqrt(jnp.mean(o * o, axis=0, keepdims=True) + RMS_EPS) * subln_ref[...]
        o = o * (1.0 - lam_init)
        o_ref[:, g * LANES:(g + 1) * LANES] = o.T.astype(o_ref.dtype)


def _a_prompt(q, k, vt, toep_t, lam_p, subln, lam_init, tq=256, tk=256, rb=32):
    b, t, _ = q.shape
    return pl.pallas_call(
        functools.partial(_a_prompt_kernel, tq=tq, tk=tk, lam_init=lam_init, rb=rb),
        grid=(b, A_KV, t // tq),
        in_specs=[pl.BlockSpec((None, tq, 512), lambda bi, kv, qi: (bi, qi, kv)),
                  pl.BlockSpec((None, t, LANES), lambda bi, kv, qi: (bi, 0, kv)),
                  pl.BlockSpec((None, LANES, t), lambda bi, kv, qi: (bi, kv, 0)),
                  pl.BlockSpec((TOEP_TILES, 8, BAND, BAND), lambda bi, kv, qi: (0, kv, 0, 0)),
                  pl.BlockSpec((4, HEAD_DIM), lambda bi, kv, qi: (0, 0)),
                  pl.BlockSpec((LANES, 1), lambda bi, kv, qi: (0, 0))],
        out_specs=pl.BlockSpec((None, tq, 512), lambda bi, kv, qi: (bi, qi, kv)),
        out_shape=jax.ShapeDtypeStruct((b, t, 1024), BF16),
        scratch_shapes=[pltpu.VMEM((8 * tq, LANES), BF16),
                        pltpu.VMEM((tk, 8 * tq), F32),
                        pltpu.VMEM((tk, 8 * tq), F32),
                        pltpu.VMEM((tk, 8 * tq), BF16),
                        pltpu.VMEM((1, 8 * tq), F32),
                        pltpu.VMEM((1, 8 * tq), F32),
                        pltpu.VMEM((LANES + ONES_ROWS, 8 * tq), F32)],
        compiler_params=_cparams(("parallel", "parallel", "arbitrary")),
        name="a_prompt",
    )(q, k, vt, toep_t, lam_p, subln.reshape(LANES, 1))


def _a_sample_kernel(pt_ref, q_ref, new_ref, bp_ref, bc_ref, bn_ref, lam_ref, subln_ref, *rest,
                     pages, tq, lam_init):
    page_refs = rest[:pages]
    o_ref = rest[pages]
    qm_scr, m_scr, l_scr, acc_scr = rest[pages + 1:]
    c = pl.program_id(1)
    last = pl.num_programs(1) - 1
    low = _low_half()
    rows = 8 * tq

    @pl.when(c == 0)
    def _():
        for kv in range(A_KV):
            for g in range(4):
                qg = q_ref[:, (kv * 4 + g) * LANES:(kv * 4 + g + 1) * LANES]
                qm_scr[kv, (2 * g) * tq:(2 * g + 1) * tq, :] = jnp.where(low, qg, 0.0)
                qm_scr[kv, (2 * g + 1) * tq:(2 * g + 2) * tq, :] = jnp.where(low, 0.0, qg)
        m_scr[...] = jnp.full(m_scr.shape, NEG, F32)
        l_scr[...] = jnp.zeros(l_scr.shape, F32)
        acc_scr[...] = jnp.zeros(acc_scr.shape, F32)

    n_slot = 2 * A_KV
    is_last = c == last

    def scores(kv):
        qm = qm_scr[kv].astype(BF16)
        keys = jnp.concatenate([pr[pl.ds(kv, PAGE_SIZE, stride=n_slot), :] for pr in page_refs], axis=0)
        return _dot_nt(qm, keys.astype(BF16)) + jnp.where(is_last, bp_ref[kv], bc_ref[kv])

    def values(kv):
        return jnp.concatenate([pr[pl.ds(A_KV + kv, PAGE_SIZE, stride=n_slot), :] for pr in page_refs], axis=0).astype(BF16)

    s = jnp.concatenate([scores(kv) for kv in range(A_KV)], axis=0)
    m_prev = m_scr[...]
    m_new = jnp.maximum(m_prev, jnp.max(s, axis=-1, keepdims=True))
    alpha = jnp.exp(m_prev - m_new)
    p = jnp.exp(s - m_new)
    l_scr[...] = alpha * l_scr[...] + jnp.sum(p, axis=-1, keepdims=True)
    m_scr[...] = m_new
    pb = p.astype(BF16)
    pv = jnp.concatenate([_dot(pb[kv * rows:(kv + 1) * rows], values(kv)) for kv in range(A_KV)], axis=0)
    acc_scr[...] = alpha * acc_scr[...] + pv

    @pl.when(is_last)
    def _():
        pad = jnp.zeros((LANES - tq, LANES), F32)
        for kv in range(A_KV):
            sl = slice(kv * rows, (kv + 1) * rows)
            qm = qm_scr[kv].astype(BF16)
            kn = jnp.concatenate([new_ref[:, kv * LANES:(kv + 1) * LANES], pad], axis=0)
            vn = jnp.concatenate([new_ref[:, (A_KV + kv) * LANES:(A_KV + kv + 1) * LANES], pad], axis=0)
            s = _dot_nt(qm, kn.astype(BF16)) + bn_ref[kv]
            m_prev = m_scr[sl]
            m_new = jnp.maximum(m_prev, jnp.max(s, axis=-1, keepdims=True))
            alpha = jnp.exp(m_prev - m_new)
            p = jnp.exp(s - m_new)
            l = alpha * l_scr[sl] + jnp.sum(p, axis=-1, keepdims=True)
            acc = alpha * acc_scr[sl] + _dot(p.astype(BF16), vn.astype(BF16))
            for g in range(4):
                r0, r1 = (2 * g) * tq, (2 * g + 1) * tq
                o = _diff_finalize(acc[r0:r0 + tq], l[r0:r0 + tq], acc[r1:r1 + tq], l[r1:r1 + tq],
                                   lam_ref, subln_ref, lam_init)
                o_ref[:, (kv * 4 + g) * LANES:(kv * 4 + g + 1) * LANES] = o.astype(o_ref.dtype)


def _a_sample(q, new_kv, cache, layer, page_table, bias_past, bias_const, bias_new, lam_p, subln, lam_init,
              pages=32):
    db, tq, _ = q.shape
    n_pages = page_table.shape[1]
    n_chunks = n_pages // pages
    assert n_chunks == 1 or pages * PAGE_SIZE + 1 >= THR[-1]
    cache2 = cache.reshape(cache.shape[0], cache.shape[1], PAGE_SIZE * 2 * A_KV, LANES)
    rows = 8 * tq

    def page_spec(j):
        return pl.BlockSpec((None, None, PAGE_SIZE * 2 * A_KV, LANES),
                            lambda bi, c, pt: (layer, pt[bi * n_pages + c * pages + j], 0, 0))

    grid_spec = pltpu.PrefetchScalarGridSpec(
        num_scalar_prefetch=1,
        grid=(db, n_chunks),
        in_specs=[pl.BlockSpec((None, tq, 1024), lambda bi, c, pt: (bi, 0, 0)),
                  pl.BlockSpec((None, tq, 512), lambda bi, c, pt: (bi, 0, 0)),
                  pl.BlockSpec((A_KV, rows, pages * PAGE_SIZE), lambda bi, c, pt: (0, 0, n_chunks - 1)),
                  pl.BlockSpec((A_KV, rows, 1), lambda bi, c, pt: (0, 0, 0)),
                  pl.BlockSpec((A_KV, rows, LANES), lambda bi, c, pt: (0, 0, 0)),
                  pl.BlockSpec((4, HEAD_DIM), lambda bi, c, pt: (0, 0)),
                  pl.BlockSpec((1, LANES), lambda bi, c, pt: (0, 0))]
                 + [page_spec(j) for j in range(pages)],
        out_specs=pl.BlockSpec((None, tq, 1024), lambda bi, c, pt: (bi, 0, 0)),
        scratch_shapes=[pltpu.VMEM((A_KV, rows, LANES), F32),
                        pltpu.VMEM((A_KV * rows, 1), F32),
                        pltpu.VMEM((A_KV * rows, 1), F32),
                        pltpu.VMEM((A_KV * rows, LANES), F32)])
    return pl.pallas_call(
        functools.partial(_a_sample_kernel, pages=pages, tq=tq, lam_init=lam_init),
        grid_spec=grid_spec,
        out_shape=jax.ShapeDtypeStruct((db, tq, 1024), F32),
        compiler_params=_cparams(("parallel", "arbitrary")),
        name="a_sample",
    )(page_table.reshape(-1), q, new_kv, bias_past, bias_const, bias_new, lam_p, subln.reshape(1, LANES),
      *([cache2] * pages))


def _band_kernel(*refs, n_kv, use_sink, want_lse):
    q_ref, kp_ref, kc_ref, vp_ref, vc_ref, bias_ref = refs[:6]
    pos = 6
    sink_ref = None
    if use_sink:
        sink_ref = refs[pos]
        pos += 1
    o_ref = refs[pos]
    lse_ref = refs[pos + 1] if want_lse else None
    i = pl.program_id(2)
    grp = N_HEADS // n_kv
    low = _low_half()
    col = lax.broadcasted_iota(jnp.int32, (1, 2 * BAND), 1)
    prev_ok = (col >= BAND) | (i > 0)
    lane_id = lax.broadcasted_iota(jnp.int32, (1, LANES), 1)
    lse_acc = jnp.zeros((BAND, LANES), F32)
    for kv in range(n_kv):
        k2 = jnp.concatenate([kp_ref[:, kv * LANES:(kv + 1) * LANES], kc_ref[:, kv * LANES:(kv + 1) * LANES]], axis=0)
        v2 = jnp.concatenate([vp_ref[:, kv * LANES:(kv + 1) * LANES], vc_ref[:, kv * LANES:(kv + 1) * LANES]], axis=0)
        for hp in range(grp // 2):
            c = kv * (grp // 2) + hp
            qp = q_ref[:, c * LANES:(c + 1) * LANES]
            outs = []
            for par in range(2):
                h = 2 * c + par
                zero = jnp.zeros_like(qp)
                qm = jnp.where(low, qp, zero) if par == 0 else jnp.where(low, zero, qp)
                s = jnp.where(prev_ok, _dot_nt(qm, k2) + bias_ref[h], NEG)
                m = jnp.max(s, axis=-1, keepdims=True)
                if use_sink:
                    m = jnp.maximum(m, sink_ref[h])
                p = jnp.exp(s - m)
                l = jnp.sum(p, axis=-1, keepdims=True)
                if use_sink:
                    l = l + jnp.exp(sink_ref[h] - m)
                outs.append(_dot(p.astype(BF16), v2) / l)
                if want_lse:
                    lse_acc = jnp.where(lane_id == h, m + jnp.log(l), lse_acc)
            o_ref[:, c * LANES:(c + 1) * LANES] = jnp.where(low, outs[0], outs[1]).astype(o_ref.dtype)
    if want_lse:
        lse_ref[...] = lse_acc


def _band_prompt(q, kd, vd, bias, sinks, n_kv, want_lse, out_dtype):
    b, dil, ts, _ = q.shape
    nb = ts // BAND
    ck = n_kv * LANES
    cur = lambda bi, r, i: (bi, r, i, 0)
    prev = lambda bi, r, i: (bi, r, jnp.maximum(i - 1, 0), 0)
    in_specs = [pl.BlockSpec((None, None, BAND, 1024), cur),
                pl.BlockSpec((None, None, BAND, ck), prev),
                pl.BlockSpec((None, None, BAND, ck), cur),
                pl.BlockSpec((None, None, BAND, ck), prev),
                pl.BlockSpec((None, None, BAND, ck), cur),
                pl.BlockSpec((N_HEADS, BAND, 2 * BAND), lambda bi, r, i: (0, 0, 0))]
    args = [q, kd, kd, vd, vd, bias]
    if sinks is not None:
        in_specs.append(pl.BlockSpec(memory_space=pltpu.SMEM))
        args.append(sinks)
    out_specs = [pl.BlockSpec((None, None, BAND, 1024), cur)]
    out_shape = [jax.ShapeDtypeStruct((b, dil, ts, 1024), out_dtype)]
    if want_lse:
        out_specs.append(pl.BlockSpec((None, None, BAND, LANES), cur))
        out_shape.append(jax.ShapeDtypeStruct((b, dil, ts, LANES), F32))
    return pl.pallas_call(
        functools.partial(_band_kernel, n_kv=n_kv, use_sink=sinks is not None, want_lse=want_lse),
        grid=(b, dil, nb),
        in_specs=in_specs,
        out_specs=out_specs,
        out_shape=out_shape,
        compiler_params=_cparams(("parallel", "parallel", "arbitrary")),
        name=f"band_d{dil}",
    )(*args)


def _decode_kernel(*refs, n_grp, n_kv, past_lens, use_sink, tq):
    q_ref, new_ref = refs[:2]
    pos = 2
    st_refs = refs[pos:pos + n_grp]; pos += n_grp
    bp_refs = refs[pos:pos + n_grp]; pos += n_grp
    bn_refs = refs[pos:pos + n_grp]; pos += n_grp
    sink_ref = None
    if use_sink:
        sink_ref = refs[pos]; pos += 1
    o_ref = refs[pos]; pos += 1
    nst_refs = refs[pos:pos + n_grp]
    width = 2 * n_kv * HEAD_DIM
    n_chunks = n_kv // 2
    heads_per_chunk = N_HEADS // n_chunks
    grp_heads = N_HEADS // n_kv
    low = _low_half()
    pad = jnp.zeros((LANES - tq, LANES), F32)
    for c in range(n_chunks):
        parts = []
        for g in range(n_grp):
            qs = jnp.concatenate(
                [q_ref[:, (g * N_HEADS + c * heads_per_chunk + hh) * LANES:(g * N_HEADS + c * heads_per_chunk + hh + 1) * LANES]
                 for hh in range(heads_per_chunk)], axis=0).astype(BF16)
            st = st_refs[g]
            kt = st[0, c * LANES:(c + 1) * LANES, :].astype(BF16)
            vt = st[1, c * LANES:(c + 1) * LANES, :].astype(BF16)
            kn = jnp.concatenate([new_ref[:, g * width + c * LANES:g * width + (c + 1) * LANES], pad], axis=0).astype(BF16)
            vn = jnp.concatenate([new_ref[:, g * width + n_kv * HEAD_DIM + c * LANES:g * width + n_kv * HEAD_DIM + (c + 1) * LANES], pad], axis=0).astype(BF16)
            sp = _dot(qs, kt) + bp_refs[g][c]
            sn = _dot_nt(qs, kn) + bn_refs[g][c]
            m = jnp.maximum(jnp.max(sp, axis=-1, keepdims=True), jnp.max(sn, axis=-1, keepdims=True))
            if use_sink:
                m = jnp.maximum(m, sink_ref[c])
            pp = jnp.exp(sp - m)
            pn = jnp.exp(sn - m)
            l = jnp.sum(pp, axis=-1, keepdims=True) + jnp.sum(pn, axis=-1, keepdims=True)
            if use_sink:
                l = l + jnp.exp(sink_ref[c] - m)
            acc = _dot_nt(pp.astype(BF16), vt) + _dot(pn.astype(BF16), vn)
            parts.append((m, l, acc))
        m_all = parts[0][0]
        for m, _, _ in parts[1:]:
            m_all = jnp.maximum(m_all, m)
        num = 0.0
        den = 0.0
        for m, l, acc in parts:
            w = jnp.exp(m - m_all)
            num = num + acc * w
            den = den + l * w
        o = num / den
        for hh in range(heads_per_chunk):
            h = c * heads_per_chunk + hh
            odd = (h // grp_heads) % 2
            oh = o[hh * tq:(hh + 1) * tq, :]
            o_ref[:, h * LANES:(h + 1) * LANES] = (jnp.where(low, 0.0, oh) if odd else jnp.where(low, oh, 0.0)).astype(o_ref.dtype)
    n_rows = n_kv * HEAD_DIM
    for g in range(n_grp):
        lp = past_lens[g]
        for kv_slot in range(2):
            c0 = g * width + kv_slot * n_rows
            new_t = jnp.concatenate([new_ref[:, c0:c0 + n_rows], jnp.zeros((LANES - tq, n_rows), F32)], axis=0).T
            for r0 in range(0, n_rows, HEAD_DIM):
                if lp > tq:
                    nst_refs[g][kv_slot, r0:r0 + HEAD_DIM, 0:lp - tq] = st_refs[g][kv_slot, r0:r0 + HEAD_DIM, tq:lp]
                nst_refs[g][kv_slot, r0:r0 + HEAD_DIM, lp - tq:lp] = new_t[r0:r0 + HEAD_DIM, 0:tq]


def _decode(q, new_kv, states, layer, bias_past, bias_new, sinks_rows, n_kv):
    db, tq, _ = q.shape
    n_grp = len(states)
    width = 2 * n_kv * HEAD_DIM
    n_chunks = n_kv // 2
    rows = tq * N_HEADS // n_chunks
    past_lens = tuple(s.shape[2] for s in states)
    st_t = [jnp.transpose(s, (0, 1, 3, 4, 5, 2)).reshape(s.shape[0], db, 2, n_kv * HEAD_DIM, s.shape[2])
            for s in states]
    in_specs = [pl.BlockSpec((None, tq, q.shape[2]), lambda bi: (bi, 0, 0)),
                pl.BlockSpec((None, tq, n_grp * width), lambda bi: (bi, 0, 0))]
    in_specs += [pl.BlockSpec((None, None, 2, n_kv * HEAD_DIM, lp), lambda bi: (layer, bi, 0, 0, 0))
                 for lp in past_lens]
    in_specs += [pl.BlockSpec((n_chunks, rows, lp), lambda bi: (0, 0, 0)) for lp in past_lens]
    in_specs += [pl.BlockSpec((n_chunks, rows, LANES), lambda bi: (0, 0, 0)) for _ in past_lens]
    args = [q, new_kv] + st_t + list(bias_past) + list(bias_new)
    if sinks_rows is not None:
        in_specs.append(pl.BlockSpec((n_chunks, rows, 1), lambda bi: (0, 0, 0)))
        args.append(sinks_rows)
    res = pl.pallas_call(
        functools.partial(_decode_kernel, n_grp=n_grp, n_kv=n_kv, past_lens=past_lens,
                          use_sink=sinks_rows is not None, tq=tq),
        grid=(db,),
        in_specs=in_specs,
        out_specs=[pl.BlockSpec((None, tq, N_HEADS * LANES), lambda bi: (bi, 0, 0))]
        + [pl.BlockSpec((None, 2, n_kv * HEAD_DIM, lp), lambda bi: (bi, 0, 0, 0)) for lp in past_lens],
        out_shape=[jax.ShapeDtypeStruct((db, tq, N_HEADS * LANES), F32)]
        + [jax.ShapeDtypeStruct((db, 2, n_kv * HEAD_DIM, lp), F32) for lp in past_lens],
        compiler_params=_cparams(("parallel",)),
        name=f"decode_kv{n_kv}",
    )(*args)
    new_states = [jnp.transpose(r.reshape(db, 2, n_kv, HEAD_DIM, lp), (0, 4, 1, 2, 3))
                  for r, lp in zip(res[1:], past_lens)]
    return res[0], new_states


def _ffn_kernel(*refs, n_attn, sample, tiles_per_seq, row_tile, ff_chunk, tq, dils):
    x_ref = refs[0]
    pos = 1
    if n_attn == 1:
        o_in = refs[pos]; pos += 1
    else:
        o_refs = refs[pos:pos + n_attn]; pos += n_attn
        lse_refs = refs[pos:pos + n_attn]; pos += n_attn
    wo_ref, g_ref, win_ref, cw_ref, cb_ref, wout_ref = refs[pos:pos + 6]; pos += 6
    if sample:
        fix1_ref, fix2_ref = refs[pos:pos + 2]; pos += 2
    xo_ref, gate_ref = refs[pos:pos + 2]; pos += 2
    gp_scr, halo_scr = refs[pos:pos + 2]; pos += 2
    il_scr = refs[pos:]
    i = pl.program_id(0)
    low = _low_half()

    @pl.when(i == 0)
    def _():
        halo_scr[...] = jnp.zeros(halo_scr.shape, F32)

    if n_attn == 1:
        ob = o_in[...]
    else:
        def natural(ref4, g, col0):
            d = dils[g]
            if d == 1:
                return ref4[0, :, col0:col0 + LANES]
            scr = il_scr[sum(1 for x in dils[:g] if x > 1)]
            for r in range(d):
                scr[pl.ds(r, row_tile // d, stride=d), :] = ref4[r, :, col0:col0 + LANES]
            return scr[...]

        lses = [natural(lse_refs[g], g, 0) for g in range(n_attn)]
        m_all = lses[0]
        for z in lses[1:]:
            m_all = jnp.maximum(m_all, z)
        es = [jnp.exp(z - m_all) for z in lses]
        den = es[0]
        for e in es[1:]:
            den = den + e
        ws = [e / den for e in es]
        chunks = []
        for c in range(1024 // LANES):
            acc = None
            for g in range(n_attn):
                wsel = jnp.where(low, ws[g][:, 2 * c:2 * c + 1], ws[g][:, 2 * c + 1:2 * c + 2])
                term = wsel * natural(o_refs[g], g, c * LANES)
                acc = term if acc is None else acc + term
            chunks.append(acc.astype(BF16))
        ob = jnp.concatenate(chunks, axis=1)
    x1 = x_ref[...] + _dot(ob, wo_ref[...])
    h = x1 * lax.rsqrt(jnp.mean(x1 * x1, axis=-1, keepdims=True) + RMS_EPS) * g_ref[...]
    hb = h.astype(BF16)

    if sample:
        t_in_seq = lax.broadcasted_iota(jnp.int32, (row_tile, 1), 0) & (tq - 1)
    y = jnp.zeros((row_tile, D_MODEL), F32)
    for f in range(D_FF // ff_chunk):
        f0 = f * ff_chunk
        gate = _dot(hb, win_ref[:, f0:f0 + ff_chunk])
        up = _dot(hb, win_ref[:, D_FF + f0:D_FF + f0 + ff_chunk])
        if sample:
            gp_scr[0:SUBLANES, :] = jnp.zeros((SUBLANES, ff_chunk), F32)
        else:
            first = (i % tiles_per_seq) == 0
            gp_scr[0:SUBLANES, :] = jnp.where(first, 0.0, halo_scr[f])
        gp_scr[SUBLANES:SUBLANES + row_tile, :] = gate
        g2 = gp_scr[SUBLANES - 2:SUBLANES - 2 + row_tile, :]
        g1 = gp_scr[SUBLANES - 1:SUBLANES - 1 + row_tile, :]
        if sample:
            g2 = jnp.where(t_in_seq >= 2, g2, fix2_ref[:, f0:f0 + ff_chunk])
            g1 = jnp.where(t_in_seq >= 1, g1, fix1_ref[:, f0:f0 + ff_chunk])
        gc = (cb_ref[:, f0:f0 + ff_chunk] + cw_ref[0:1, f0:f0 + ff_chunk] * g2
              + cw_ref[1:2, f0:f0 + ff_chunk] * g1 + cw_ref[2:3, f0:f0 + ff_chunk] * gate)
        act = gc * (1.0 / (1.0 + jnp.exp(-gc))) * up
        y = y + _dot(act.astype(BF16), wout_ref[f0:f0 + ff_chunk, :])
        if sample:
            gate_ref[:, f0:f0 + ff_chunk] = gate
        else:
            tail = gate[row_tile - SUBLANES:row_tile, :]
            halo_scr[f] = tail
            gate_ref[:, f0:f0 + ff_chunk] = tail
    xo_ref[...] = x1 + y


def _ffn(x, attn, lses, w_o, gain, w_in, conv_w, conv_b, w_out, fix, seq_len, row_tile=512,
         ff_chunk=1408):
    rows = x.shape[0]
    sample = fix is not None
    n_attn = len(attn)
    row_spec = lambda w: pl.BlockSpec((row_tile, w), lambda i: (i, 0))
    full = lambda a: pl.BlockSpec(a.shape, lambda i: (0,) * a.ndim, pipeline_mode=pl.Buffered(1))
    tiles = seq_len // row_tile
    res_spec = lambda a: pl.BlockSpec((None, a.shape[1], row_tile // a.shape[1], a.shape[3]),
                                      lambda i: (i // tiles, 0, i % tiles, 0))
    dils = (1,)
    if n_attn == 1:
        in_specs = [row_spec(D_MODEL), row_spec(attn[0].shape[1])]
        args = [x] + list(attn)
    else:
        dils = tuple(a.shape[1] for a in attn)
        in_specs = [row_spec(D_MODEL)] + [res_spec(a) for a in attn] + [res_spec(a) for a in lses]
        args = [x] + list(attn) + list(lses)
    consts = [w_o.astype(BF16), gain.reshape(1, D_MODEL), w_in.astype(BF16), conv_w,
              conv_b.reshape(1, D_FF), w_out.astype(BF16)]
    in_specs += [full(a) for a in consts]
    args += consts
    if sample:
        in_specs += [row_spec(D_FF), row_spec(D_FF)]
        args += list(fix)
        gate_spec = row_spec(D_FF)
        gate_shape = jax.ShapeDtypeStruct((rows, D_FF), F32)
        tiles_per_seq = 1
        tq = seq_len
    else:
        gate_spec = pl.BlockSpec((None, SUBLANES, D_FF), lambda i: (i, 0, 0))
        gate_shape = jax.ShapeDtypeStruct((rows // row_tile, SUBLANES, D_FF), F32)
        tiles_per_seq = seq_len // row_tile
        tq = 1
    return pl.pallas_call(
        functools.partial(_ffn_kernel, n_attn=n_attn, sample=sample, tiles_per_seq=tiles_per_seq,
                          row_tile=row_tile, ff_chunk=ff_chunk, tq=tq, dils=dils),
        grid=(rows // row_tile,),
        in_specs=in_specs,
        out_specs=[row_spec(D_MODEL), gate_spec],
        out_shape=[jax.ShapeDtypeStruct((rows, D_MODEL), F32), gate_shape],
        scratch_shapes=[pltpu.VMEM((SUBLANES + row_tile, ff_chunk), F32),
                        pltpu.VMEM((D_FF // ff_chunk, SUBLANES, ff_chunk), F32)]
        + [pltpu.VMEM((row_tile, LANES), F32) for d in dils if d > 1],
        compiler_params=_cparams(("arbitrary",)),
        name="ffn_sample" if sample else "ffn_prompt",
    )(*args)


def _expand_wo_sample(w_o, n_kv):
    grp_heads = N_HEADS // n_kv
    src = []
    for h in range(N_HEADS):
        rows = _cols(h * HEAD_DIM, HEAD_DIM)
        src += (_zeros(HEAD_DIM) + rows) if (h // grp_heads) % 2 else (rows + _zeros(HEAD_DIM))
    return _take_cols(w_o.T, src).T


def kernel(x_prompt, x_sample, cache_kv_a, state_kv_b, state_kv_c1, state_kv_c2, state_kv_c3, state_conv_ffn, page_table, rel_bias_table, norm_mix, norm_ffn, w_in_a, q_norm_a, k_norm_a, lambda_a, subln_a, w_o_a, w_in_b, q_norm_b, k_norm_b, sinks_b, w_o_b, w_in_c, q_norm_c, k_norm_c, w_o_c, w_ffn_in, conv_ffn_w, conv_ffn_b, w_ffn_out):
    bp, tp, _ = x_prompt.shape
    bs, ts, _ = x_sample.shape
    depth = norm_mix.shape[0]
    past_len = page_table.shape[1] * PAGE_SIZE
    xp = x_prompt.reshape(bp * tp, D_MODEL)
    xs = x_sample.reshape(bs * ts, D_MODEL)
    state_c = (state_kv_c1, state_kv_c2, state_kv_c3)

    toep = _toep_table(rel_bias_table)
    band_bias = {d: _band_bias_table(rel_bias_table, d) for d in (1, 4, 16)}
    a_cols = [[kv * 8 + j for j in range(8)] for kv in range(A_KV)]
    a_bias_past, a_bias_new = _decode_bias_table(rel_bias_table, a_cols, past_len, past_len + ts, 1, ts)
    a_bias_far = jnp.repeat(rel_bias_table[N_BUCKETS - 1][jnp.asarray(a_cols)], ts, axis=1)[..., None]
    b_cols = [list(range(N_HEADS))]
    b_len = state_kv_b.shape[2]
    b_bias_past, b_bias_new = _decode_bias_table(rel_bias_table, b_cols, b_len, B_WINDOW, 1, ts)
    c_cols = [list(range(8)), list(range(8, 16))]
    c_bias = [_decode_bias_table(rel_bias_table, c_cols, state_c[g].shape[2], w, d, ts)
              for g, (w, d) in enumerate(C_PAIRS)]

    kv_a_p, kv_a_s, kv_b_p, kv_b_s = [], [], [], []
    kv_c_p = [[] for _ in C_PAIRS]
    kv_c_s = [[] for _ in C_PAIRS]
    conv_p, conv_s = [], []
    for i in range(depth):
        n = i // N_MIXERS
        if i % N_MIXERS == 0:
            lam_init = 0.8 - 0.6 * math.exp(-0.3 * i)
            q, kvn, k, vt = _run_proj(xp, norm_mix[i], w_in_a[n], q_norm_a[n], k_norm_a[n], _layout_a(True), 512,
                                      q_scale=LOG2E, t_cols=_cols(1280, 256), seq_len=tp)
            op = _a_prompt(q.reshape(bp, tp, 1024), k.reshape(bp, tp, 256), vt, toep, lambda_a[n], subln_a[n],
                           lam_init)
            attn_p, lse_p = [op.reshape(bp * tp, 1024)], None
            kv_a_p.append(kvn.reshape(bp, tp, 2, A_KV, 2 * HEAD_DIM))
            qs, kvs = _run_proj(xs, norm_mix[i], w_in_a[n], q_norm_a[n], k_norm_a[n], _layout_a(False), 512)
            os_ = _a_sample(qs.reshape(bs, ts, 1024), kvs.reshape(bs, ts, 512), cache_kv_a, n, page_table,
                            a_bias_past, a_bias_far, a_bias_new, lambda_a[n], subln_a[n], lam_init)
            attn_s = [os_.reshape(bs * ts, 1024).astype(BF16)]
            kv_a_s.append(kvs.reshape(bs, ts, 2, A_KV, 2 * HEAD_DIM))
            w_o_p = w_o_a[n]
            w_o_s = w_o_a[n]
        elif i % N_MIXERS == 1:
            kvn, q, kd, vd = _run_proj(xp, norm_mix[i], w_in_b[n], q_norm_b[n], k_norm_b[n],
                                       _layout_band_prompt((1,), B_KV), 512)
            op = _band_prompt(q.reshape(bp, 1, tp, 1024), kd.reshape(bp, 1, tp, B_KV * LANES),
                              vd.reshape(bp, 1, tp, B_KV * LANES), band_bias[1], sinks_b[n], B_KV, False, BF16)[0]
            attn_p, lse_p = [op.reshape(bp * tp, 1024)], None
            wb = min(B_WINDOW, tp)
            kv_b_p.append(kvn.reshape(bp, tp, -1)[:, tp - wb:].reshape(bp, wb, 2, B_KV, HEAD_DIM))
            qs, kvs = _run_proj(xs, norm_mix[i], w_in_b[n], q_norm_b[n], k_norm_b[n],
                                _layout_band_sample(1, B_KV), 512)
            sink_rows = jnp.repeat(sinks_b[n], ts).reshape(1, N_HEADS * ts, 1)
            os_, nst = _decode(qs.reshape(bs, ts, -1), kvs.reshape(bs, ts, -1), [state_kv_b], n,
                               [b_bias_past], [b_bias_new], sink_rows, B_KV)
            attn_s = [os_.reshape(bs * ts, N_HEADS * LANES).astype(BF16)]
            kv_b_s.append(nst[0])
            w_o_p = w_o_b[n]
            w_o_s = _expand_wo_sample(w_o_b[n], B_KV)
        else:
            c_dils = tuple(d for _, d in C_PAIRS)
            outs = _run_proj(xp, norm_mix[i], w_in_c[n], q_norm_c[n], k_norm_c[n],
                             _layout_band_prompt(c_dils, C_KV), 256, seq_len=tp)
            kvn = outs[0]
            attn_p, lse_p = [], []
            for g, (w, d) in enumerate(C_PAIRS):
                q, kd, vd = (a.reshape(bp, d, tp // d, a.shape[-1]) for a in outs[1 + 3 * g:4 + 3 * g])
                o_g, lse_g = _band_prompt(q, kd, vd, band_bias[d], None, C_KV, True, F32)
                attn_p.append(o_g)
                lse_p.append(lse_g)
            kvn = kvn.reshape(bp, tp, -1)
            gw = 2 * C_KV * HEAD_DIM
            for g, (w, d) in enumerate(C_PAIRS):
                wc = min(w, tp)
                kv_c_p[g].append(kvn[:, tp - wc:, g * gw:(g + 1) * gw].reshape(bp, wc, 2, C_KV, HEAD_DIM))
            qs, kvs = _run_proj(xs, norm_mix[i], w_in_c[n], q_norm_c[n], k_norm_c[n],
                                _layout_band_sample(3, C_KV), 256)
            os_, nst = _decode(qs.reshape(bs, ts, -1), kvs.reshape(bs, ts, -1), list(state_c), n,
                               [cb[0] for cb in c_bias], [cb[1] for cb in c_bias], None, C_KV)
            attn_s = [os_.reshape(bs * ts, N_HEADS * LANES).astype(BF16)]
            for g in range(3):
                kv_c_s[g].append(nst[g])
            w_o_p = w_o_c[n]
            w_o_s = _expand_wo_sample(w_o_c[n], C_KV)

        xp, gate_tail = _ffn(xp, attn_p, lse_p, w_o_p, norm_ffn[i], w_ffn_in[i], conv_ffn_w[i],
                             conv_ffn_b[i], w_ffn_out[i], None, tp)
        tiles = tp // 512
        conv_p.append(gate_tail.reshape(bp, tiles, SUBLANES, D_FF)[:, -1, SUBLANES - (CONV_W - 1):])
        hist = state_conv_ffn[i]
        zero = jnp.zeros((bs, ts - 1, D_FF), F32)
        fix1 = jnp.concatenate([hist[:, 1:2], zero], axis=1).reshape(bs * ts, D_FF)
        fix2 = jnp.concatenate([hist[:, 0:1], hist[:, 1:2], zero[:, 1:]], axis=1).reshape(bs * ts, D_FF)
        xs, gate_s = _ffn(xs, attn_s, None, w_o_s, norm_ffn[i], w_ffn_in[i], conv_ffn_w[i],
                          conv_ffn_b[i], w_ffn_out[i], (fix1, fix2), ts, row_tile=256)
        conv_s.append(gate_s.reshape(bs, ts, D_FF)[:, ts - (CONV_W - 1):])

    return (xp.reshape(bp, tp, D_MODEL), xs.reshape(bs, ts, D_MODEL),
            jnp.stack(kv_a_p), jnp.stack(kv_a_s),
            jnp.stack(kv_b_p), jnp.stack(kv_b_s),
            jnp.stack(kv_c_p[0]), jnp.stack(kv_c_s[0]),
            jnp.stack(kv_c_p[1]), jnp.stack(kv_c_s[1]),
            jnp.stack(kv_c_p[2]), jnp.stack(kv_c_s[2]),
            jnp.stack(conv_p), jnp.stack(conv_s))
```

```python
import functools
import math

import numpy as np
import jax
import jax.numpy as jnp
from jax import lax
from jax.experimental import pallas as pl
from jax.experimental.pallas import tpu as pltpu

F32 = jnp.float32
BF16 = jnp.bfloat16

D_MODEL = 1024
HEAD_DIM = 64
LANES = 128
SUBLANES = 8
RMS_EPS = 1e-6
NEG = -1e30
LOG2E = math.log2(math.e)
N_HEADS = 16
N_BUCKETS = 32
MAX_DISTANCE = 2048
PAGE_SIZE = 128
D_FF = 2816
CONV_W = 3
N_MIXERS = 3
A_KV = 2
B_KV = 2
C_KV = 4
B_WINDOW = 128
C_PAIRS = ((128, 1), (512, 4), (2048, 16))
BAND = 128
ONES_ROWS = 16
VMEM_LIMIT = 56 * 1024 * 1024


def _bucket_thresholds():
    n = np.arange(0, 1 << 15)
    x = np.log(np.maximum(n, 1) / (N_BUCKETS // 2)) / math.log(MAX_DISTANCE / (N_BUCKETS // 2))
    large = N_BUCKETS // 2 + (x * (N_BUCKETS - N_BUCKETS // 2)).astype(np.int64)
    bucket = np.where(n < N_BUCKETS // 2, n, np.minimum(large, N_BUCKETS - 1))
    return [int(np.argmax(bucket >= b)) for b in range(N_BUCKETS)]


THR = _bucket_thresholds()
N_TOEP = -(-(THR[-1] + BAND - 1) // BAND)
TOEP_TILES = N_TOEP + 2


def _cparams(sem, vmem=VMEM_LIMIT):
    return pltpu.CompilerParams(dimension_semantics=sem, vmem_limit_bytes=vmem)


def _dot(a, b):
    return jnp.dot(a, b, preferred_element_type=F32)


def _dot_nt(a, b):
    return lax.dot_general(a, b, (((1,), (1,)), ((), ())), preferred_element_type=F32)


def _low_half():
    return lax.broadcasted_iota(jnp.int32, (1, LANES), 1) < HEAD_DIM


def _bias_of(dist, col, tab_ref, lo, hi):
    b_lo = max(b for b in range(N_BUCKETS) if THR[b] <= max(lo, 0))
    b_hi = max(b for b in range(N_BUCKETS) if THR[b] <= max(hi, 0))
    val = jnp.full(dist.shape, tab_ref[b_hi, col], F32)
    for b in range(b_hi - 1, b_lo - 1, -1):
        val = jnp.where(dist < THR[b + 1], tab_ref[b, col], val)
    return val


def _toep_kernel(tab_ref, o_ref):
    t = pl.program_id(0)
    key = lax.broadcasted_iota(jnp.int32, (BAND, BAND), 0)
    qry = lax.broadcasted_iota(jnp.int32, (BAND, BAND), 1)
    for delta in range(-1, N_TOEP + 1):
        @pl.when(t == delta + 1)
        def _(delta=delta):
            for c in range(N_HEADS):
                if delta < 0:
                    o_ref[c] = jnp.full((BAND, BAND), NEG, F32)
                elif delta == N_TOEP:
                    o_ref[c] = jnp.full((BAND, BAND), tab_ref[N_BUCKETS - 1, c] * LOG2E, F32)
                else:
                    dist = delta * BAND + qry - key
                    val = _bias_of(dist, c, tab_ref, delta * BAND - BAND + 1, delta * BAND + BAND - 1)
                    o_ref[c] = jnp.where(dist >= 0, val * LOG2E, NEG)


def _toep_table(table):
    return pl.pallas_call(
        _toep_kernel,
        grid=(TOEP_TILES,),
        in_specs=[pl.BlockSpec(memory_space=pltpu.SMEM)],
        out_specs=pl.BlockSpec((None, N_HEADS, BAND, BAND), lambda t: (t, 0, 0, 0)),
        out_shape=jax.ShapeDtypeStruct((TOEP_TILES, N_HEADS, BAND, BAND), F32),
        compiler_params=_cparams(("arbitrary",)),
        name="bias_toeplitz",
    )(table)


def _band_bias_kernel(tab_ref, o_ref, *, dil):
    key = lax.broadcasted_iota(jnp.int32, (2 * BAND, BAND), 0)
    qry = lax.broadcasted_iota(jnp.int32, (2 * BAND, BAND), 1)
    steps = BAND + qry - key
    valid = (steps >= 0) & (steps <= BAND)
    for c in range(N_HEADS):
        val = _bias_of(steps * dil, c, tab_ref, 0, BAND * dil)
        o_ref[:, c * BAND:(c + 1) * BAND] = jnp.where(valid, val, NEG)


def _band_bias_table(table, dil):
    return pl.pallas_call(
        functools.partial(_band_bias_kernel, dil=dil),
        in_specs=[pl.BlockSpec(memory_space=pltpu.SMEM)],
        out_shape=jax.ShapeDtypeStruct((2 * BAND, N_HEADS * BAND), F32),
        compiler_params=_cparams(None),
        name=f"bias_band_d{dil}",
    )(table)


def _decode_bias_kernel(tab_ref, past_ref, new_ref, *, cols, past_len, window, dil, tq):
    n_grp = len(cols)
    t_p = lax.broadcasted_iota(jnp.int32, (tq, past_len), 0)
    i_p = lax.broadcasted_iota(jnp.int32, (tq, past_len), 1)
    d_p = past_len + t_p - i_p
    ok_p = d_p <= window
    if dil > 1:
        ok_p = ok_p & ((d_p & (dil - 1)) == 0)
    t_n = lax.broadcasted_iota(jnp.int32, (tq, LANES), 0)
    i_n = lax.broadcasted_iota(jnp.int32, (tq, LANES), 1)
    d_n = t_n - i_n
    ok_n = (d_n >= 0) & (i_n < tq)
    if dil > 1:
        ok_n = ok_n & ((d_n & (dil - 1)) == 0)
    for r in range(n_grp):
        c = cols[r]
        vp = _bias_of(d_p, c, tab_ref, 1, min(window, past_len + tq - 1))
        past_ref[r * tq:(r + 1) * tq, :] = jnp.where(ok_p, vp, NEG)
        vn = _bias_of(d_n, c, tab_ref, 0, tq - 1)
        new_ref[r * tq:(r + 1) * tq, :] = jnp.where(ok_n, vn, NEG)


def _decode_bias_table(table, cols, past_len, window, dil, tq):
    outs = [pl.pallas_call(
        functools.partial(_decode_bias_kernel, cols=tuple(cc), past_len=past_len, window=window,
                          dil=dil, tq=tq),
        in_specs=[pl.BlockSpec(memory_space=pltpu.SMEM)],
        out_shape=(jax.ShapeDtypeStruct((len(cc) * tq, past_len), F32),
                   jax.ShapeDtypeStruct((len(cc) * tq, LANES), F32)),
        compiler_params=_cparams(None),
        name=f"bias_decode_L{past_len}_d{dil}",
    )(table) for cc in cols]
    return jnp.stack([o[0] for o in outs]), jnp.stack([o[1] for o in outs])


def _proj_kernel(x_ref, g_ref, w_ref, cg_ref, *rest, plan, col_chunk, transposed, dils, slots):
    x = x_ref[...]
    h = x * lax.rsqrt(jnp.mean(x * x, axis=-1, keepdims=True) + RMS_EPS) * g_ref[...]
    hb = h.astype(BF16)
    low = _low_half()
    row_tile = x_ref.shape[0]
    dl_scr = None
    if max(dils) > 1:
        rest, dl_scr = rest[:-1], rest[-1]
    out_refs = rest
    if transposed:
        wt_ref, out_refs, ot_ref = rest[0], rest[1:-1], rest[-1]
        ot_ref[...] = _dot_nt(wt_ref[...], hb).astype(ot_ref.dtype)
    n_total = w_ref.shape[1]
    for c0 in range(0, n_total, col_chunk):
        cw = min(col_chunk, n_total - c0)
        y = _dot(hb, w_ref[:, c0:c0 + cw])
        for j in range(cw // LANES):
            src = c0 + j * LANES
            normed, dsts = plan[src // LANES]
            blk = y[:, j * LANES:(j + 1) * LANES]
            if normed:
                sq = blk * blk
                ms_lo = jnp.sum(jnp.where(low, sq, 0.0), axis=-1, keepdims=True) * (1.0 / HEAD_DIM)
                ms_hi = jnp.sum(jnp.where(low, 0.0, sq), axis=-1, keepdims=True) * (1.0 / HEAD_DIM)
                inv = jnp.where(low, lax.rsqrt(ms_lo + RMS_EPS), lax.rsqrt(ms_hi + RMS_EPS))
                blk = blk * inv * cg_ref[:, src:src + LANES]
            if any(dils[oi] > 1 for oi, _ in dsts):
                dl_scr[...] = blk
            for oi, dst in dsts:
                if slots[oi] > 1:
                    out_refs[oi][pl.ds(dst // LANES, row_tile, stride=slots[oi]), :] = blk.astype(out_refs[oi].dtype)
                elif dils[oi] == 1:
                    out_refs[oi][:, dst:dst + LANES] = blk.astype(out_refs[oi].dtype)
                else:
                    for r in range(dils[oi]):
                        part = dl_scr[pl.ds(r, row_tile // dils[oi], stride=dils[oi]), :]
                        out_refs[oi][r, :, dst:dst + LANES] = part.astype(out_refs[oi].dtype)


class _ProjLayout:
    def __init__(self):
        self.src = []
        self.gain = []
        self.plan = []
        self.outs = []

    def add_output(self, width, dtype, dil=1, slots=1):
        self.outs.append((width, dtype, dil, slots))
        return len(self.outs) - 1

    def add_chunk(self, src_cols, kind, dsts):
        assert len(src_cols) == LANES
        self.src.extend(src_cols)
        self.gain.extend([kind] * LANES)
        self.plan.append((kind != 0, list(dsts)))


def _cols(start, n):
    return list(range(start, start + n))


def _zeros(n):
    return [-1] * n


def _take_cols(w, src):
    parts, i = [], 0
    while i < len(src):
        j = i
        if src[i] < 0:
            while j < len(src) and src[j] < 0:
                j += 1
            parts.append(jnp.zeros((w.shape[0], j - i), w.dtype))
        else:
            while j + 1 < len(src) and src[j + 1] == src[j] + 1:
                j += 1
            j += 1
            parts.append(w[:, src[i]:src[i] + j - i])
        i = j
    return jnp.concatenate(parts, axis=1) if len(parts) > 1 else parts[0]


def _run_proj(x, norm_gain, w_in, q_gain, k_gain, layout, row_tile, col_chunk=1024, q_scale=1.0,
              t_cols=None, seq_len=None):
    rows = x.shape[0]
    src = np.asarray(layout.src)
    w_ext = _take_cols(w_in, list(layout.src)).astype(BF16)
    kind = np.asarray(layout.gain)
    n_ext = len(src)
    reps = n_ext // HEAD_DIM
    cg = jnp.where(jnp.asarray(kind == 1), jnp.tile(q_gain, reps) * (HEAD_DIM ** -0.5 * q_scale),
                   jnp.where(jnp.asarray(kind == 2), jnp.tile(k_gain, reps), 1.0)).reshape(1, n_ext)
    dils = tuple(d for _, _, d, _ in layout.outs)
    slots = tuple(sl for _, _, _, sl in layout.outs)
    tiles = None if seq_len is None else seq_len // row_tile
    out_shape, out_specs = [], []
    for w, dt, d, sl in layout.outs:
        if sl > 1:
            out_shape.append(jax.ShapeDtypeStruct((rows * sl, LANES), dt))
            out_specs.append(pl.BlockSpec((row_tile * sl, LANES), lambda i: (i, 0)))
        elif d == 1:
            out_shape.append(jax.ShapeDtypeStruct((rows, w), dt))
            out_specs.append(pl.BlockSpec((row_tile, w), lambda i: (i, 0)))
        else:
            out_shape.append(jax.ShapeDtypeStruct((rows // seq_len, d, seq_len // d, w), dt))
            out_specs.append(pl.BlockSpec((None, d, row_tile // d, w), lambda i: (i // tiles, 0, i % tiles, 0)))
    in_specs = [pl.BlockSpec((row_tile, D_MODEL), lambda i: (i, 0)),
                pl.BlockSpec((1, D_MODEL), lambda i: (0, 0)),
                pl.BlockSpec((D_MODEL, n_ext), lambda i: (0, 0), pipeline_mode=pl.Buffered(1)),
                pl.BlockSpec((1, n_ext), lambda i: (0, 0))]
    args = [x, norm_gain.reshape(1, D_MODEL), w_ext, cg]
    if t_cols is not None:
        n_t = len(t_cols)
        in_specs.append(pl.BlockSpec((n_t, D_MODEL), lambda i: (0, 0), pipeline_mode=pl.Buffered(1)))
        args.append(_take_cols(w_in, list(t_cols)).T.astype(BF16))
        out_shape.append(jax.ShapeDtypeStruct((rows // seq_len, n_t, seq_len), BF16))
        out_specs.append(pl.BlockSpec((None, n_t, row_tile), lambda i: (i // tiles, 0, i % tiles)))
    return pl.pallas_call(
        functools.partial(_proj_kernel, plan=tuple(layout.plan), col_chunk=col_chunk,
                          transposed=t_cols is not None, dils=dils, slots=slots),
        grid=(rows // row_tile,),
        in_specs=in_specs,
        out_specs=out_specs,
        out_shape=out_shape,
        scratch_shapes=[pltpu.VMEM((row_tile, LANES), F32)] if max(dils) > 1 else [],
        compiler_params=_cparams(("parallel",)),
        name="proj",
    )(*args)


def _layout_a(prompt):
    lay = _ProjLayout()
    oq = lay.add_output(1024, BF16 if prompt else F32)
    okv = lay.add_output(512, F32, slots=4 if prompt else 1)
    ok = lay.add_output(256, BF16) if prompt else None
    for c in range(8):
        lay.add_chunk(_cols(c * LANES, LANES), 1, [(oq, c * LANES)])
    for c in range(2):
        lay.add_chunk(_cols(1024 + c * LANES, LANES), 2,
                      [(okv, c * LANES)] + ([(ok, c * LANES)] if prompt else []))
    for c in range(2):
        lay.add_chunk(_cols(1280 + c * LANES, LANES), 0, [(okv, 256 + c * LANES)])
    return lay


def _layout_band_prompt(dils, n_kv):
    n_grp = len(dils)
    nq = n_grp * N_HEADS * HEAD_DIM
    nk = n_grp * n_kv * HEAD_DIM
    gk = n_kv * HEAD_DIM
    lay = _ProjLayout()
    okv = lay.add_output(2 * nk, F32)
    for g, d in enumerate(dils):
        oq = lay.add_output(N_HEADS * HEAD_DIM, BF16, d)
        okd = lay.add_output(n_kv * LANES, BF16, d)
        ovd = lay.add_output(n_kv * LANES, BF16, d)
        for c in range(N_HEADS * HEAD_DIM // LANES):
            lay.add_chunk(_cols(g * N_HEADS * HEAD_DIM + c * LANES, LANES), 1, [(oq, c * LANES)])
        for c in range(gk // LANES):
            lay.add_chunk(_cols(nq + g * gk + c * LANES, LANES), 2, [(okv, g * 2 * gk + c * LANES)])
        for c in range(gk // LANES):
            lay.add_chunk(_cols(nq + nk + g * gk + c * LANES, LANES), 0, [(okv, g * 2 * gk + gk + c * LANES)])
        for kv in range(n_kv):
            lay.add_chunk(_cols(nq + g * gk + kv * HEAD_DIM, HEAD_DIM) * 2, 2, [(okd, kv * LANES)])
        for kv in range(n_kv):
            lay.add_chunk(_cols(nq + nk + g * gk + kv * HEAD_DIM, HEAD_DIM) * 2, 0, [(ovd, kv * LANES)])
    return lay


def _layout_band_sample(n_grp, n_kv):
    nq = n_grp * N_HEADS * HEAD_DIM
    nk = n_grp * n_kv * HEAD_DIM
    grp_heads = N_HEADS // n_kv
    lay = _ProjLayout()
    oq = lay.add_output(n_grp * N_HEADS * LANES, F32)
    okv = lay.add_output(2 * nk, F32)
    for g in range(n_grp):
        for h in range(N_HEADS):
            qc = _cols((g * N_HEADS + h) * HEAD_DIM, HEAD_DIM)
            odd = (h // grp_heads) % 2
            lay.add_chunk(_zeros(HEAD_DIM) + qc if odd else qc + _zeros(HEAD_DIM), 1,
                          [(oq, (g * N_HEADS + h) * LANES)])
    gk = n_kv * HEAD_DIM
    for g in range(n_grp):
        for c in range(gk // LANES):
            lay.add_chunk(_cols(nq + g * gk + c * LANES, LANES), 2, [(okv, g * 2 * gk + c * LANES)])
        for c in range(gk // LANES):
            lay.add_chunk(_cols(nq + nk + g * gk + c * LANES, LANES), 0, [(okv, g * 2 * gk + gk + c * LANES)])
    return lay


def _diff_finalize(acc0, l0, acc1, l1, lam_ref, subln_ref, lam_init):
    lp = lam_ref[...]
    lam = (jnp.exp(jnp.sum(lp[0:1] * lp[1:2], axis=-1, keepdims=True))
           - jnp.exp(jnp.sum(lp[2:3] * lp[3:4], axis=-1, keepdims=True)) + lam_init)
    o = acc0 / l0 - lam * (acc1 / l1)
    o = o * lax.rsqrt(jnp.mean(o * o, axis=-1, keepdims=True) + RMS_EPS) * subln_ref[...]
    return o * (1.0 - lam_init)


def _a_prompt_kernel(q_ref, k_ref, vt_ref, toep_ref, lam_ref, subln_ref, o_ref,
                     qm_scr, ta_scr, tb_scr, p_scr, cm_scr, m_scr, acc_scr, *, tq, tk, lam_init, rb):
    qi = pl.program_id(2)
    low = _low_half()
    nh = 8
    w = nh * tq
    for g in range(4):
        qg = q_ref[:, g * LANES:(g + 1) * LANES]
        qm_scr[(2 * g) * tq:(2 * g + 1) * tq, :] = jnp.where(low, qg, jnp.zeros_like(qg))
        qm_scr[(2 * g + 1) * tq:(2 * g + 2) * tq, :] = jnp.where(low, jnp.zeros_like(qg), qg)
    m_scr[...] = jnp.full(m_scr.shape, NEG, F32)
    acc_scr[...] = jnp.zeros(acc_scr.shape, F32)
    n_chunks = (qi * tq + tq - 1) // tk + 1
    last = k_ref.shape[0] // tk - 1

    def stage_a(kj, t_ref):
        k0 = pl.multiple_of(jnp.minimum(kj, last) * tk, tk)
        s = _dot_nt(k_ref[pl.ds(k0, tk), :], qm_scr[...])
        macc = jnp.full((SUBLANES, w), NEG, F32)
        for u in range(tk // BAND):
            idx = [jnp.clip((qi * (tq // BAND) + r) - (kj * (tk // BAND) + u) + 1, 0, N_TOEP + 1)
                   for r in range(tq // BAND)]
            for b0 in range(0, BAND, rb):
                bias = jnp.concatenate([toep_ref[idx[r], h, b0:b0 + rb, :] for h in range(nh)
                                        for r in range(tq // BAND)], axis=1)
                r0 = u * BAND + b0
                t = s[r0:r0 + rb, :] + bias
                t_ref[r0:r0 + rb, :] = t
                for g8 in range(rb // SUBLANES):
                    macc = jnp.maximum(macc, t[g8 * SUBLANES:(g8 + 1) * SUBLANES, :])
        return jnp.max(macc, axis=0, keepdims=True)

    def stage_b(kj, t_ref, m_new, alpha):
        k0 = pl.multiple_of(jnp.minimum(kj, last) * tk, tk)
        vtc = jnp.concatenate([vt_ref[:, pl.ds(k0, tk)], jnp.ones((ONES_ROWS, tk), BF16)], axis=0)
        mb = jnp.broadcast_to(m_new, (SUBLANES, w))
        for r0 in range(0, tk, rb):
            t = t_ref[r0:r0 + rb, :]
            ps = []
            for g8 in range(rb // SUBLANES):
                p8 = jnp.exp2(t[g8 * SUBLANES:(g8 + 1) * SUBLANES, :] - mb)
                ps.append(p8)
            p_scr[r0:r0 + rb, :] = jnp.concatenate(ps, axis=0).astype(BF16)
        acc_scr[...] = alpha * acc_scr[...] + _dot(vtc, p_scr[...])

    cm_scr[...] = stage_a(0, ta_scr)

    def body(j, carry):
        cmax1 = stage_a(2 * j + 1, tb_scr)
        m_prev = m_scr[...]
        m0 = jnp.maximum(m_prev, cm_scr[...])
        stage_b(2 * j, ta_scr, m0, jnp.exp2(m_prev - m0))
        cm_scr[...] = stage_a(2 * j + 2, ta_scr)
        m1 = jnp.maximum(m0, cmax1)
        stage_b(2 * j + 1, tb_scr, m1, jnp.exp2(m0 - m1))
        m_scr[...] = m1
        return carry

    lax.fori_loop(0, (n_chunks + 1) // 2, body, 0)
    lp = lam_ref[...]
    lam = (jnp.exp(jnp.sum(lp[0:1] * lp[1:2], axis=-1, keepdims=True))
           - jnp.exp(jnp.sum(lp[2:3] * lp[3:4], axis=-1, keepdims=True)) + lam_init)
    on = acc_scr[0:LANES, :] / acc_scr[LANES:LANES + 1, :]
    for g in range(4):
        o = on[:, (2 * g) * tq:(2 * g + 1) * tq] - lam * on[:, (2 * g + 1) * tq:(2 * g + 2) * tq]
        o = o * lax.rsqrt(jnp.mean(o * o, axis=0, keepdims=True) + RMS_EPS) * subln_ref[...]
        o = o * (1.0 - lam_init)
        o_ref[:, g * LANES:(g + 1) * LANES] = o.T.astype(o_ref.dtype)


def _a_prompt(q, k, vt, toep_t, lam_p, subln, lam_init, tq=256, tk=256, rb=32):
    b, t, _ = q.shape
    return pl.pallas_call(
        functools.partial(_a_prompt_kernel, tq=tq, tk=tk, lam_init=lam_init, rb=rb),
        grid=(b, A_KV, t // tq),
        in_specs=[pl.BlockSpec((None, tq, 512), lambda bi, kv, qi: (bi, qi, kv)),
                  pl.BlockSpec((None, t, LANES), lambda bi, kv, qi: (bi, 0, kv)),
                  pl.BlockSpec((None, LANES, t), lambda bi, kv, qi: (bi, kv, 0)),
                  pl.BlockSpec((TOEP_TILES, 8, BAND, BAND), lambda bi, kv, qi: (0, kv, 0, 0)),
                  pl.BlockSpec((4, HEAD_DIM), lambda bi, kv, qi: (0, 0)),
                  pl.BlockSpec((LANES, 1), lambda bi, kv, qi: (0, 0))],
        out_specs=pl.BlockSpec((None, tq, 512), lambda bi, kv, qi: (bi, qi, kv)),
        out_shape=jax.ShapeDtypeStruct((b, t, 1024), BF16),
        scratch_shapes=[pltpu.VMEM((8 * tq, LANES), BF16),
                        pltpu.VMEM((tk, 8 * tq), F32),
                        pltpu.VMEM((tk, 8 * tq), F32),
                        pltpu.VMEM((tk, 8 * tq), BF16),
                        pltpu.VMEM((1, 8 * tq), F32),
                        pltpu.VMEM((1, 8 * tq), F32),
                        pltpu.VMEM((LANES + ONES_ROWS, 8 * tq), F32)],
        compiler_params=_cparams(("parallel", "parallel", "arbitrary")),
        name="a_prompt",
    )(q, k, vt, toep_t, lam_p, subln.reshape(LANES, 1))


def _a_sample_kernel(pt_ref, q_ref, new_ref, bp_ref, bc_ref, bn_ref, lam_ref, subln_ref, *rest,
                     pages, tq, lam_init):
    page_refs = rest[:pages]
    o_ref = rest[pages]
    qm_scr, m_scr, l_scr, acc_scr = rest[pages + 1:]
    c = pl.program_id(1)
    last = pl.num_programs(1) - 1
    low = _low_half()
    rows = 8 * tq

    @pl.when(c == 0)
    def _():
        for kv in range(A_KV):
            for g in range(4):
                qg = q_ref[:, (kv * 4 + g) * LANES:(kv * 4 + g + 1) * LANES]
                qm_scr[kv, (2 * g) * tq:(2 * g + 1) * tq, :] = jnp.where(low, qg, 0.0)
                qm_scr[kv, (2 * g + 1) * tq:(2 * g + 2) * tq, :] = jnp.where(low, 0.0, qg)
        m_scr[...] = jnp.full(m_scr.shape, NEG, F32)
        l_scr[...] = jnp.zeros(l_scr.shape, F32)
        acc_scr[...] = jnp.zeros(acc_scr.shape, F32)

    n_slot = 2 * A_KV
    is_last = c == last

    def scores(kv):
        qm = qm_scr[kv].astype(BF16)
        keys = jnp.concatenate([pr[pl.ds(kv, PAGE_SIZE, stride=n_slot), :] for pr in page_refs], axis=0)
        return _dot_nt(qm, keys.astype(BF16)) + jnp.where(is_last, bp_ref[kv], bc_ref[kv])

    def values(kv):
        return jnp.concatenate([pr[pl.ds(A_KV + kv, PAGE_SIZE, stride=n_slot), :] for pr in page_refs], axis=0).astype(BF16)

    s = jnp.concatenate([scores(kv) for kv in range(A_KV)], axis=0)
    m_prev = m_scr[...]
    m_new = jnp.maximum(m_prev, jnp.max(s, axis=-1, keepdims=True))
    alpha = jnp.exp(m_prev - m_new)
    p = jnp.exp(s - m_new)
    l_scr[...] = alpha * l_scr[...] + jnp.sum(p, axis=-1, keepdims=True)
    m_scr[...] = m_new
    pb = p.astype(BF16)
    pv = jnp.concatenate([_dot(pb[kv * rows:(kv + 1) * rows], values(kv)) for kv in range(A_KV)], axis=0)
    acc_scr[...] = alpha * acc_scr[...] + pv

    @pl.when(is_last)
    def _():
        pad = jnp.zeros((LANES - tq, LANES), F32)
        for kv in range(A_KV):
            sl = slice(kv * rows, (kv + 1) * rows)
            qm = qm_scr[kv].astype(BF16)
            kn = jnp.concatenate([new_ref[:, kv * LANES:(kv + 1) * LANES], pad], axis=0)
            vn = jnp.concatenate([new_ref[:, (A_KV + kv) * LANES:(A_KV + kv + 1) * LANES], pad], axis=0)
            s = _dot_nt(qm, kn.astype(BF16)) + bn_ref[kv]
            m_prev = m_scr[sl]
            m_new = jnp.maximum(m_prev, jnp.max(s, axis=-1, keepdims=True))
            alpha = jnp.exp(m_prev - m_new)
            p = jnp.exp(s - m_new)
            l = alpha * l_scr[sl] + jnp.sum(p, axis=-1, keepdims=True)
            acc = alpha * acc_scr[sl] + _dot(p.astype(BF16), vn.astype(BF16))
            for g in range(4):
                r0, r1 = (2 * g) * tq, (2 * g + 1) * tq
                o = _diff_finalize(acc[r0:r0 + tq], l[r0:r0 + tq], acc[r1:r1 + tq], l[r1:r1 + tq],
                                   lam_ref, subln_ref, lam_init)
                o_ref[:, (kv * 4 + g) * LANES:(kv * 4 + g + 1) * LANES] = o.astype(o_ref.dtype)


def _a_sample(q, new_kv, cache, layer, page_table, bias_past, bias_const, bias_new, lam_p, subln, lam_init,
              pages=32):
    db, tq, _ = q.shape
    n_pages = page_table.shape[1]
    n_chunks = n_pages // pages
    assert n_chunks == 1 or pages * PAGE_SIZE + 1 >= THR[-1]
    cache2 = cache.reshape(cache.shape[0], cache.shape[1], PAGE_SIZE * 2 * A_KV, LANES)
    rows = 8 * tq

    def page_spec(j):
        return pl.BlockSpec((None, None, PAGE_SIZE * 2 * A_KV, LANES),
                            lambda bi, c, pt: (layer, pt[bi * n_pages + c * pages + j], 0, 0))

    grid_spec = pltpu.PrefetchScalarGridSpec(
        num_scalar_prefetch=1,
        grid=(db, n_chunks),
        in_specs=[pl.BlockSpec((None, tq, 1024), lambda bi, c, pt: (bi, 0, 0)),
                  pl.BlockSpec((None, tq, 512), lambda bi, c, pt: (bi, 0, 0)),
                  pl.BlockSpec((A_KV, rows, pages * PAGE_SIZE), lambda bi, c, pt: (0, 0, n_chunks - 1)),
                  pl.BlockSpec((A_KV, rows, 1), lambda bi, c, pt: (0, 0, 0)),
                  pl.BlockSpec((A_KV, rows, LANES), lambda bi, c, pt: (0, 0, 0)),
                  pl.BlockSpec((4, HEAD_DIM), lambda bi, c, pt: (0, 0)),
                  pl.BlockSpec((1, LANES), lambda bi, c, pt: (0, 0))]
                 + [page_spec(j) for j in range(pages)],
        out_specs=pl.BlockSpec((None, tq, 1024), lambda bi, c, pt: (bi, 0, 0)),
        scratch_shapes=[pltpu.VMEM((A_KV, rows, LANES), F32),
                        pltpu.VMEM((A_KV * rows, 1), F32),
                        pltpu.VMEM((A_KV * rows, 1), F32),
                        pltpu.VMEM((A_KV * rows, LANES), F32)])
    return pl.pallas_call(
        functools.partial(_a_sample_kernel, pages=pages, tq=tq, lam_init=lam_init),
        grid_spec=grid_spec,
        out_shape=jax.ShapeDtypeStruct((db, tq, 1024), F32),
        compiler_params=_cparams(("parallel", "arbitrary")),
        name="a_sample",
    )(page_table.reshape(-1), q, new_kv, bias_past, bias_const, bias_new, lam_p, subln.reshape(1, LANES),
      *([cache2] * pages))


def _band_kernel(*refs, n_kv, use_sink, want_lse):
    q_ref, kp_ref, kc_ref, vp_ref, vc_ref, bias_ref = refs[:6]
    pos = 6
    sink_ref = None
    if use_sink:
        sink_ref = refs[pos]
        pos += 1
    o_ref = refs[pos]
    lse_ref = refs[pos + 1] if want_lse else None
    qm_scr = refs[-1]
    i = pl.program_id(2)
    grp = N_HEADS // n_kv
    low = _low_half()
    for c in range(N_HEADS // 2):
        qp = q_ref[:, c * LANES:(c + 1) * LANES]
        zero = jnp.zeros_like(qp)
        qm_scr[(2 * c) * BAND:(2 * c + 1) * BAND, :] = jnp.where(low, qp, zero)
        qm_scr[(2 * c + 1) * BAND:(2 * c + 2) * BAND, :] = jnp.where(low, zero, qp)
    s_parts = []
    for kv in range(n_kv):
        k2 = jnp.concatenate([kp_ref[:, kv * LANES:(kv + 1) * LANES], kc_ref[:, kv * LANES:(kv + 1) * LANES]], axis=0)
        s_parts.append(_dot_nt(k2, qm_scr[kv * grp * BAND:(kv + 1) * grp * BAND, :]))
    key = lax.broadcasted_iota(jnp.int32, (2 * BAND, 1), 0)
    prev_ok = (key >= BAND) | (i > 0)
    s = jnp.where(prev_ok, jnp.concatenate(s_parts, axis=1) + bias_ref[...], NEG)
    m = jnp.max(s, axis=0, keepdims=True)
    if use_sink:
        m = jnp.maximum(m, sink_ref[...])
    p = jnp.exp(s - m).astype(BF16)
    ones = jnp.ones((ONES_ROWS, 2 * BAND), BF16)
    acc_parts = []
    for kv in range(n_kv):
        v2 = jnp.concatenate([vp_ref[:, kv * LANES:(kv + 1) * LANES], vc_ref[:, kv * LANES:(kv + 1) * LANES]], axis=0)
        vt = v2.astype(F32).T[0:HEAD_DIM, :].astype(BF16)
        acc_parts.append(_dot(jnp.concatenate([vt, ones], axis=0), p[:, kv * grp * BAND:(kv + 1) * grp * BAND]))
    acc = jnp.concatenate(acc_parts, axis=1)
    l = acc[HEAD_DIM:HEAD_DIM + 1, :]
    if use_sink:
        l = l + jnp.exp(sink_ref[...] - m)
    on = acc[0:HEAD_DIM, :] / l
    for c in range(N_HEADS // 2):
        pair = jnp.concatenate([on[:, (2 * c) * BAND:(2 * c + 1) * BAND], on[:, (2 * c + 1) * BAND:(2 * c + 2) * BAND]], axis=0)
        o_ref[:, c * LANES:(c + 1) * LANES] = pair.T.astype(o_ref.dtype)
    if want_lse:
        lse = m + jnp.log(l)
        rows = jnp.concatenate([lse[:, h * BAND:(h + 1) * BAND] for h in range(N_HEADS)]
                               + [jnp.zeros((LANES - N_HEADS, BAND), F32)], axis=0)
        lse_ref[...] = rows.T


def _band_prompt(q, kd, vd, bias, sinks, n_kv, want_lse, out_dtype):
    b, dil, ts, _ = q.shape
    nb = ts // BAND
    ck = n_kv * LANES
    cur = lambda bi, r, i: (bi, r, i, 0)
    prev = lambda bi, r, i: (bi, r, jnp.maximum(i - 1, 0), 0)
    in_specs = [pl.BlockSpec((None, None, BAND, 1024), cur),
                pl.BlockSpec((None, None, BAND, ck), prev),
                pl.BlockSpec((None, None, BAND, ck), cur),
                pl.BlockSpec((None, None, BAND, ck), prev),
                pl.BlockSpec((None, None, BAND, ck), cur),
                pl.BlockSpec((2 * BAND, N_HEADS * BAND), lambda bi, r, i: (0, 0))]
    args = [q, kd, kd, vd, vd, bias]
    if sinks is not None:
        in_specs.append(pl.BlockSpec((1, N_HEADS * BAND), lambda bi, r, i: (0, 0)))
        args.append(jnp.repeat(sinks, BAND).reshape(1, N_HEADS * BAND))
    out_specs = [pl.BlockSpec((None, None, BAND, 1024), cur)]
    out_shape = [jax.ShapeDtypeStruct((b, dil, ts, 1024), out_dtype)]
    if want_lse:
        out_specs.append(pl.BlockSpec((None, None, BAND, LANES), cur))
        out_shape.append(jax.ShapeDtypeStruct((b, dil, ts, LANES), F32))
    return pl.pallas_call(
        functools.partial(_band_kernel, n_kv=n_kv, use_sink=sinks is not None, want_lse=want_lse),
        grid=(b, dil, nb),
        in_specs=in_specs,
        out_specs=out_specs,
        out_shape=out_shape,
        scratch_shapes=[pltpu.VMEM((N_HEADS * BAND, LANES), BF16)],
        compiler_params=_cparams(("parallel", "parallel", "arbitrary")),
        name=f"band_d{dil}",
    )(*args)


def _decode_kernel(*refs, n_grp, n_kv, past_lens, use_sink, tq):
    q_ref, new_ref = refs[:2]
    pos = 2
    st_refs = refs[pos:pos + n_grp]; pos += n_grp
    bp_refs = refs[pos:pos + n_grp]; pos += n_grp
    bn_refs = refs[pos:pos + n_grp]; pos += n_grp
    sink_ref = None
    if use_sink:
        sink_ref = refs[pos]; pos += 1
    o_ref = refs[pos]; pos += 1
    nst_refs = refs[pos:pos + n_grp]
    width = 2 * n_kv * HEAD_DIM
    n_chunks = n_kv // 2
    heads_per_chunk = N_HEADS // n_chunks
    grp_heads = N_HEADS // n_kv
    low = _low_half()
    pad = jnp.zeros((LANES - tq, LANES), F32)
    for c in range(n_chunks):
        parts = []
        for g in range(n_grp):
            qs = jnp.concatenate(
                [q_ref[:, (g * N_HEADS + c * heads_per_chunk + hh) * LANES:(g * N_HEADS + c * heads_per_chunk + hh + 1) * LANES]
                 for hh in range(heads_per_chunk)], axis=0).astype(BF16)
            st = st_refs[g]
            kt = st[0, c * LANES:(c + 1) * LANES, :].astype(BF16)
            vt = st[1, c * LANES:(c + 1) * LANES, :].astype(BF16)
            kn = jnp.concatenate([new_ref[:, g * width + c * LANES:g * width + (c + 1) * LANES], pad], axis=0).astype(BF16)
            vn = jnp.concatenate([new_ref[:, g * width + n_kv * HEAD_DIM + c * LANES:g * width + n_kv * HEAD_DIM + (c + 1) * LANES], pad], axis=0).astype(BF16)
            sp = _dot(qs, kt) + bp_refs[g][c]
            sn = _dot_nt(qs, kn) + bn_refs[g][c]
            m = jnp.maximum(jnp.max(sp, axis=-1, keepdims=True), jnp.max(sn, axis=-1, keepdims=True))
            if use_sink:
                m = jnp.maximum(m, sink_ref[c])
            pp = jnp.exp(sp - m)
            pn = jnp.exp(sn - m)
            l = jnp.sum(pp, axis=-1, keepdims=True) + jnp.sum(pn, axis=-1, keepdims=True)
            if use_sink:
                l = l + jnp.exp(sink_ref[c] - m)
            acc = _dot_nt(pp.astype(BF16), vt) + _dot(pn.astype(BF16), vn)
            parts.append((m, l, acc))
        m_all = parts[0][0]
        for m, _, _ in parts[1:]:
            m_all = jnp.maximum(m_all, m)
        num = 0.0
        den = 0.0
        for m, l, acc in parts:
            w = jnp.exp(m - m_all)
            num = num + acc * w
            den = den + l * w
        o = num / den
        for hh in range(heads_per_chunk):
            h = c * heads_per_chunk + hh
            odd = (h // grp_heads) % 2
            oh = o[hh * tq:(hh + 1) * tq, :]
            o_ref[:, h * LANES:(h + 1) * LANES] = (jnp.where(low, 0.0, oh) if odd else jnp.where(low, oh, 0.0)).astype(o_ref.dtype)
    n_rows = n_kv * HEAD_DIM
    for g in range(n_grp):
        lp = past_lens[g]
        for kv_slot in range(2):
            c0 = g * width + kv_slot * n_rows
            new_t = jnp.concatenate([new_ref[:, c0:c0 + n_rows], jnp.zeros((LANES - tq, n_rows), F32)], axis=0).T
            for r0 in range(0, n_rows, HEAD_DIM):
                if lp > tq:
                    nst_refs[g][kv_slot, r0:r0 + HEAD_DIM, 0:lp - tq] = st_refs[g][kv_slot, r0:r0 + HEAD_DIM, tq:lp]
                nst_refs[g][kv_slot, r0:r0 + HEAD_DIM, lp - tq:lp] = new_t[r0:r0 + HEAD_DIM, 0:tq]


def _decode(q, new_kv, states, layer, bias_past, bias_new, sinks_rows, n_kv):
    db, tq, _ = q.shape
    n_grp = len(states)
    width = 2 * n_kv * HEAD_DIM
    n_chunks = n_kv // 2
    rows = tq * N_HEADS // n_chunks
    past_lens = tuple(s.shape[2] for s in states)
    st_t = [jnp.transpose(s, (0, 1, 3, 4, 5, 2)).reshape(s.shape[0], db, 2, n_kv * HEAD_DIM, s.shape[2])
            for s in states]
    in_specs = [pl.BlockSpec((None, tq, q.shape[2]), lambda bi: (bi, 0, 0)),
                pl.BlockSpec((None, tq, n_grp * width), lambda bi: (bi, 0, 0))]
    in_specs += [pl.BlockSpec((None, None, 2, n_kv * HEAD_DIM, lp), lambda bi: (layer, bi, 0, 0, 0))
                 for lp in past_lens]
    in_specs += [pl.BlockSpec((n_chunks, rows, lp), lambda bi: (0, 0, 0)) for lp in past_lens]
    in_specs += [pl.BlockSpec((n_chunks, rows, LANES), lambda bi: (0, 0, 0)) for _ in past_lens]
    args = [q, new_kv] + st_t + list(bias_past) + list(bias_new)
    if sinks_rows is not None:
        in_specs.append(pl.BlockSpec((n_chunks, rows, 1), lambda bi: (0, 0, 0)))
        args.append(sinks_rows)
    res = pl.pallas_call(
        functools.partial(_decode_kernel, n_grp=n_grp, n_kv=n_kv, past_lens=past_lens,
                          use_sink=sinks_rows is not None, tq=tq),
        grid=(db,),
        in_specs=in_specs,
        out_specs=[pl.BlockSpec((None, tq, N_HEADS * LANES), lambda bi: (bi, 0, 0))]
        + [pl.BlockSpec((None, 2, n_kv * HEAD_DIM, lp), lambda bi: (bi, 0, 0, 0)) for lp in past_lens],
        out_shape=[jax.ShapeDtypeStruct((db, tq, N_HEADS * LANES), F32)]
        + [jax.ShapeDtypeStruct((db, 2, n_kv * HEAD_DIM, lp), F32) for lp in past_lens],
        compiler_params=_cparams(("parallel",)),
        name=f"decode_kv{n_kv}",
    )(*args)
    new_states = [jnp.transpose(r.reshape(db, 2, n_kv, HEAD_DIM, lp), (0, 4, 1, 2, 3))
                  for r, lp in zip(res[1:], past_lens)]
    return res[0], new_states


def _ffn_kernel(*refs, n_attn, sample, tiles_per_seq, row_tile, ff_chunk, tq, dils):
    x_ref = refs[0]
    pos = 1
    if n_attn == 1:
        o_in = refs[pos]; pos += 1
    else:
        o_refs = refs[pos:pos + n_attn]; pos += n_attn
        lse_refs = refs[pos:pos + n_attn]; pos += n_attn
    wo_ref, g_ref, win_ref, cw_ref, cb_ref, wout_ref = refs[pos:pos + 6]; pos += 6
    if sample:
        fix1_ref, fix2_ref = refs[pos:pos + 2]; pos += 2
    xo_ref, gate_ref = refs[pos:pos + 2]; pos += 2
    gp_scr, halo_scr = refs[pos:pos + 2]; pos += 2
    il_scr = refs[pos:]
    i = pl.program_id(0)
    low = _low_half()

    @pl.when(i == 0)
    def _():
        halo_scr[...] = jnp.zeros(halo_scr.shape, F32)

    if n_attn == 1:
        ob = o_in[...]
    else:
        def natural(ref4, g, col0):
            d = dils[g]
            if d == 1:
                return ref4[0, :, col0:col0 + LANES]
            scr = il_scr[sum(1 for x in dils[:g] if x > 1)]
            for r in range(d):
                scr[pl.ds(r, row_tile // d, stride=d), :] = ref4[r, :, col0:col0 + LANES]
            return scr[...]

        lses = [natural(lse_refs[g], g, 0) for g in range(n_attn)]
        m_all = lses[0]
        for z in lses[1:]:
            m_all = jnp.maximum(m_all, z)
        es = [jnp.exp(z - m_all) for z in lses]
        den = es[0]
        for e in es[1:]:
            den = den + e
        ws = [e / den for e in es]
        chunks = []
        for c in range(1024 // LANES):
            acc = None
            for g in range(n_attn):
                wsel = jnp.where(low, ws[g][:, 2 * c:2 * c + 1], ws[g][:, 2 * c + 1:2 * c + 2])
                term = wsel * natural(o_refs[g], g, c * LANES)
                acc = term if acc is None else acc + term
            chunks.append(acc.astype(BF16))
        ob = jnp.concatenate(chunks, axis=1)
    x1 = x_ref[...] + _dot(ob, wo_ref[...])
    h = x1 * lax.rsqrt(jnp.mean(x1 * x1, axis=-1, keepdims=True) + RMS_EPS) * g_ref[...]
    hb = h.astype(BF16)

    if sample:
        t_in_seq = lax.broadcasted_iota(jnp.int32, (row_tile, 1), 0) & (tq - 1)
    y = jnp.zeros((row_tile, D_MODEL), F32)
    for f in range(D_FF // ff_chunk):
        f0 = f * ff_chunk
        gate = _dot(hb, win_ref[:, f0:f0 + ff_chunk])
        up = _dot(hb, win_ref[:, D_FF + f0:D_FF + f0 + ff_chunk])
        if sample:
            gp_scr[0:SUBLANES, :] = jnp.zeros((SUBLANES, ff_chunk), F32)
        else:
            first = (i % tiles_per_seq) == 0
            gp_scr[0:SUBLANES, :] = jnp.where(first, 0.0, halo_scr[f])
        gp_scr[SUBLANES:SUBLANES + row_tile, :] = gate
        g2 = gp_scr[SUBLANES - 2:SUBLANES - 2 + row_tile, :]
        g1 = gp_scr[SUBLANES - 1:SUBLANES - 1 + row_tile, :]
        if sample:
            g2 = jnp.where(t_in_seq >= 2, g2, fix2_ref[:, f0:f0 + ff_chunk])
            g1 = jnp.where(t_in_seq >= 1, g1, fix1_ref[:, f0:f0 + ff_chunk])
        gc = (cb_ref[:, f0:f0 + ff_chunk] + cw_ref[0:1, f0:f0 + ff_chunk] * g2
              + cw_ref[1:2, f0:f0 + ff_chunk] * g1 + cw_ref[2:3, f0:f0 + ff_chunk] * gate)
        act = gc * (1.0 / (1.0 + jnp.exp(-gc))) * up
        y = y + _dot(act.astype(BF16), wout_ref[f0:f0 + ff_chunk, :])
        if sample:
            gate_ref[:, f0:f0 + ff_chunk] = gate
        else:
            tail = gate[row_tile - SUBLANES:row_tile, :]
            halo_scr[f] = tail
            gate_ref[:, f0:f0 + ff_chunk] = tail
    xo_ref[...] = x1 + y


def _ffn(x, attn, lses, w_o, gain, w_in, conv_w, conv_b, w_out, fix, seq_len, row_tile=512,
         ff_chunk=1408):
    rows = x.shape[0]
    sample = fix is not None
    n_attn = len(attn)
    row_spec = lambda w: pl.BlockSpec((row_tile, w), lambda i: (i, 0))
    full = lambda a: pl.BlockSpec(a.shape, lambda i: (0,) * a.ndim, pipeline_mode=pl.Buffered(1))
    tiles = seq_len // row_tile
    res_spec = lambda a: pl.BlockSpec((None, a.shape[1], row_tile // a.shape[1], a.shape[3]),
                                      lambda i: (i // tiles, 0, i % tiles, 0))
    dils = (1,)
    if n_attn == 1:
        in_specs = [row_spec(D_MODEL), row_spec(attn[0].shape[1])]
        args = [x] + list(attn)
    else:
        dils = tuple(a.shape[1] for a in attn)
        in_specs = [row_spec(D_MODEL)] + [res_spec(a) for a in attn] + [res_spec(a) for a in lses]
        args = [x] + list(attn) + list(lses)
    consts = [w_o.astype(BF16), gain.reshape(1, D_MODEL), w_in.astype(BF16), conv_w,
              conv_b.reshape(1, D_FF), w_out.astype(BF16)]
    in_specs += [full(a) for a in consts]
    args += consts
    if sample:
        in_specs += [row_spec(D_FF), row_spec(D_FF)]
        args += list(fix)
        gate_spec = row_spec(D_FF)
        gate_shape = jax.ShapeDtypeStruct((rows, D_FF), F32)
        tiles_per_seq = 1
        tq = seq_len
    else:
        gate_spec = pl.BlockSpec((None, SUBLANES, D_FF), lambda i: (i, 0, 0))
        gate_shape = jax.ShapeDtypeStruct((rows // row_tile, SUBLANES, D_FF), F32)
        tiles_per_seq = seq_len // row_tile
        tq = 1
    return pl.pallas_call(
        functools.partial(_ffn_kernel, n_attn=n_attn, sample=sample, tiles_per_seq=tiles_per_seq,
                          row_tile=row_tile, ff_chunk=ff_chunk, tq=tq, dils=dils),
        grid=(rows // row_tile,),
        in_specs=in_specs,
        out_specs=[row_spec(D_MODEL), gate_spec],
        out_shape=[jax.ShapeDtypeStruct((rows, D_MODEL), F32), gate_shape],
        scratch_shapes=[pltpu.VMEM((SUBLANES + row_tile, ff_chunk), F32),
                        pltpu.VMEM((D_FF // ff_chunk, SUBLANES, ff_chunk), F32)]
        + [pltpu.VMEM((row_tile, LANES), F32) for d in dils if d > 1],
        compiler_params=_cparams(("arbitrary",)),
        name="ffn_sample" if sample else "ffn_prompt",
    )(*args)


def _expand_wo_sample(w_o, n_kv):
    grp_heads = N_HEADS // n_kv
    src = []
    for h in range(N_HEADS):
        rows = _cols(h * HEAD_DIM, HEAD_DIM)
        src += (_zeros(HEAD_DIM) + rows) if (h // grp_heads) % 2 else (rows + _zeros(HEAD_DIM))
    return _take_cols(w_o.T, src).T


def kernel(x_prompt, x_sample, cache_kv_a, state_kv_b, state_kv_c1, state_kv_c2, state_kv_c3, state_conv_ffn, page_table, rel_bias_table, norm_mix, norm_ffn, w_in_a, q_norm_a, k_norm_a, lambda_a, subln_a, w_o_a, w_in_b, q_norm_b, k_norm_b, sinks_b, w_o_b, w_in_c, q_norm_c, k_norm_c, w_o_c, w_ffn_in, conv_ffn_w, conv_ffn_b, w_ffn_out):
    bp, tp, _ = x_prompt.shape
    bs, ts, _ = x_sample.shape
    depth = norm_mix.shape[0]
    past_len = page_table.shape[1] * PAGE_SIZE
    xp = x_prompt.reshape(bp * tp, D_MODEL)
    xs = x_sample.reshape(bs * ts, D_MODEL)
    state_c = (state_kv_c1, state_kv_c2, state_kv_c3)

    toep = _toep_table(rel_bias_table)
    band_bias = {d: _band_bias_table(rel_bias_table, d) for d in (1, 4, 16)}
    a_cols = [[kv * 8 + j for j in range(8)] for kv in range(A_KV)]
    a_bias_past, a_bias_new = _decode_bias_table(rel_bias_table, a_cols, past_len, past_len + ts, 1, ts)
    a_bias_far = jnp.repeat(rel_bias_table[N_BUCKETS - 1][jnp.asarray(a_cols)], ts, axis=1)[..., None]
    b_cols = [list(range(N_HEADS))]
    b_len = state_kv_b.shape[2]
    b_bias_past, b_bias_new = _decode_bias_table(rel_bias_table, b_cols, b_len, B_WINDOW, 1, ts)
    c_cols = [list(range(8)), list(range(8, 16))]
    c_bias = [_decode_bias_table(rel_bias_table, c_cols, state_c[g].shape[2], w, d, ts)
              for g, (w, d) in enumerate(C_PAIRS)]

    kv_a_p, kv_a_s, kv_b_p, kv_b_s = [], [], [], []
    kv_c_p = [[] for _ in C_PAIRS]
    kv_c_s = [[] for _ in C_PAIRS]
    conv_p, conv_s = [], []
    for i in range(depth):
        n = i // N_MIXERS
        if i % N_MIXERS == 0:
            lam_init = 0.8 - 0.6 * math.exp(-0.3 * i)
            q, kvn, k, vt = _run_proj(xp, norm_mix[i], w_in_a[n], q_norm_a[n], k_norm_a[n], _layout_a(True), 512,
                                      q_scale=LOG2E, t_cols=_cols(1280, 256), seq_len=tp)
            op = _a_prompt(q.reshape(bp, tp, 1024), k.reshape(bp, tp, 256), vt, toep, lambda_a[n], subln_a[n],
                           lam_init)
            attn_p, lse_p = [op.reshape(bp * tp, 1024)], None
            kv_a_p.append(kvn.reshape(bp, tp, 2, A_KV, 2 * HEAD_DIM))
            qs, kvs = _run_proj(xs, norm_mix[i], w_in_a[n], q_norm_a[n], k_norm_a[n], _layout_a(False), 512)
            os_ = _a_sample(qs.reshape(bs, ts, 1024), kvs.reshape(bs, ts, 512), cache_kv_a, n, page_table,
                            a_bias_past, a_bias_far, a_bias_new, lambda_a[n], subln_a[n], lam_init)
            attn_s = [os_.reshape(bs * ts, 1024).astype(BF16)]
            kv_a_s.append(kvs.reshape(bs, ts, 2, A_KV, 2 * HEAD_DIM))
            w_o_p = w_o_a[n]
            w_o_s = w_o_a[n]
        elif i % N_MIXERS == 1:
            kvn, q, kd, vd = _run_proj(xp, norm_mix[i], w_in_b[n], q_norm_b[n], k_norm_b[n],
                                       _layout_band_prompt((1,), B_KV), 512)
            op = _band_prompt(q.reshape(bp, 1, tp, 1024), kd.reshape(bp, 1, tp, B_KV * LANES),
                              vd.reshape(bp, 1, tp, B_KV * LANES), band_bias[1], sinks_b[n], B_KV, False, BF16)[0]
            attn_p, lse_p = [op.reshape(bp * tp, 1024)], None
            wb = min(B_WINDOW, tp)
            kv_b_p.append(kvn.reshape(bp, tp, -1)[:, tp - wb:].reshape(bp, wb, 2, B_KV, HEAD_DIM))
            qs, kvs = _run_proj(xs, norm_mix[i], w_in_b[n], q_norm_b[n], k_norm_b[n],
                                _layout_band_sample(1, B_KV), 512)
            sink_rows = jnp.repeat(sinks_b[n], ts).reshape(1, N_HEADS * ts, 1)
            os_, nst = _decode(qs.reshape(bs, ts, -1), kvs.reshape(bs, ts, -1), [state_kv_b], n,
                               [b_bias_past], [b_bias_new], sink_rows, B_KV)
            attn_s = [os_.reshape(bs * ts, N_HEADS * LANES).astype(BF16)]
            kv_b_s.append(nst[0])
            w_o_p = w_o_b[n]
            w_o_s = _expand_wo_sample(w_o_b[n], B_KV)
        else:
            c_dils = tuple(d for _, d in C_PAIRS)
            outs = _run_proj(xp, norm_mix[i], w_in_c[n], q_norm_c[n], k_norm_c[n],
                             _layout_band_prompt(c_dils, C_KV), 256, seq_len=tp)
            kvn = outs[0]
            attn_p, lse_p = [], []
            for g, (w, d) in enumerate(C_PAIRS):
                q, kd, vd = (a.reshape(bp, d, tp // d, a.shape[-1]) for a in outs[1 + 3 * g:4 + 3 * g])
                o_g, lse_g = _band_prompt(q, kd, vd, band_bias[d], None, C_KV, True, F32)
                attn_p.append(o_g)
                lse_p.append(lse_g)
            kvn = kvn.reshape(bp, tp, -1)
            gw = 2 * C_KV * HEAD_DIM
            for g, (w, d) in enumerate(C_PAIRS):
                wc = min(w, tp)
                kv_c_p[g].append(kvn[:, tp - wc:, g * gw:(g + 1) * gw].reshape(bp, wc, 2, C_KV, HEAD_DIM))
            qs, kvs = _run_proj(xs, norm_mix[i], w_in_c[n], q_norm_c[n], k_norm_c[n],
                                _layout_band_sample(3, C_KV), 256)
            os_, nst = _decode(qs.reshape(bs, ts, -1), kvs.reshape(bs, ts, -1), list(state_c), n,
                               [cb[0] for cb in c_bias], [cb[1] for cb in c_bias], None, C_KV)
            attn_s = [os_.reshape(bs * ts, N_HEADS * LANES).astype(BF16)]
            for g in range(3):
                kv_c_s[g].append(nst[g])
            w_o_p = w_o_c[n]
            w_o_s = _expand_wo_sample(w_o_c[n], C_KV)

        xp, gate_tail = _ffn(xp, attn_p, lse_p, w_o_p, norm_ffn[i], w_ffn_in[i], conv_ffn_w[i],
                             conv_ffn_b[i], w_ffn_out[i], None, tp)
        tiles = tp // 512
        conv_p.append(gate_tail.reshape(bp, tiles, SUBLANES, D_FF)[:, -1, SUBLANES - (CONV_W - 1):])
        hist = state_conv_ffn[i]
        zero = jnp.zeros((bs, ts - 1, D_FF), F32)
        fix1 = jnp.concatenate([hist[:, 1:2], zero], axis=1).reshape(bs * ts, D_FF)
        fix2 = jnp.concatenate([hist[:, 0:1], hist[:, 1:2], zero[:, 1:]], axis=1).reshape(bs * ts, D_FF)
        xs, gate_s = _ffn(xs, attn_s, None, w_o_s, norm_ffn[i], w_ffn_in[i], conv_ffn_w[i],
                          conv_ffn_b[i], w_ffn_out[i], (fix1, fix2), ts, row_tile=256)
        conv_s.append(gate_s.reshape(bs, ts, D_FF)[:, ts - (CONV_W - 1):])

    return (xp.reshape(bp, tp, D_MODEL), xs.reshape(bs, ts, D_MODEL),
            jnp.stack(kv_a_p), jnp.stack(kv_a_s),
            jnp.stack(kv_b_p), jnp.stack(kv_b_s),
            jnp.stack(kv_c_p[0]), jnp.stack(kv_c_s[0]),
            jnp.stack(kv_c_p[1]), jnp.stack(kv_c_s[1]),
            jnp.stack(kv_c_p[2]), jnp.stack(kv_c_s[2]),
            jnp.stack(conv_p), jnp.stack(conv_s))
```

```python
import functools
import math

import numpy as np
import jax
import jax.numpy as jnp
from jax import lax
from jax.experimental import pallas as pl
from jax.experimental.pallas import tpu as pltpu

F32 = jnp.float32
BF16 = jnp.bfloat16

D_MODEL = 1024
HEAD_DIM = 64
LANES = 128
SUBLANES = 8
RMS_EPS = 1e-6
NEG = -1e30
LOG2E = math.log2(math.e)
N_HEADS = 16
N_BUCKETS = 32
MAX_DISTANCE = 2048
PAGE_SIZE = 128
D_FF = 2816
CONV_W = 3
N_MIXERS = 3
A_KV = 2
B_KV = 2
C_KV = 4
B_WINDOW = 128
C_PAIRS = ((128, 1), (512, 4), (2048, 16))
BAND = 128
ONES_ROWS = 16
VMEM_LIMIT = 56 * 1024 * 1024


def _bucket_thresholds():
    n = np.arange(0, 1 << 15)
    x = np.log(np.maximum(n, 1) / (N_BUCKETS // 2)) / math.log(MAX_DISTANCE / (N_BUCKETS // 2))
    large = N_BUCKETS // 2 + (x * (N_BUCKETS - N_BUCKETS // 2)).astype(np.int64)
    bucket = np.where(n < N_BUCKETS // 2, n, np.minimum(large, N_BUCKETS - 1))
    return [int(np.argmax(bucket >= b)) for b in range(N_BUCKETS)]


THR = _bucket_thresholds()
N_TOEP = -(-(THR[-1] + BAND - 1) // BAND)
TOEP_TILES = N_TOEP + 2


def _cparams(sem, vmem=VMEM_LIMIT):
    return pltpu.CompilerParams(dimension_semantics=sem, vmem_limit_bytes=vmem)


def _dot(a, b):
    return jnp.dot(a, b, preferred_element_type=F32)


def _dot_nt(a, b):
    return lax.dot_general(a, b, (((1,), (1,)), ((), ())), preferred_element_type=F32)


def _low_half():
    return lax.broadcasted_iota(jnp.int32, (1, LANES), 1) < HEAD_DIM


def _bias_of(dist, col, tab_ref, lo, hi):
    b_lo = max(b for b in range(N_BUCKETS) if THR[b] <= max(lo, 0))
    b_hi = max(b for b in range(N_BUCKETS) if THR[b] <= max(hi, 0))
    val = jnp.full(dist.shape, tab_ref[b_hi, col], F32)
    for b in range(b_hi - 1, b_lo - 1, -1):
        val = jnp.where(dist < THR[b + 1], tab_ref[b, col], val)
    return val


def _toep_kernel(tab_ref, o_ref):
    t = pl.program_id(0)
    key = lax.broadcasted_iota(jnp.int32, (BAND, BAND), 0)
    qry = lax.broadcasted_iota(jnp.int32, (BAND, BAND), 1)
    for delta in range(-1, N_TOEP + 1):
        @pl.when(t == delta + 1)
        def _(delta=delta):
            for c in range(N_HEADS):
                if delta < 0:
                    o_ref[c] = jnp.full((BAND, BAND), NEG, F32)
                elif delta == N_TOEP:
                    o_ref[c] = jnp.full((BAND, BAND), tab_ref[N_BUCKETS - 1, c] * LOG2E, F32)
                else:
                    dist = delta * BAND + qry - key
                    val = _bias_of(dist, c, tab_ref, delta * BAND - BAND + 1, delta * BAND + BAND - 1)
                    o_ref[c] = jnp.where(dist >= 0, val * LOG2E, NEG)


def _toep_table(table):
    return pl.pallas_call(
        _toep_kernel,
        grid=(TOEP_TILES,),
        in_specs=[pl.BlockSpec(memory_space=pltpu.SMEM)],
        out_specs=pl.BlockSpec((None, N_HEADS, BAND, BAND), lambda t: (t, 0, 0, 0)),
        out_shape=jax.ShapeDtypeStruct((TOEP_TILES, N_HEADS, BAND, BAND), F32),
        compiler_params=_cparams(("arbitrary",)),
        name="bias_toeplitz",
    )(table)


def _band_bias_kernel(tab_ref, o_ref, *, dil):
    key = lax.broadcasted_iota(jnp.int32, (2 * BAND, BAND), 0)
    qry = lax.broadcasted_iota(jnp.int32, (2 * BAND, BAND), 1)
    steps = BAND + qry - key
    valid = (steps >= 0) & (steps <= BAND)
    for c in range(N_HEADS):
        val = _bias_of(steps * dil, c, tab_ref, 0, BAND * dil)
        o_ref[:, c * BAND:(c + 1) * BAND] = jnp.where(valid, val, NEG)


def _band_bias_table(table, dil):
    return pl.pallas_call(
        functools.partial(_band_bias_kernel, dil=dil),
        in_specs=[pl.BlockSpec(memory_space=pltpu.SMEM)],
        out_shape=jax.ShapeDtypeStruct((2 * BAND, N_HEADS * BAND), F32),
        compiler_params=_cparams(None),
        name=f"bias_band_d{dil}",
    )(table)


def _decode_bias_kernel(tab_ref, past_ref, new_ref, *, cols, past_len, window, dil, tq):
    n_grp = len(cols)
    t_p = lax.broadcasted_iota(jnp.int32, (tq, past_len), 0)
    i_p = lax.broadcasted_iota(jnp.int32, (tq, past_len), 1)
    d_p = past_len + t_p - i_p
    ok_p = d_p <= window
    if dil > 1:
        ok_p = ok_p & ((d_p & (dil - 1)) == 0)
    t_n = lax.broadcasted_iota(jnp.int32, (tq, LANES), 0)
    i_n = lax.broadcasted_iota(jnp.int32, (tq, LANES), 1)
    d_n = t_n - i_n
    ok_n = (d_n >= 0) & (i_n < tq)
    if dil > 1:
        ok_n = ok_n & ((d_n & (dil - 1)) == 0)
    for r in range(n_grp):
        c = cols[r]
        vp = _bias_of(d_p, c, tab_ref, 1, min(window, past_len + tq - 1))
        past_ref[r * tq:(r + 1) * tq, :] = jnp.where(ok_p, vp, NEG)
        vn = _bias_of(d_n, c, tab_ref, 0, tq - 1)
        new_ref[r * tq:(r + 1) * tq, :] = jnp.where(ok_n, vn, NEG)


def _decode_bias_table(table, cols, past_len, window, dil, tq):
    outs = [pl.pallas_call(
        functools.partial(_decode_bias_kernel, cols=tuple(cc), past_len=past_len, window=window,
                          dil=dil, tq=tq),
        in_specs=[pl.BlockSpec(memory_space=pltpu.SMEM)],
        out_shape=(jax.ShapeDtypeStruct((len(cc) * tq, past_len), F32),
                   jax.ShapeDtypeStruct((len(cc) * tq, LANES), F32)),
        compiler_params=_cparams(None),
        name=f"bias_decode_L{past_len}_d{dil}",
    )(table) for cc in cols]
    return jnp.stack([o[0] for o in outs]), jnp.stack([o[1] for o in outs])


def _proj_kernel(x_ref, g_ref, w_ref, cg_ref, *rest, plan, col_chunk, transposed, dils, slots):
    x = x_ref[...]
    h = x * lax.rsqrt(jnp.mean(x * x, axis=-1, keepdims=True) + RMS_EPS) * g_ref[...]
    hb = h.astype(BF16)
    low = _low_half()
    row_tile = x_ref.shape[0]
    dl_scr = None
    if max(dils) > 1:
        rest, dl_scr = rest[:-1], rest[-1]
    out_refs = rest
    if transposed:
        wt_ref, out_refs, ot_ref = rest[0], rest[1:-1], rest[-1]
        ot_ref[...] = _dot_nt(wt_ref[...], hb).astype(ot_ref.dtype)
    n_total = w_ref.shape[1]
    for c0 in range(0, n_total, col_chunk):
        cw = min(col_chunk, n_total - c0)
        y = _dot(hb, w_ref[:, c0:c0 + cw])
        for j in range(cw // LANES):
            src = c0 + j * LANES
            normed, dsts = plan[src // LANES]
            blk = y[:, j * LANES:(j + 1) * LANES]
            if normed:
                sq = blk * blk
                ms_lo = jnp.sum(jnp.where(low, sq, 0.0), axis=-1, keepdims=True) * (1.0 / HEAD_DIM)
                ms_hi = jnp.sum(jnp.where(low, 0.0, sq), axis=-1, keepdims=True) * (1.0 / HEAD_DIM)
                inv = jnp.where(low, lax.rsqrt(ms_lo + RMS_EPS), lax.rsqrt(ms_hi + RMS_EPS))
                blk = blk * inv * cg_ref[:, src:src + LANES]
            if any(dils[oi] > 1 for oi, _ in dsts):
                dl_scr[...] = blk
            for oi, dst in dsts:
                if slots[oi] > 1:
                    out_refs[oi][pl.ds(dst // LANES, row_tile, stride=slots[oi]), :] = blk.astype(out_refs[oi].dtype)
                elif dils[oi] == 1:
                    out_refs[oi][:, dst:dst + LANES] = blk.astype(out_refs[oi].dtype)
                else:
                    for r in range(dils[oi]):
                        part = dl_scr[pl.ds(r, row_tile // dils[oi], stride=dils[oi]), :]
                        out_refs[oi][r, :, dst:dst + LANES] = part.astype(out_refs[oi].dtype)


class _ProjLayout:
    def __init__(self):
        self.src = []
        self.gain = []
        self.plan = []
        self.outs = []

    def add_output(self, width, dtype, dil=1, slots=1):
        self.outs.append((width, dtype, dil, slots))
        return len(self.outs) - 1

    def add_chunk(self, src_cols, kind, dsts):
        assert len(src_cols) == LANES
        self.src.extend(src_cols)
        self.gain.extend([kind] * LANES)
        self.plan.append((kind != 0, list(dsts)))


def _cols(start, n):
    return list(range(start, start + n))


def _zeros(n):
    return [-1] * n


def _take_cols(w, src):
    parts, i = [], 0
    while i < len(src):
        j = i
        if src[i] < 0:
            while j < len(src) and src[j] < 0:
                j += 1
            parts.append(jnp.zeros((w.shape[0], j - i), w.dtype))
        else:
            while j + 1 < len(src) and src[j + 1] == src[j] + 1:
                j += 1
            j += 1
            parts.append(w[:, src[i]:src[i] + j - i])
        i = j
    return jnp.concatenate(parts, axis=1) if len(parts) > 1 else parts[0]


def _run_proj(x, norm_gain, w_in, q_gain, k_gain, layout, row_tile, col_chunk=1024, q_scale=1.0,
              t_cols=None, seq_len=None):
    rows = x.shape[0]
    src = np.asarray(layout.src)
    w_ext = _take_cols(w_in, list(layout.src)).astype(BF16)
    kind = np.asarray(layout.gain)
    n_ext = len(src)
    reps = n_ext // HEAD_DIM
    cg = jnp.where(jnp.asarray(kind == 1), jnp.tile(q_gain, reps) * (HEAD_DIM ** -0.5 * q_scale),
                   jnp.where(jnp.asarray(kind == 2), jnp.tile(k_gain, reps), 1.0)).reshape(1, n_ext)
    dils = tuple(d for _, _, d, _ in layout.outs)
    slots = tuple(sl for _, _, _, sl in layout.outs)
    tiles = None if seq_len is None else seq_len // row_tile
    out_shape, out_specs = [], []
    for w, dt, d, sl in layout.outs:
        if sl > 1:
            out_shape.append(jax.ShapeDtypeStruct((rows * sl, LANES), dt))
            out_specs.append(pl.BlockSpec((row_tile * sl, LANES), lambda i: (i, 0)))
        elif d == 1:
            out_shape.append(jax.ShapeDtypeStruct((rows, w), dt))
            out_specs.append(pl.BlockSpec((row_tile, w), lambda i: (i, 0)))
        else:
            out_shape.append(jax.ShapeDtypeStruct((rows // seq_len, d, seq_len // d, w), dt))
            out_specs.append(pl.BlockSpec((None, d, row_tile // d, w), lambda i: (i // tiles, 0, i % tiles, 0)))
    in_specs = [pl.BlockSpec((row_tile, D_MODEL), lambda i: (i, 0)),
                pl.BlockSpec((1, D_MODEL), lambda i: (0, 0)),
                pl.BlockSpec((D_MODEL, n_ext), lambda i: (0, 0), pipeline_mode=pl.Buffered(1)),
                pl.BlockSpec((1, n_ext), lambda i: (0, 0))]
    args = [x, norm_gain.reshape(1, D_MODEL), w_ext, cg]
    if t_cols is not None:
        n_t = len(t_cols)
        in_specs.append(pl.BlockSpec((n_t, D_MODEL), lambda i: (0, 0), pipeline_mode=pl.Buffered(1)))
        args.append(_take_cols(w_in, list(t_cols)).T.astype(BF16))
        out_shape.append(jax.ShapeDtypeStruct((rows // seq_len, n_t, seq_len), BF16))
        out_specs.append(pl.BlockSpec((None, n_t, row_tile), lambda i: (i // tiles, 0, i % tiles)))
    return pl.pallas_call(
        functools.partial(_proj_kernel, plan=tuple(layout.plan), col_chunk=col_chunk,
                          transposed=t_cols is not None, dils=dils, slots=slots),
        grid=(rows // row_tile,),
        in_specs=in_specs,
        out_specs=out_specs,
        out_shape=out_shape,
        scratch_shapes=[pltpu.VMEM((row_tile, LANES), F32)] if max(dils) > 1 else [],
        compiler_params=_cparams(("parallel",)),
        name="proj",
    )(*args)


def _layout_a(prompt):
    lay = _ProjLayout()
    oq = lay.add_output(1024, BF16 if prompt else F32)
    okv = lay.add_output(512, F32, slots=4 if prompt else 1)
    ok = lay.add_output(256, BF16) if prompt else None
    for c in range(8):
        lay.add_chunk(_cols(c * LANES, LANES), 1, [(oq, c * LANES)])
    for c in range(2):
        lay.add_chunk(_cols(1024 + c * LANES, LANES), 2,
                      [(okv, c * LANES)] + ([(ok, c * LANES)] if prompt else []))
    for c in range(2):
        lay.add_chunk(_cols(1280 + c * LANES, LANES), 0, [(okv, 256 + c * LANES)])
    return lay


def _layout_band_prompt(dils, n_kv):
    n_grp = len(dils)
    nq = n_grp * N_HEADS * HEAD_DIM
    nk = n_grp * n_kv * HEAD_DIM
    gk = n_kv * HEAD_DIM
    lay = _ProjLayout()
    okv = lay.add_output(2 * nk, F32)
    for g, d in enumerate(dils):
        oq = lay.add_output(N_HEADS * HEAD_DIM, BF16, d)
        okd = lay.add_output(n_kv * LANES, BF16, d)
        ovd = lay.add_output(n_kv * LANES, BF16, d)
        for c in range(N_HEADS * HEAD_DIM // LANES):
            lay.add_chunk(_cols(g * N_HEADS * HEAD_DIM + c * LANES, LANES), 1, [(oq, c * LANES)])
        for c in range(gk // LANES):
            lay.add_chunk(_cols(nq + g * gk + c * LANES, LANES), 2, [(okv, g * 2 * gk + c * LANES)])
        for c in range(gk // LANES):
            lay.add_chunk(_cols(nq + nk + g * gk + c * LANES, LANES), 0, [(okv, g * 2 * gk + gk + c * LANES)])
        for kv in range(n_kv):
            lay.add_chunk(_cols(nq + g * gk + kv * HEAD_DIM, HEAD_DIM) * 2, 2, [(okd, kv * LANES)])
        for kv in range(n_kv):
            lay.add_chunk(_cols(nq + nk + g * gk + kv * HEAD_DIM, HEAD_DIM) * 2, 0, [(ovd, kv * LANES)])
    return lay


def _layout_band_sample(n_grp, n_kv):
    nq = n_grp * N_HEADS * HEAD_DIM
    nk = n_grp * n_kv * HEAD_DIM
    grp_heads = N_HEADS // n_kv
    lay = _ProjLayout()
    oq = lay.add_output(n_grp * N_HEADS * LANES, F32)
    okv = lay.add_output(2 * nk, F32)
    for g in range(n_grp):
        for h in range(N_HEADS):
            qc = _cols((g * N_HEADS + h) * HEAD_DIM, HEAD_DIM)
            odd = (h // grp_heads) % 2
            lay.add_chunk(_zeros(HEAD_DIM) + qc if odd else qc + _zeros(HEAD_DIM), 1,
                          [(oq, (g * N_HEADS + h) * LANES)])
    gk = n_kv * HEAD_DIM
    for g in range(n_grp):
        for c in range(gk // LANES):
            lay.add_chunk(_cols(nq + g * gk + c * LANES, LANES), 2, [(okv, g * 2 * gk + c * LANES)])
        for c in range(gk // LANES):
            lay.add_chunk(_cols(nq + nk + g * gk + c * LANES, LANES), 0, [(okv, g * 2 * gk + gk + c * LANES)])
    return lay


def _diff_finalize(acc0, l0, acc1, l1, lam_ref, subln_ref, lam_init):
    lp = lam_ref[...]
    lam = (jnp.exp(jnp.sum(lp[0:1] * lp[1:2], axis=-1, keepdims=True))
           - jnp.exp(jnp.sum(lp[2:3] * lp[3:4], axis=-1, keepdims=True)) + lam_init)
    o = acc0 / l0 - lam * (acc1 / l1)
    o = o * lax.rsqrt(jnp.mean(o * o, axis=-1, keepdims=True) + RMS_EPS) * subln_ref[...]
    return o * (1.0 - lam_init)


def _a_prompt_kernel(q_ref, k_ref, vt_ref, toep_ref, lam_ref, subln_ref, o_ref,
                     qm_scr, ta_scr, tb_scr, p_scr, cm_scr, m_scr, acc_scr, *, tq, tk, lam_init, rb):
    qi = pl.program_id(2)
    low = _low_half()
    nh = 8
    w = nh * tq
    for g in range(4):
        qg = q_ref[:, g * LANES:(g + 1) * LANES]
        qm_scr[(2 * g) * tq:(2 * g + 1) * tq, :] = jnp.where(low, qg, jnp.zeros_like(qg))
        qm_scr[(2 * g + 1) * tq:(2 * g + 2) * tq, :] = jnp.where(low, jnp.zeros_like(qg), qg)
    m_scr[...] = jnp.full(m_scr.shape, NEG, F32)
    acc_scr[...] = jnp.zeros(acc_scr.shape, F32)
    n_chunks = (qi * tq + tq - 1) // tk + 1
    last = k_ref.shape[0] // tk - 1

    def stage_a(kj, t_ref):
        k0 = pl.multiple_of(jnp.minimum(kj, last) * tk, tk)
        s = _dot_nt(k_ref[pl.ds(k0, tk), :], qm_scr[...])
        macc = jnp.full((SUBLANES, w), NEG, F32)
        for u in range(tk // BAND):
            idx = [jnp.clip((qi * (tq // BAND) + r) - (kj * (tk // BAND) + u) + 1, 0, N_TOEP + 1)
                   for r in range(tq // BAND)]
            for b0 in range(0, BAND, rb):
                bias = jnp.concatenate([toep_ref[idx[r], h, b0:b0 + rb, :] for h in range(nh)
                                        for r in range(tq // BAND)], axis=1)
                r0 = u * BAND + b0
                t = s[r0:r0 + rb, :] + bias
                t_ref[r0:r0 + rb, :] = t
                for g8 in range(rb // SUBLANES):
                    macc = jnp.maximum(macc, t[g8 * SUBLANES:(g8 + 1) * SUBLANES, :])
        return jnp.max(macc, axis=0, keepdims=True)

    def stage_b(kj, t_ref, m_new, alpha):
        k0 = pl.multiple_of(jnp.minimum(kj, last) * tk, tk)
        vtc = jnp.concatenate([vt_ref[:, pl.ds(k0, tk)], jnp.ones((ONES_ROWS, tk), BF16)], axis=0)
        mb = jnp.broadcast_to(m_new, (SUBLANES, w))
        for r0 in range(0, tk, rb):
            t = t_ref[r0:r0 + rb, :]
            ps = []
            for g8 in range(rb // SUBLANES):
                p8 = jnp.exp2(t[g8 * SUBLANES:(g8 + 1) * SUBLANES, :] - mb)
                ps.append(p8)
            p_scr[r0:r0 + rb, :] = jnp.concatenate(ps, axis=0).astype(BF16)
        acc_scr[...] = alpha * acc_scr[...] + _dot(vtc, p_scr[...])

    cm_scr[...] = stage_a(0, ta_scr)

    def body(j, carry):
        cmax1 = stage_a(2 * j + 1, tb_scr)
        m_prev = m_scr[...]
        m0 = jnp.maximum(m_prev, cm_scr[...])
        stage_b(2 * j, ta_scr, m0, jnp.exp2(m_prev - m0))
        cm_scr[...] = stage_a(2 * j + 2, ta_scr)
        m1 = jnp.maximum(m0, cmax1)
        stage_b(2 * j + 1, tb_scr, m1, jnp.exp2(m0 - m1))
        m_scr[...] = m1
        return carry

    lax.fori_loop(0, (n_chunks + 1) // 2, body, 0)
    lp = lam_ref[...]
    lam = (jnp.exp(jnp.sum(lp[0:1] * lp[1:2], axis=-1, keepdims=True))
           - jnp.exp(jnp.sum(lp[2:3] * lp[3:4], axis=-1, keepdims=True)) + lam_init)
    on = acc_scr[0:LANES, :] / acc_scr[LANES:LANES + 1, :]
    for g in range(4):
        o = on[:, (2 * g) * tq:(2 * g + 1) * tq] - lam * on[:, (2 * g + 1) * tq:(2 * g + 2) * tq]
        o = o * lax.rsqrt(jnp.mean(o * o, axis=0, keepdims=True) + RMS_EPS) * subln_ref[...]
        o = o * (1.0 - lam_init)
        o_ref[:, g * LANES:(g + 1) * LANES] = o.T.astype(o_ref.dtype)


def _a_prompt(q, k, vt, toep_t, lam_p, subln, lam_init, tq=256, tk=256, rb=32):
    b, t, _ = q.shape
    return pl.pallas_call(
        functools.partial(_a_prompt_kernel, tq=tq, tk=tk, lam_init=lam_init, rb=rb),
        grid=(b, A_KV, t // tq),
        in_specs=[pl.BlockSpec((None, tq, 512), lambda bi, kv, qi: (bi, qi, kv)),
                  pl.BlockSpec((None, t, LANES), lambda bi, kv, qi: (bi, 0, kv)),
                  pl.BlockSpec((None, LANES, t), lambda bi, kv, qi: (bi, kv, 0)),
                  pl.BlockSpec((TOEP_TILES, 8, BAND, BAND), lambda bi, kv, qi: (0, kv, 0, 0)),
                  pl.BlockSpec((4, HEAD_DIM), lambda bi, kv, qi: (0, 0)),
                  pl.BlockSpec((LANES, 1), lambda bi, kv, qi: (0, 0))],
        out_specs=pl.BlockSpec((None, tq, 512), lambda bi, kv, qi: (bi, qi, kv)),
        out_shape=jax.ShapeDtypeStruct((b, t, 1024), BF16),
        scratch_shapes=[pltpu.VMEM((8 * tq, LANES), BF16),
                        pltpu.VMEM((tk, 8 * tq), F32),
                        pltpu.VMEM((tk, 8 * tq), F32),
                        pltpu.VMEM((tk, 8 * tq), BF16),
                        pltpu.VMEM((1, 8 * tq), F32),
                        pltpu.VMEM((1, 8 * tq), F32),
                        pltpu.VMEM((LANES + ONES_ROWS, 8 * tq), F32)],
        compiler_params=_cparams(("parallel", "parallel", "arbitrary")),
        name="a_prompt",
    )(q, k, vt, toep_t, lam_p, subln.reshape(LANES, 1))


def _a_sample_kernel(pt_ref, q_ref, new_ref, bp_ref, bc_ref, bn_ref, lam_ref, subln_ref, *rest,
                     pages, tq, lam_init):
    page_refs = rest[:pages]
    o_ref = rest[pages]
    qm_scr, m_scr, l_scr, acc_scr = rest[pages + 1:]
    c = pl.program_id(1)
    last = pl.num_programs(1) - 1
    low = _low_half()
    rows = 8 * tq

    @pl.when(c == 0)
    def _():
        for kv in range(A_KV):
            for g in range(4):
                qg = q_ref[:, (kv * 4 + g) * LANES:(kv * 4 + g + 1) * LANES]
                qm_scr[kv, (2 * g) * tq:(2 * g + 1) * tq, :] = jnp.where(low, qg, 0.0)
                qm_scr[kv, (2 * g + 1) * tq:(2 * g + 2) * tq, :] = jnp.where(low, 0.0, qg)
        m_scr[...] = jnp.full(m_scr.shape, NEG, F32)
        l_scr[...] = jnp.zeros(l_scr.shape, F32)
        acc_scr[...] = jnp.zeros(acc_scr.shape, F32)

    n_slot = 2 * A_KV
    is_last = c == last

    def scores(kv):
        qm = qm_scr[kv].astype(BF16)
        keys = jnp.concatenate([pr[pl.ds(kv, PAGE_SIZE, stride=n_slot), :] for pr in page_refs], axis=0)
        return _dot_nt(qm, keys.astype(BF16)) + jnp.where(is_last, bp_ref[kv], bc_ref[kv])

    def values(kv):
        return jnp.concatenate([pr[pl.ds(A_KV + kv, PAGE_SIZE, stride=n_slot), :] for pr in page_refs], axis=0).astype(BF16)

    s = jnp.concatenate([scores(kv) for kv in range(A_KV)], axis=0)
    m_prev = m_scr[...]
    m_new = jnp.maximum(m_prev, jnp.max(s, axis=-1, keepdims=True))
    alpha = jnp.exp(m_prev - m_new)
    p = jnp.exp(s - m_new)
    l_scr[...] = alpha * l_scr[...] + jnp.sum(p, axis=-1, keepdims=True)
    m_scr[...] = m_new
    pb = p.astype(BF16)
    pv = jnp.concatenate([_dot(pb[kv * rows:(kv + 1) * rows], values(kv)) for kv in range(A_KV)], axis=0)
    acc_scr[...] = alpha * acc_scr[...] + pv

    @pl.when(is_last)
    def _():
        pad = jnp.zeros((LANES - tq, LANES), F32)
        for kv in range(A_KV):
            sl = slice(kv * rows, (kv + 1) * rows)
            qm = qm_scr[kv].astype(BF16)
            kn = jnp.concatenate([new_ref[:, kv * LANES:(kv + 1) * LANES], pad], axis=0)
            vn = jnp.concatenate([new_ref[:, (A_KV + kv) * LANES:(A_KV + kv + 1) * LANES], pad], axis=0)
            s = _dot_nt(qm, kn.astype(BF16)) + bn_ref[kv]
            m_prev = m_scr[sl]
            m_new = jnp.maximum(m_prev, jnp.max(s, axis=-1, keepdims=True))
            alpha = jnp.exp(m_prev - m_new)
            p = jnp.exp(s - m_new)
            l = alpha * l_scr[sl] + jnp.sum(p, axis=-1, keepdims=True)
            acc = alpha * acc_scr[sl] + _dot(p.astype(BF16), vn.astype(BF16))
            for g in range(4):
                r0, r1 = (2 * g) * tq, (2 * g + 1) * tq
                o = _diff_finalize(acc[r0:r0 + tq], l[r0:r0 + tq], acc[r1:r1 + tq], l[r1:r1 + tq],
                                   lam_ref, subln_ref, lam_init)
                o_ref[:, (kv * 4 + g) * LANES:(kv * 4 + g + 1) * LANES] = o.astype(o_ref.dtype)


def _a_sample(q, new_kv, cache, layer, page_table, bias_past, bias_const, bias_new, lam_p, subln, lam_init,
              pages=64):
    db, tq, _ = q.shape
    n_pages = page_table.shape[1]
    n_chunks = n_pages // pages
    assert n_chunks == 1 or pages * PAGE_SIZE + 1 >= THR[-1]
    cache2 = cache.reshape(cache.shape[0], cache.shape[1], PAGE_SIZE * 2 * A_KV, LANES)
    rows = 8 * tq

    def page_spec(j):
        return pl.BlockSpec((None, None, PAGE_SIZE * 2 * A_KV, LANES),
                            lambda bi, c, pt: (layer, pt[bi * n_pages + c * pages + j], 0, 0))

    grid_spec = pltpu.PrefetchScalarGridSpec(
        num_scalar_prefetch=1,
        grid=(db, n_chunks),
        in_specs=[pl.BlockSpec((None, tq, 1024), lambda bi, c, pt: (bi, 0, 0)),
                  pl.BlockSpec((None, tq, 512), lambda bi, c, pt: (bi, 0, 0)),
                  pl.BlockSpec((A_KV, rows, pages * PAGE_SIZE), lambda bi, c, pt: (0, 0, n_chunks - 1)),
                  pl.BlockSpec((A_KV, rows, 1), lambda bi, c, pt: (0, 0, 0)),
                  pl.BlockSpec((A_KV, rows, LANES), lambda bi, c, pt: (0, 0, 0)),
                  pl.BlockSpec((4, HEAD_DIM), lambda bi, c, pt: (0, 0)),
                  pl.BlockSpec((1, LANES), lambda bi, c, pt: (0, 0))]
                 + [page_spec(j) for j in range(pages)],
        out_specs=pl.BlockSpec((None, tq, 1024), lambda bi, c, pt: (bi, 0, 0)),
        scratch_shapes=[pltpu.VMEM((A_KV, rows, LANES), F32),
                        pltpu.VMEM((A_KV * rows, 1), F32),
                        pltpu.VMEM((A_KV * rows, 1), F32),
                        pltpu.VMEM((A_KV * rows, LANES), F32)])
    return pl.pallas_call(
        functools.partial(_a_sample_kernel, pages=pages, tq=tq, lam_init=lam_init),
        grid_spec=grid_spec,
        out_shape=jax.ShapeDtypeStruct((db, tq, 1024), F32),
        compiler_params=_cparams(("parallel", "arbitrary")),
        name="a_sample",
    )(page_table.reshape(-1), q, new_kv, bias_past, bias_const, bias_new, lam_p, subln.reshape(1, LANES),
      *([cache2] * pages))


def _band_kernel(*refs, n_kv, use_sink, want_lse, n_sub):
    q_ref, kp_ref, kc_ref, vp_ref, vc_ref, bias_ref = refs[:6]
    pos = 6
    sink_ref = None
    if use_sink:
        sink_ref = refs[pos]
        pos += 1
    o_ref = refs[pos]
    lse_ref = refs[pos + 1] if want_lse else None
    qm_scr = refs[-1]
    i = pl.program_id(2)
    grp = N_HEADS // n_kv
    low = _low_half()
    key = lax.broadcasted_iota(jnp.int32, (2 * BAND, 1), 0)
    ones = jnp.ones((ONES_ROWS, 2 * BAND), BF16)
    for j in range(n_sub):
        rows = slice(j * BAND, (j + 1) * BAND)
        prv = slice((j - 1) * BAND, j * BAND)

        def kv_pair(pref, cref, kv):
            lanes = slice(kv * LANES, (kv + 1) * LANES)
            before = pref[:, lanes] if j == 0 else cref[prv, lanes]
            return jnp.concatenate([before, cref[rows, lanes]], axis=0)

        for c in range(N_HEADS // 2):
            qp = q_ref[rows, c * LANES:(c + 1) * LANES]
            zero = jnp.zeros_like(qp)
            qm_scr[(2 * c) * BAND:(2 * c + 1) * BAND, :] = jnp.where(low, qp, zero)
            qm_scr[(2 * c + 1) * BAND:(2 * c + 2) * BAND, :] = jnp.where(low, zero, qp)
        s_parts = [_dot_nt(kv_pair(kp_ref, kc_ref, kv), qm_scr[kv * grp * BAND:(kv + 1) * grp * BAND, :])
                   for kv in range(n_kv)]
        s = jnp.concatenate(s_parts, axis=1) + bias_ref[...]
        if j == 0:
            s = jnp.where((key >= BAND) | (i > 0), s, NEG)
        m = jnp.max(s, axis=0, keepdims=True)
        if use_sink:
            m = jnp.maximum(m, sink_ref[...])
        p = jnp.exp(s - m).astype(BF16)
        acc_parts = []
        for kv in range(n_kv):
            vt = kv_pair(vp_ref, vc_ref, kv).astype(F32).T[0:HEAD_DIM, :].astype(BF16)
            acc_parts.append(_dot(jnp.concatenate([vt, ones], axis=0), p[:, kv * grp * BAND:(kv + 1) * grp * BAND]))
        acc = jnp.concatenate(acc_parts, axis=1)
        l = acc[HEAD_DIM:HEAD_DIM + 1, :]
        if use_sink:
            l = l + jnp.exp(sink_ref[...] - m)
        on = acc[0:HEAD_DIM, :] / l
        for c in range(N_HEADS // 2):
            pair = jnp.concatenate([on[:, (2 * c) * BAND:(2 * c + 1) * BAND], on[:, (2 * c + 1) * BAND:(2 * c + 2) * BAND]], axis=0)
            o_ref[rows, c * LANES:(c + 1) * LANES] = pair.T.astype(o_ref.dtype)
        if want_lse:
            lse = m + jnp.log(l)
            lrows = jnp.concatenate([lse[:, h * BAND:(h + 1) * BAND] for h in range(N_HEADS)]
                                    + [jnp.zeros((LANES - N_HEADS, BAND), F32)], axis=0)
            lse_ref[rows, :] = lrows.T


def _band_prompt(q, kd, vd, bias, sinks, n_kv, want_lse, out_dtype):
    b, dil, ts, _ = q.shape
    n_sub = 2 if ts % (2 * BAND) == 0 else 1
    blk = n_sub * BAND
    nb = ts // blk
    ck = n_kv * LANES
    cur = lambda bi, r, i: (bi, r, i, 0)
    prev = lambda bi, r, i: (bi, r, jnp.maximum(i * n_sub - 1, 0), 0)
    in_specs = [pl.BlockSpec((None, None, blk, 1024), cur),
                pl.BlockSpec((None, None, BAND, ck), prev),
                pl.BlockSpec((None, None, blk, ck), cur),
                pl.BlockSpec((None, None, BAND, ck), prev),
                pl.BlockSpec((None, None, blk, ck), cur),
                pl.BlockSpec((2 * BAND, N_HEADS * BAND), lambda bi, r, i: (0, 0))]
    args = [q, kd, kd, vd, vd, bias]
    if sinks is not None:
        in_specs.append(pl.BlockSpec((1, N_HEADS * BAND), lambda bi, r, i: (0, 0)))
        args.append(jnp.repeat(sinks, BAND).reshape(1, N_HEADS * BAND))
    out_specs = [pl.BlockSpec((None, None, blk, 1024), cur)]
    out_shape = [jax.ShapeDtypeStruct((b, dil, ts, 1024), out_dtype)]
    if want_lse:
        out_specs.append(pl.BlockSpec((None, None, blk, LANES), cur))
        out_shape.append(jax.ShapeDtypeStruct((b, dil, ts, LANES), F32))
    return pl.pallas_call(
        functools.partial(_band_kernel, n_kv=n_kv, use_sink=sinks is not None, want_lse=want_lse, n_sub=n_sub),
        grid=(b, dil, nb),
        in_specs=in_specs,
        out_specs=out_specs,
        out_shape=out_shape,
        scratch_shapes=[pltpu.VMEM((N_HEADS * BAND, LANES), BF16)],
        compiler_params=_cparams(("parallel", "parallel", "arbitrary")),
        name=f"band_d{dil}",
    )(*args)


def _decode_kernel(*refs, n_grp, n_kv, past_lens, use_sink, tq):
    q_ref, new_ref = refs[:2]
    pos = 2
    st_refs = refs[pos:pos + n_grp]; pos += n_grp
    bp_refs = refs[pos:pos + n_grp]; pos += n_grp
    bn_refs = refs[pos:pos + n_grp]; pos += n_grp
    sink_ref = None
    if use_sink:
        sink_ref = refs[pos]; pos += 1
    o_ref = refs[pos]; pos += 1
    nst_refs = refs[pos:pos + n_grp]
    width = 2 * n_kv * HEAD_DIM
    n_chunks = n_kv // 2
    heads_per_chunk = N_HEADS // n_chunks
    grp_heads = N_HEADS // n_kv
    low = _low_half()
    pad = jnp.zeros((LANES - tq, LANES), F32)
    for c in range(n_chunks):
        parts = []
        for g in range(n_grp):
            qs = jnp.concatenate(
                [q_ref[:, (g * N_HEADS + c * heads_per_chunk + hh) * LANES:(g * N_HEADS + c * heads_per_chunk + hh + 1) * LANES]
                 for hh in range(heads_per_chunk)], axis=0).astype(BF16)
            st = st_refs[g]
            kt = st[0, c * LANES:(c + 1) * LANES, :].astype(BF16)
            vt = st[1, c * LANES:(c + 1) * LANES, :].astype(BF16)
            kn = jnp.concatenate([new_ref[:, g * width + c * LANES:g * width + (c + 1) * LANES], pad], axis=0).astype(BF16)
            vn = jnp.concatenate([new_ref[:, g * width + n_kv * HEAD_DIM + c * LANES:g * width + n_kv * HEAD_DIM + (c + 1) * LANES], pad], axis=0).astype(BF16)
            sp = _dot(qs, kt) + bp_refs[g][c]
            sn = _dot_nt(qs, kn) + bn_refs[g][c]
            m = jnp.maximum(jnp.max(sp, axis=-1, keepdims=True), jnp.max(sn, axis=-1, keepdims=True))
            if use_sink:
                m = jnp.maximum(m, sink_ref[c])
            pp = jnp.exp(sp - m)
            pn = jnp.exp(sn - m)
            l = jnp.sum(pp, axis=-1, keepdims=True) + jnp.sum(pn, axis=-1, keepdims=True)
            if use_sink:
                l = l + jnp.exp(sink_ref[c] - m)
            acc = _dot_nt(pp.astype(BF16), vt) + _dot(pn.astype(BF16), vn)
            parts.append((m, l, acc))
        m_all = parts[0][0]
        for m, _, _ in parts[1:]:
            m_all = jnp.maximum(m_all, m)
        num = 0.0
        den = 0.0
        for m, l, acc in parts:
            w = jnp.exp(m - m_all)
            num = num + acc * w
            den = den + l * w
        o = num / den
        for hh in range(heads_per_chunk):
            h = c * heads_per_chunk + hh
            odd = (h // grp_heads) % 2
            oh = o[hh * tq:(hh + 1) * tq, :]
            o_ref[:, h * LANES:(h + 1) * LANES] = (jnp.where(low, 0.0, oh) if odd else jnp.where(low, oh, 0.0)).astype(o_ref.dtype)
    n_rows = n_kv * HEAD_DIM
    for g in range(n_grp):
        lp = past_lens[g]
        for kv_slot in range(2):
            c0 = g * width + kv_slot * n_rows
            new_t = jnp.concatenate([new_ref[:, c0:c0 + n_rows], jnp.zeros((LANES - tq, n_rows), F32)], axis=0).T
            for r0 in range(0, n_rows, HEAD_DIM):
                if lp > tq:
                    nst_refs[g][kv_slot, r0:r0 + HEAD_DIM, 0:lp - tq] = st_refs[g][kv_slot, r0:r0 + HEAD_DIM, tq:lp]
                nst_refs[g][kv_slot, r0:r0 + HEAD_DIM, lp - tq:lp] = new_t[r0:r0 + HEAD_DIM, 0:tq]


def _decode(q, new_kv, states, layer, bias_past, bias_new, sinks_rows, n_kv):
    db, tq, _ = q.shape
    n_grp = len(states)
    width = 2 * n_kv * HEAD_DIM
    n_chunks = n_kv // 2
    rows = tq * N_HEADS // n_chunks
    past_lens = tuple(s.shape[2] for s in states)
    st_t = [jnp.transpose(s, (0, 1, 3, 4, 5, 2)).reshape(s.shape[0], db, 2, n_kv * HEAD_DIM, s.shape[2])
            for s in states]
    in_specs = [pl.BlockSpec((None, tq, q.shape[2]), lambda bi: (bi, 0, 0)),
                pl.BlockSpec((None, tq, n_grp * width), lambda bi: (bi, 0, 0))]
    in_specs += [pl.BlockSpec((None, None, 2, n_kv * HEAD_DIM, lp), lambda bi: (layer, bi, 0, 0, 0))
                 for lp in past_lens]
    in_specs += [pl.BlockSpec((n_chunks, rows, lp), lambda bi: (0, 0, 0)) for lp in past_lens]
    in_specs += [pl.BlockSpec((n_chunks, rows, LANES), lambda bi: (0, 0, 0)) for _ in past_lens]
    args = [q, new_kv] + st_t + list(bias_past) + list(bias_new)
    if sinks_rows is not None:
        in_specs.append(pl.BlockSpec((n_chunks, rows, 1), lambda bi: (0, 0, 0)))
        args.append(sinks_rows)
    res = pl.pallas_call(
        functools.partial(_decode_kernel, n_grp=n_grp, n_kv=n_kv, past_lens=past_lens,
                          use_sink=sinks_rows is not None, tq=tq),
        grid=(db,),
        in_specs=in_specs,
        out_specs=[pl.BlockSpec((None, tq, N_HEADS * LANES), lambda bi: (bi, 0, 0))]
        + [pl.BlockSpec((None, 2, n_kv * HEAD_DIM, lp), lambda bi: (bi, 0, 0, 0)) for lp in past_lens],
        out_shape=[jax.ShapeDtypeStruct((db, tq, N_HEADS * LANES), F32)]
        + [jax.ShapeDtypeStruct((db, 2, n_kv * HEAD_DIM, lp), F32) for lp in past_lens],
        compiler_params=_cparams(("parallel",)),
        name=f"decode_kv{n_kv}",
    )(*args)
    new_states = [jnp.transpose(r.reshape(db, 2, n_kv, HEAD_DIM, lp), (0, 4, 1, 2, 3))
                  for r, lp in zip(res[1:], past_lens)]
    return res[0], new_states


def _ffn_kernel(*refs, n_attn, sample, tiles_per_seq, row_tile, ff_chunk, tq, dils):
    x_ref = refs[0]
    pos = 1
    if n_attn == 1:
        o_in = refs[pos]; pos += 1
    else:
        o_refs = refs[pos:pos + n_attn]; pos += n_attn
        lse_refs = refs[pos:pos + n_attn]; pos += n_attn
    wo_ref, g_ref, win_ref, cw_ref, cb_ref, wout_ref = refs[pos:pos + 6]; pos += 6
    if sample:
        fix1_ref, fix2_ref = refs[pos:pos + 2]; pos += 2
    xo_ref, gate_ref = refs[pos:pos + 2]; pos += 2
    gp_scr, halo_scr = refs[pos:pos + 2]; pos += 2
    il_scr = refs[pos:]
    i = pl.program_id(0)
    low = _low_half()

    @pl.when(i == 0)
    def _():
        halo_scr[...] = jnp.zeros(halo_scr.shape, F32)

    if n_attn == 1:
        ob = o_in[...]
    else:
        def natural(ref4, g, col0):
            d = dils[g]
            if d == 1:
                return ref4[0, :, col0:col0 + LANES]
            scr = il_scr[sum(1 for x in dils[:g] if x > 1)]
            for r in range(d):
                scr[pl.ds(r, row_tile // d, stride=d), :] = ref4[r, :, col0:col0 + LANES]
            return scr[...]

        lses = [natural(lse_refs[g], g, 0) for g in range(n_attn)]
        m_all = lses[0]
        for z in lses[1:]:
            m_all = jnp.maximum(m_all, z)
        es = [jnp.exp(z - m_all) for z in lses]
        den = es[0]
        for e in es[1:]:
            den = den + e
        ws = [e / den for e in es]
        chunks = []
        for c in range(1024 // LANES):
            acc = None
            for g in range(n_attn):
                wsel = jnp.where(low, ws[g][:, 2 * c:2 * c + 1], ws[g][:, 2 * c + 1:2 * c + 2])
                term = wsel * natural(o_refs[g], g, c * LANES)
                acc = term if acc is None else acc + term
            chunks.append(acc.astype(BF16))
        ob = jnp.concatenate(chunks, axis=1)
    x1 = x_ref[...] + _dot(ob, wo_ref[...])
    h = x1 * lax.rsqrt(jnp.mean(x1 * x1, axis=-1, keepdims=True) + RMS_EPS) * g_ref[...]
    hb = h.astype(BF16)

    if sample:
        t_in_seq = lax.broadcasted_iota(jnp.int32, (row_tile, 1), 0) & (tq - 1)
    y = jnp.zeros((row_tile, D_MODEL), F32)
    for f in range(D_FF // ff_chunk):
        f0 = f * ff_chunk
        gate = _dot(hb, win_ref[:, f0:f0 + ff_chunk])
        up = _dot(hb, win_ref[:, D_FF + f0:D_FF + f0 + ff_chunk])
        if sample:
            gp_scr[0:SUBLANES, :] = jnp.zeros((SUBLANES, ff_chunk), F32)
        else:
            first = (i % tiles_per_seq) == 0
            gp_scr[0:SUBLANES, :] = jnp.where(first, 0.0, halo_scr[f])
        gp_scr[SUBLANES:SUBLANES + row_tile, :] = gate
        g2 = gp_scr[SUBLANES - 2:SUBLANES - 2 + row_tile, :]
        g1 = gp_scr[SUBLANES - 1:SUBLANES - 1 + row_tile, :]
        if sample:
            g2 = jnp.where(t_in_seq >= 2, g2, fix2_ref[:, f0:f0 + ff_chunk])
            g1 = jnp.where(t_in_seq >= 1, g1, fix1_ref[:, f0:f0 + ff_chunk])
        gc = (cb_ref[:, f0:f0 + ff_chunk] + cw_ref[0:1, f0:f0 + ff_chunk] * g2
              + cw_ref[1:2, f0:f0 + ff_chunk] * g1 + cw_ref[2:3, f0:f0 + ff_chunk] * gate)
        act = gc * (1.0 / (1.0 + jnp.exp(-gc))) * up
        y = y + _dot(act.astype(BF16), wout_ref[f0:f0 + ff_chunk, :])
        if sample:
            gate_ref[:, f0:f0 + ff_chunk] = gate
        else:
            tail = gate[row_tile - SUBLANES:row_tile, :]
            halo_scr[f] = tail
            gate_ref[:, f0:f0 + ff_chunk] = tail
    xo_ref[...] = x1 + y


def _ffn(x, attn, lses, w_o, gain, w_in, conv_w, conv_b, w_out, fix, seq_len, row_tile=512,
         ff_chunk=1408):
    rows = x.shape[0]
    sample = fix is not None
    n_attn = len(attn)
    row_spec = lambda w: pl.BlockSpec((row_tile, w), lambda i: (i, 0))
    full = lambda a: pl.BlockSpec(a.shape, lambda i: (0,) * a.ndim, pipeline_mode=pl.Buffered(1))
    tiles = seq_len // row_tile
    res_spec = lambda a: pl.BlockSpec((None, a.shape[1], row_tile // a.shape[1], a.shape[3]),
                                      lambda i: (i // tiles, 0, i % tiles, 0))
    dils = (1,)
    if n_attn == 1:
        in_specs = [row_spec(D_MODEL), row_spec(attn[0].shape[1])]
        args = [x] + list(attn)
    else:
        dils = tuple(a.shape[1] for a in attn)
        in_specs = [row_spec(D_MODEL)] + [res_spec(a) for a in attn] + [res_spec(a) for a in lses]
        args = [x] + list(attn) + list(lses)
    consts = [w_o.astype(BF16), gain.reshape(1, D_MODEL), w_in.astype(BF16), conv_w,
              conv_b.reshape(1, D_FF), w_out.astype(BF16)]
    in_specs += [full(a) for a in consts]
    args += consts
    if sample:
        in_specs += [row_spec(D_FF), row_spec(D_FF)]
        args += list(fix)
        gate_spec = row_spec(D_FF)
        gate_shape = jax.ShapeDtypeStruct((rows, D_FF), F32)
        tiles_per_seq = 1
        tq = seq_len
    else:
        gate_spec = pl.BlockSpec((None, SUBLANES, D_FF), lambda i: (i, 0, 0))
        gate_shape = jax.ShapeDtypeStruct((rows // row_tile, SUBLANES, D_FF), F32)
        tiles_per_seq = seq_len // row_tile
        tq = 1
    return pl.pallas_call(
        functools.partial(_ffn_kernel, n_attn=n_attn, sample=sample, tiles_per_seq=tiles_per_seq,
                          row_tile=row_tile, ff_chunk=ff_chunk, tq=tq, dils=dils),
        grid=(rows // row_tile,),
        in_specs=in_specs,
        out_specs=[row_spec(D_MODEL), gate_spec],
        out_shape=[jax.ShapeDtypeStruct((rows, D_MODEL), F32), gate_shape],
        scratch_shapes=[pltpu.VMEM((SUBLANES + row_tile, ff_chunk), F32),
                        pltpu.VMEM((D_FF // ff_chunk, SUBLANES, ff_chunk), F32)]
        + [pltpu.VMEM((row_tile, LANES), F32) for d in dils if d > 1],
        compiler_params=_cparams(("arbitrary",)),
        name="ffn_sample" if sample else "ffn_prompt",
    )(*args)


def _expand_wo_sample(w_o, n_kv):
    grp_heads = N_HEADS // n_kv
    src = []
    for h in range(N_HEADS):
        rows = _cols(h * HEAD_DIM, HEAD_DIM)
        src += (_zeros(HEAD_DIM) + rows) if (h // grp_heads) % 2 else (rows + _zeros(HEAD_DIM))
    return _take_cols(w_o.T, src).T


def kernel(x_prompt, x_sample, cache_kv_a, state_kv_b, state_kv_c1, state_kv_c2, state_kv_c3, state_conv_ffn, page_table, rel_bias_table, norm_mix, norm_ffn, w_in_a, q_norm_a, k_norm_a, lambda_a, subln_a, w_o_a, w_in_b, q_norm_b, k_norm_b, sinks_b, w_o_b, w_in_c, q_norm_c, k_norm_c, w_o_c, w_ffn_in, conv_ffn_w, conv_ffn_b, w_ffn_out):
    bp, tp, _ = x_prompt.shape
    bs, ts, _ = x_sample.shape
    depth = norm_mix.shape[0]
    past_len = page_table.shape[1] * PAGE_SIZE
    xp = x_prompt.reshape(bp * tp, D_MODEL)
    xs = x_sample.reshape(bs * ts, D_MODEL)
    state_c = (state_kv_c1, state_kv_c2, state_kv_c3)

    toep = _toep_table(rel_bias_table)
    band_bias = {d: _band_bias_table(rel_bias_table, d) for d in (1, 4, 16)}
    a_cols = [[kv * 8 + j for j in range(8)] for kv in range(A_KV)]
    a_bias_past, a_bias_new = _decode_bias_table(rel_bias_table, a_cols, past_len, past_len + ts, 1, ts)
    a_bias_far = jnp.repeat(rel_bias_table[N_BUCKETS - 1][jnp.asarray(a_cols)], ts, axis=1)[..., None]
    b_cols = [list(range(N_HEADS))]
    b_len = state_kv_b.shape[2]
    b_bias_past, b_bias_new = _decode_bias_table(rel_bias_table, b_cols, b_len, B_WINDOW, 1, ts)
    c_cols = [list(range(8)), list(range(8, 16))]
    c_bias = [_decode_bias_table(rel_bias_table, c_cols, state_c[g].shape[2], w, d, ts)
              for g, (w, d) in enumerate(C_PAIRS)]

    kv_a_p, kv_a_s, kv_b_p, kv_b_s = [], [], [], []
    kv_c_p = [[] for _ in C_PAIRS]
    kv_c_s = [[] for _ in C_PAIRS]
    conv_p, conv_s = [], []
    for i in range(depth):
        n = i // N_MIXERS
        if i % N_MIXERS == 0:
            lam_init = 0.8 - 0.6 * math.exp(-0.3 * i)
            q, kvn, k, vt = _run_proj(xp, norm_mix[i], w_in_a[n], q_norm_a[n], k_norm_a[n], _layout_a(True), 512,
                                      q_scale=LOG2E, t_cols=_cols(1280, 256), seq_len=tp)
            op = _a_prompt(q.reshape(bp, tp, 1024), k.reshape(bp, tp, 256), vt, toep, lambda_a[n], subln_a[n],
                           lam_init)
            attn_p, lse_p = [op.reshape(bp * tp, 1024)], None
            kv_a_p.append(kvn.reshape(bp, tp, 2, A_KV, 2 * HEAD_DIM))
            qs, kvs = _run_proj(xs, norm_mix[i], w_in_a[n], q_norm_a[n], k_norm_a[n], _layout_a(False), 512)
            os_ = _a_sample(qs.reshape(bs, ts, 1024), kvs.reshape(bs, ts, 512), cache_kv_a, n, page_table,
                            a_bias_past, a_bias_far, a_bias_new, lambda_a[n], subln_a[n], lam_init)
            attn_s = [os_.reshape(bs * ts, 1024).astype(BF16)]
            kv_a_s.append(kvs.reshape(bs, ts, 2, A_KV, 2 * HEAD_DIM))
            w_o_p = w_o_a[n]
            w_o_s = w_o_a[n]
        elif i % N_MIXERS == 1:
            kvn, q, kd, vd = _run_proj(xp, norm_mix[i], w_in_b[n], q_norm_b[n], k_norm_b[n],
                                       _layout_band_prompt((1,), B_KV), 512)
            op = _band_prompt(q.reshape(bp, 1, tp, 1024), kd.reshape(bp, 1, tp, B_KV * LANES),
                              vd.reshape(bp, 1, tp, B_KV * LANES), band_bias[1], sinks_b[n], B_KV, False, BF16)[0]
            attn_p, lse_p = [op.reshape(bp * tp, 1024)], None
            wb = min(B_WINDOW, tp)
            kv_b_p.append(kvn.reshape(bp, tp, -1)[:, tp - wb:].reshape(bp, wb, 2, B_KV, HEAD_DIM))
            qs, kvs = _run_proj(xs, norm_mix[i], w_in_b[n], q_norm_b[n], k_norm_b[n],
                                _layout_band_sample(1, B_KV), 512)
            sink_rows = jnp.repeat(sinks_b[n], ts).reshape(1, N_HEADS * ts, 1)
            os_, nst = _decode(qs.reshape(bs, ts, -1), kvs.reshape(bs, ts, -1), [state_kv_b], n,
                               [b_bias_past], [b_bias_new], sink_rows, B_KV)
            attn_s = [os_.reshape(bs * ts, N_HEADS * LANES).astype(BF16)]
            kv_b_s.append(nst[0])
            w_o_p = w_o_b[n]
            w_o_s = _expand_wo_sample(w_o_b[n], B_KV)
        else:
            c_dils = tuple(d for _, d in C_PAIRS)
            outs = _run_proj(xp, norm_mix[i], w_in_c[n], q_norm_c[n], k_norm_c[n],
                             _layout_band_prompt(c_dils, C_KV), 256, seq_len=tp)
            kvn = outs[0]
            attn_p, lse_p = [], []
            for g, (w, d) in enumerate(C_PAIRS):
                q, kd, vd = (a.reshape(bp, d, tp // d, a.shape[-1]) for a in outs[1 + 3 * g:4 + 3 * g])
                o_g, lse_g = _band_prompt(q, kd, vd, band_bias[d], None, C_KV, True, F32)
                attn_p.append(o_g)
                lse_p.append(lse_g)
            kvn = kvn.reshape(bp, tp, -1)
            gw = 2 * C_KV * HEAD_DIM
            for g, (w, d) in enumerate(C_PAIRS):
                wc = min(w, tp)
                kv_c_p[g].append(kvn[:, tp - wc:, g * gw:(g + 1) * gw].reshape(bp, wc, 2, C_KV, HEAD_DIM))
            qs, kvs = _run_proj(xs, norm_mix[i], w_in_c[n], q_norm_c[n], k_norm_c[n],
                                _layout_band_sample(3, C_KV), 256)
            os_, nst = _decode(qs.reshape(bs, ts, -1), kvs.reshape(bs, ts, -1), list(state_c), n,
                               [cb[0] for cb in c_bias], [cb[1] for cb in c_bias], None, C_KV)
            attn_s = [os_.reshape(bs * ts, N_HEADS * LANES).astype(BF16)]
            for g in range(3):
                kv_c_s[g].append(nst[g])
            w_o_p = w_o_c[n]
            w_o_s = _expand_wo_sample(w_o_c[n], C_KV)

        xp, gate_tail = _ffn(xp, attn_p, lse_p, w_o_p, norm_ffn[i], w_ffn_in[i], conv_ffn_w[i],
                             conv_ffn_b[i], w_ffn_out[i], None, tp)
        tiles = tp // 512
        conv_p.append(gate_tail.reshape(bp, tiles, SUBLANES, D_FF)[:, -1, SUBLANES - (CONV_W - 1):])
        hist = state_conv_ffn[i]
        zero = jnp.zeros((bs, ts - 1, D_FF), F32)
        fix1 = jnp.concatenate([hist[:, 1:2], zero], axis=1).reshape(bs * ts, D_FF)
        fix2 = jnp.concatenate([hist[:, 0:1], hist[:, 1:2], zero[:, 1:]], axis=1).reshape(bs * ts, D_FF)
        xs, gate_s = _ffn(xs, attn_s, None, w_o_s, norm_ffn[i], w_ffn_in[i], conv_ffn_w[i],
                          conv_ffn_b[i], w_ffn_out[i], (fix1, fix2), ts, row_tile=256)
        conv_s.append(gate_s.reshape(bs, ts, D_FF)[:, ts - (CONV_W - 1):])

    return (xp.reshape(bp, tp, D_MODEL), xs.reshape(bs, ts, D_MODEL),
            jnp.stack(kv_a_p), jnp.stack(kv_a_s),
            jnp.stack(kv_b_p), jnp.stack(kv_b_s),
            jnp.stack(kv_c_p[0]), jnp.stack(kv_c_s[0]),
            jnp.stack(kv_c_p[1]), jnp.stack(kv_c_s[1]),
            jnp.stack(kv_c_p[2]), jnp.stack(kv_c_s[2]),
            jnp.stack(conv_p), jnp.stack(conv_s))
```

```python
import functools
import math

import numpy as np
import jax
import jax.numpy as jnp
from jax import lax
from jax.experimental import pallas as pl
from jax.experimental.pallas import tpu as pltpu

F32 = jnp.float32
BF16 = jnp.bfloat16

D_MODEL = 1024
HEAD_DIM = 64
LANES = 128
SUBLANES = 8
RMS_EPS = 1e-6
NEG = -1e30
LOG2E = math.log2(math.e)
N_HEADS = 16
N_BUCKETS = 32
MAX_DISTANCE = 2048
PAGE_SIZE = 128
D_FF = 2816
CONV_W = 3
N_MIXERS = 3
A_KV = 2
B_KV = 2
C_KV = 4
B_WINDOW = 128
C_PAIRS = ((128, 1), (512, 4), (2048, 16))
BAND = 128
ONES_ROWS = 16
VMEM_LIMIT = 56 * 1024 * 1024


def _bucket_thresholds():
    n = np.arange(0, 1 << 15)
    x = np.log(np.maximum(n, 1) / (N_BUCKETS // 2)) / math.log(MAX_DISTANCE / (N_BUCKETS // 2))
    large = N_BUCKETS // 2 + (x * (N_BUCKETS - N_BUCKETS // 2)).astype(np.int64)
    bucket = np.where(n < N_BUCKETS // 2, n, np.minimum(large, N_BUCKETS - 1))
    return [int(np.argmax(bucket >= b)) for b in range(N_BUCKETS)]


THR = _bucket_thresholds()
N_TOEP = -(-(THR[-1] + BAND - 1) // BAND)
TOEP_TILES = N_TOEP + 2


def _cparams(sem, vmem=VMEM_LIMIT):
    return pltpu.CompilerParams(dimension_semantics=sem, vmem_limit_bytes=vmem)


def _dot(a, b):
    return jnp.dot(a, b, preferred_element_type=F32)


def _dot_nt(a, b):
    return lax.dot_general(a, b, (((1,), (1,)), ((), ())), preferred_element_type=F32)


def _low_half():
    return lax.broadcasted_iota(jnp.int32, (1, LANES), 1) < HEAD_DIM


def _bias_of(dist, col, tab_ref, lo, hi):
    b_lo = max(b for b in range(N_BUCKETS) if THR[b] <= max(lo, 0))
    b_hi = max(b for b in range(N_BUCKETS) if THR[b] <= max(hi, 0))
    val = jnp.full(dist.shape, tab_ref[b_hi, col], F32)
    for b in range(b_hi - 1, b_lo - 1, -1):
        val = jnp.where(dist < THR[b + 1], tab_ref[b, col], val)
    return val


def _toep_kernel(tab_ref, o_ref):
    t = pl.program_id(0)
    key = lax.broadcasted_iota(jnp.int32, (BAND, BAND), 0)
    qry = lax.broadcasted_iota(jnp.int32, (BAND, BAND), 1)
    for delta in range(-1, N_TOEP + 1):
        @pl.when(t == delta + 1)
        def _(delta=delta):
            for c in range(N_HEADS):
                if delta < 0:
                    o_ref[c] = jnp.full((BAND, BAND), NEG, F32)
                elif delta == N_TOEP:
                    o_ref[c] = jnp.full((BAND, BAND), tab_ref[N_BUCKETS - 1, c] * LOG2E, F32)
                else:
                    dist = delta * BAND + qry - key
                    val = _bias_of(dist, c, tab_ref, delta * BAND - BAND + 1, delta * BAND + BAND - 1)
                    o_ref[c] = jnp.where(dist >= 0, val * LOG2E, NEG)


def _toep_table(table):
    return pl.pallas_call(
        _toep_kernel,
        grid=(TOEP_TILES,),
        in_specs=[pl.BlockSpec(memory_space=pltpu.SMEM)],
        out_specs=pl.BlockSpec((None, N_HEADS, BAND, BAND), lambda t: (t, 0, 0, 0)),
        out_shape=jax.ShapeDtypeStruct((TOEP_TILES, N_HEADS, BAND, BAND), F32),
        compiler_params=_cparams(("arbitrary",)),
        name="bias_toeplitz",
    )(table)


def _band_bias_kernel(tab_ref, o_ref, *, dil):
    key = lax.broadcasted_iota(jnp.int32, (2 * BAND, BAND), 0)
    qry = lax.broadcasted_iota(jnp.int32, (2 * BAND, BAND), 1)
    steps = BAND + qry - key
    valid = (steps >= 0) & (steps <= BAND)
    for c in range(N_HEADS):
        val = _bias_of(steps * dil, c, tab_ref, 0, BAND * dil)
        o_ref[:, c * BAND:(c + 1) * BAND] = jnp.where(valid, val, NEG)


def _band_bias_table(table, dil):
    return pl.pallas_call(
        functools.partial(_band_bias_kernel, dil=dil),
        in_specs=[pl.BlockSpec(memory_space=pltpu.SMEM)],
        out_shape=jax.ShapeDtypeStruct((2 * BAND, N_HEADS * BAND), F32),
        compiler_params=_cparams(None),
        name=f"bias_band_d{dil}",
    )(table)


def _decode_bias_kernel(tab_ref, past_ref, new_ref, *, cols, past_len, window, dil, tq):
    n_grp = len(cols)
    t_p = lax.broadcasted_iota(jnp.int32, (tq, past_len), 0)
    i_p = lax.broadcasted_iota(jnp.int32, (tq, past_len), 1)
    d_p = past_len + t_p - i_p
    ok_p = d_p <= window
    if dil > 1:
        ok_p = ok_p & ((d_p & (dil - 1)) == 0)
    t_n = lax.broadcasted_iota(jnp.int32, (tq, LANES), 0)
    i_n = lax.broadcasted_iota(jnp.int32, (tq, LANES), 1)
    d_n = t_n - i_n
    ok_n = (d_n >= 0) & (i_n < tq)
    if dil > 1:
        ok_n = ok_n & ((d_n & (dil - 1)) == 0)
    for r in range(n_grp):
        c = cols[r]
        vp = _bias_of(d_p, c, tab_ref, 1, min(window, past_len + tq - 1))
        past_ref[r * tq:(r + 1) * tq, :] = jnp.where(ok_p, vp, NEG)
        vn = _bias_of(d_n, c, tab_ref, 0, tq - 1)
        new_ref[r * tq:(r + 1) * tq, :] = jnp.where(ok_n, vn, NEG)


def _decode_bias_table(table, cols, past_len, window, dil, tq):
    outs = [pl.pallas_call(
        functools.partial(_decode_bias_kernel, cols=tuple(cc), past_len=past_len, window=window,
                          dil=dil, tq=tq),
        in_specs=[pl.BlockSpec(memory_space=pltpu.SMEM)],
        out_shape=(jax.ShapeDtypeStruct((len(cc) * tq, past_len), F32),
                   jax.ShapeDtypeStruct((len(cc) * tq, LANES), F32)),
        compiler_params=_cparams(None),
        name=f"bias_decode_L{past_len}_d{dil}",
    )(table) for cc in cols]
    return jnp.stack([o[0] for o in outs]), jnp.stack([o[1] for o in outs])


def _proj_kernel(x_ref, g_ref, w_ref, cg_ref, *rest, plan, col_chunk, transposed, dils, slots):
    x = x_ref[...]
    h = x * lax.rsqrt(jnp.mean(x * x, axis=-1, keepdims=True) + RMS_EPS) * g_ref[...]
    hb = h.astype(BF16)
    low = _low_half()
    row_tile = x_ref.shape[0]
    dl_scr = None
    if max(dils) > 1:
        rest, dl_scr = rest[:-1], rest[-1]
    out_refs = rest
    if transposed:
        wt_ref, out_refs, ot_ref = rest[0], rest[1:-1], rest[-1]
        ot_ref[...] = _dot_nt(wt_ref[...], hb).astype(ot_ref.dtype)
    n_total = w_ref.shape[1]
    for c0 in range(0, n_total, col_chunk):
        cw = min(col_chunk, n_total - c0)
        y = _dot(hb, w_ref[:, c0:c0 + cw])
        for j in range(cw // LANES):
            src = c0 + j * LANES
            normed, dsts = plan[src // LANES]
            blk = y[:, j * LANES:(j + 1) * LANES]
            if normed:
                sq = blk * blk
                ms_lo = jnp.sum(jnp.where(low, sq, 0.0), axis=-1, keepdims=True) * (1.0 / HEAD_DIM)
                ms_hi = jnp.sum(jnp.where(low, 0.0, sq), axis=-1, keepdims=True) * (1.0 / HEAD_DIM)
                inv = jnp.where(low, lax.rsqrt(ms_lo + RMS_EPS), lax.rsqrt(ms_hi + RMS_EPS))
                blk = blk * inv * cg_ref[:, src:src + LANES]
            if any(dils[oi] > 1 for oi, _ in dsts):
                dl_scr[...] = blk
            for oi, dst in dsts:
                if slots[oi] > 1:
                    out_refs[oi][pl.ds(dst // LANES, row_tile, stride=slots[oi]), :] = blk.astype(out_refs[oi].dtype)
                elif dils[oi] == 1:
                    out_refs[oi][:, dst:dst + LANES] = blk.astype(out_refs[oi].dtype)
                else:
                    for r in range(dils[oi]):
                        part = dl_scr[pl.ds(r, row_tile // dils[oi], stride=dils[oi]), :]
                        out_refs[oi][r, :, dst:dst + LANES] = part.astype(out_refs[oi].dtype)


class _ProjLayout:
    def __init__(self):
        self.src = []
        self.gain = []
        self.plan = []
        self.outs = []

    def add_output(self, width, dtype, dil=1, slots=1):
        self.outs.append((width, dtype, dil, slots))
        return len(self.outs) - 1

    def add_chunk(self, src_cols, kind, dsts):
        assert len(src_cols) == LANES
        self.src.extend(src_cols)
        self.gain.extend([kind] * LANES)
        self.plan.append((kind != 0, list(dsts)))


def _cols(start, n):
    return list(range(start, start + n))


def _zeros(n):
    return [-1] * n


def _take_cols(w, src):
    parts, i = [], 0
    while i < len(src):
        j = i
        if src[i] < 0:
            while j < len(src) and src[j] < 0:
                j += 1
            parts.append(jnp.zeros((w.shape[0], j - i), w.dtype))
        else:
            while j + 1 < len(src) and src[j + 1] == src[j] + 1:
                j += 1
            j += 1
            parts.append(w[:, src[i]:src[i] + j - i])
        i = j
    return jnp.concatenate(parts, axis=1) if len(parts) > 1 else parts[0]


def _run_proj(x, norm_gain, w_in, q_gain, k_gain, layout, row_tile, col_chunk=1024, q_scale=1.0,
              t_cols=None, seq_len=None):
    rows = x.shape[0]
    src = np.asarray(layout.src)
    w_ext = _take_cols(w_in, list(layout.src)).astype(BF16)
    kind = np.asarray(layout.gain)
    n_ext = len(src)
    reps = n_ext // HEAD_DIM
    cg = jnp.where(jnp.asarray(kind == 1), jnp.tile(q_gain, reps) * (HEAD_DIM ** -0.5 * q_scale),
                   jnp.where(jnp.asarray(kind == 2), jnp.tile(k_gain, reps), 1.0)).reshape(1, n_ext)
    dils = tuple(d for _, _, d, _ in layout.outs)
    slots = tuple(sl for _, _, _, sl in layout.outs)
    tiles = None if seq_len is None else seq_len // row_tile
    out_shape, out_specs = [], []
    for w, dt, d, sl in layout.outs:
        if sl > 1:
            out_shape.append(jax.ShapeDtypeStruct((rows * sl, LANES), dt))
            out_specs.append(pl.BlockSpec((row_tile * sl, LANES), lambda i: (i, 0)))
        elif d == 1:
            out_shape.append(jax.ShapeDtypeStruct((rows, w), dt))
            out_specs.append(pl.BlockSpec((row_tile, w), lambda i: (i, 0)))
        else:
            out_shape.append(jax.ShapeDtypeStruct((rows // seq_len, d, seq_len // d, w), dt))
            out_specs.append(pl.BlockSpec((None, d, row_tile // d, w), lambda i: (i // tiles, 0, i % tiles, 0)))
    in_specs = [pl.BlockSpec((row_tile, D_MODEL), lambda i: (i, 0)),
                pl.BlockSpec((1, D_MODEL), lambda i: (0, 0)),
                pl.BlockSpec((D_MODEL, n_ext), lambda i: (0, 0), pipeline_mode=pl.Buffered(1)),
                pl.BlockSpec((1, n_ext), lambda i: (0, 0))]
    args = [x, norm_gain.reshape(1, D_MODEL), w_ext, cg]
    if t_cols is not None:
        n_t = len(t_cols)
        in_specs.append(pl.BlockSpec((n_t, D_MODEL), lambda i: (0, 0), pipeline_mode=pl.Buffered(1)))
        args.append(_take_cols(w_in, list(t_cols)).T.astype(BF16))
        out_shape.append(jax.ShapeDtypeStruct((rows // seq_len, n_t, seq_len), BF16))
        out_specs.append(pl.BlockSpec((None, n_t, row_tile), lambda i: (i // tiles, 0, i % tiles)))
    return pl.pallas_call(
        functools.partial(_proj_kernel, plan=tuple(layout.plan), col_chunk=col_chunk,
                          transposed=t_cols is not None, dils=dils, slots=slots),
        grid=(rows // row_tile,),
        in_specs=in_specs,
        out_specs=out_specs,
        out_shape=out_shape,
        scratch_shapes=[pltpu.VMEM((row_tile, LANES), F32)] if max(dils) > 1 else [],
        compiler_params=_cparams(("parallel",)),
        name="proj",
    )(*args)


def _layout_a(prompt):
    lay = _ProjLayout()
    oq = lay.add_output(1024, BF16 if prompt else F32)
    okv = lay.add_output(512, F32, slots=4 if prompt else 1)
    ok = lay.add_output(256, BF16) if prompt else None
    for c in range(8):
        lay.add_chunk(_cols(c * LANES, LANES), 1, [(oq, c * LANES)])
    for c in range(2):
        lay.add_chunk(_cols(1024 + c * LANES, LANES), 2,
                      [(okv, c * LANES)] + ([(ok, c * LANES)] if prompt else []))
    for c in range(2):
        lay.add_chunk(_cols(1280 + c * LANES, LANES), 0, [(okv, 256 + c * LANES)])
    return lay


def _layout_band_prompt(dils, n_kv):
    n_grp = len(dils)
    nq = n_grp * N_HEADS * HEAD_DIM
    nk = n_grp * n_kv * HEAD_DIM
    gk = n_kv * HEAD_DIM
    lay = _ProjLayout()
    okv = lay.add_output(2 * nk, F32)
    for g, d in enumerate(dils):
        oq = lay.add_output(N_HEADS * HEAD_DIM, BF16, d)
        okd = lay.add_output(n_kv * LANES, BF16, d)
        ovd = lay.add_output(n_kv * LANES, BF16, d)
        for c in range(N_HEADS * HEAD_DIM // LANES):
            lay.add_chunk(_cols(g * N_HEADS * HEAD_DIM + c * LANES, LANES), 1, [(oq, c * LANES)])
        for c in range(gk // LANES):
            lay.add_chunk(_cols(nq + g * gk + c * LANES, LANES), 2, [(okv, g * 2 * gk + c * LANES)])
        for c in range(gk // LANES):
            lay.add_chunk(_cols(nq + nk + g * gk + c * LANES, LANES), 0, [(okv, g * 2 * gk + gk + c * LANES)])
        for kv in range(n_kv):
            lay.add_chunk(_cols(nq + g * gk + kv * HEAD_DIM, HEAD_DIM) * 2, 2, [(okd, kv * LANES)])
        for kv in range(n_kv):
            lay.add_chunk(_cols(nq + nk + g * gk + kv * HEAD_DIM, HEAD_DIM) * 2, 0, [(ovd, kv * LANES)])
    return lay


def _layout_band_sample(n_grp, n_kv):
    nq = n_grp * N_HEADS * HEAD_DIM
    nk = n_grp * n_kv * HEAD_DIM
    grp_heads = N_HEADS // n_kv
    lay = _ProjLayout()
    oq = lay.add_output(n_grp * N_HEADS * LANES, F32)
    okv = lay.add_output(2 * nk, F32)
    for g in range(n_grp):
        for h in range(N_HEADS):
            qc = _cols((g * N_HEADS + h) * HEAD_DIM, HEAD_DIM)
            odd = (h // grp_heads) % 2
            lay.add_chunk(_zeros(HEAD_DIM) + qc if odd else qc + _zeros(HEAD_DIM), 1,
                          [(oq, (g * N_HEADS + h) * LANES)])
    gk = n_kv * HEAD_DIM
    for g in range(n_grp):
        for c in range(gk // LANES):
            lay.add_chunk(_cols(nq + g * gk + c * LANES, LANES), 2, [(okv, g * 2 * gk + c * LANES)])
        for c in range(gk // LANES):
            lay.add_chunk(_cols(nq + nk + g * gk + c * LANES, LANES), 0, [(okv, g * 2 * gk + gk + c * LANES)])
    return lay


def _diff_finalize(acc0, l0, acc1, l1, lam_ref, subln_ref, lam_init):
    lp = lam_ref[...]
    lam = (jnp.exp(jnp.sum(lp[0:1] * lp[1:2], axis=-1, keepdims=True))
           - jnp.exp(jnp.sum(lp[2:3] * lp[3:4], axis=-1, keepdims=True)) + lam_init)
    o = acc0 / l0 - lam * (acc1 / l1)
    o = o * lax.rsqrt(jnp.mean(o * o, axis=-1, keepdims=True) + RMS_EPS) * subln_ref[...]
    return o * (1.0 - lam_init)


def _a_prompt_kernel(q_ref, k_ref, vt_ref, toep_ref, lam_ref, subln_ref, o_ref,
                     qm_scr, ta_scr, tb_scr, p_scr, cm_scr, m_scr, acc_scr, *, tq, tk, lam_init, rb):
    qi = pl.program_id(2)
    low = _low_half()
    nh = 8
    w = nh * tq
    for g in range(4):
        qg = q_ref[:, g * LANES:(g + 1) * LANES]
        qm_scr[(2 * g) * tq:(2 * g + 1) * tq, :] = jnp.where(low, qg, jnp.zeros_like(qg))
        qm_scr[(2 * g + 1) * tq:(2 * g + 2) * tq, :] = jnp.where(low, jnp.zeros_like(qg), qg)
    m_scr[...] = jnp.full(m_scr.shape, NEG, F32)
    acc_scr[...] = jnp.zeros(acc_scr.shape, F32)
    n_chunks = (qi * tq + tq - 1) // tk + 1
    last = k_ref.shape[0] // tk - 1

    def stage_a(kj, t_ref):
        k0 = pl.multiple_of(jnp.minimum(kj, last) * tk, tk)
        s = _dot_nt(k_ref[pl.ds(k0, tk), :], qm_scr[...])
        macc = jnp.full((SUBLANES, w), NEG, F32)
        for u in range(tk // BAND):
            idx = [jnp.clip((qi * (tq // BAND) + r) - (kj * (tk // BAND) + u) + 1, 0, N_TOEP + 1)
                   for r in range(tq // BAND)]
            for b0 in range(0, BAND, rb):
                bias = jnp.concatenate([toep_ref[idx[r], h, b0:b0 + rb, :] for h in range(nh)
                                        for r in range(tq // BAND)], axis=1)
                r0 = u * BAND + b0
                t = s[r0:r0 + rb, :] + bias
                t_ref[r0:r0 + rb, :] = t
                for g8 in range(rb // SUBLANES):
                    macc = jnp.maximum(macc, t[g8 * SUBLANES:(g8 + 1) * SUBLANES, :])
        return jnp.max(macc, axis=0, keepdims=True)

    def stage_b(kj, t_ref, m_new, alpha):
        k0 = pl.multiple_of(jnp.minimum(kj, last) * tk, tk)
        vtc = jnp.concatenate([vt_ref[:, pl.ds(k0, tk)], jnp.ones((ONES_ROWS, tk), BF16)], axis=0)
        mb = jnp.broadcast_to(m_new, (SUBLANES, w))
        for r0 in range(0, tk, rb):
            t = t_ref[r0:r0 + rb, :]
            ps = []
            for g8 in range(rb // SUBLANES):
                p8 = jnp.exp2(t[g8 * SUBLANES:(g8 + 1) * SUBLANES, :] - mb)
                ps.append(p8)
            p_scr[r0:r0 + rb, :] = jnp.concatenate(ps, axis=0).astype(BF16)
        acc_scr[...] = alpha * acc_scr[...] + _dot(vtc, p_scr[...])

    cm_scr[...] = stage_a(0, ta_scr)

    def body(j, carry):
        cmax1 = stage_a(2 * j + 1, tb_scr)
        m_prev = m_scr[...]
        m0 = jnp.maximum(m_prev, cm_scr[...])
        stage_b(2 * j, ta_scr, m0, jnp.exp2(m_prev - m0))
        cm_scr[...] = stage_a(2 * j + 2, ta_scr)
        m1 = jnp.maximum(m0, cmax1)
        stage_b(2 * j + 1, tb_scr, m1, jnp.exp2(m0 - m1))
        m_scr[...] = m1
        return carry

    lax.fori_loop(0, (n_chunks + 1) // 2, body, 0)
    lp = lam_ref[...]
    lam = (jnp.exp(jnp.sum(lp[0:1] * lp[1:2], axis=-1, keepdims=True))
           - jnp.exp(jnp.sum(lp[2:3] * lp[3:4], axis=-1, keepdims=True)) + lam_init)
    on = acc_scr[0:LANES, :] / acc_scr[LANES:LANES + 1, :]
    for g in range(4):
        o = on[:, (2 * g) * tq:(2 * g + 1) * tq] - lam * on[:, (2 * g + 1) * tq:(2 * g + 2) * tq]
        o = o * lax.rsqrt(jnp.mean(o * o, axis=0, keepdims=True) + RMS_EPS) * subln_ref[...]
        o = o * (1.0 - lam_init)
        o_ref[:, g * LANES:(g + 1) * LANES] = o.T.astype(o_ref.dtype)


def _a_prompt(q, k, vt, toep_t, lam_p, subln, lam_init, tq=128, tk=512, rb=32):
    b, t, _ = q.shape
    return pl.pallas_call(
        functools.partial(_a_prompt_kernel, tq=tq, tk=tk, lam_init=lam_init, rb=rb),
        grid=(b, A_KV, t // tq),
        in_specs=[pl.BlockSpec((None, tq, 512), lambda bi, kv, qi: (bi, qi, kv)),
                  pl.BlockSpec((None, t, LANES), lambda bi, kv, qi: (bi, 0, kv)),
                  pl.BlockSpec((None, LANES, t), lambda bi, kv, qi: (bi, kv, 0)),
                  pl.BlockSpec((TOEP_TILES, 8, BAND, BAND), lambda bi, kv, qi: (0, kv, 0, 0)),
                  pl.BlockSpec((4, HEAD_DIM), lambda bi, kv, qi: (0, 0)),
                  pl.BlockSpec((LANES, 1), lambda bi, kv, qi: (0, 0))],
        out_specs=pl.BlockSpec((None, tq, 512), lambda bi, kv, qi: (bi, qi, kv)),
        out_shape=jax.ShapeDtypeStruct((b, t, 1024), BF16),
        scratch_shapes=[pltpu.VMEM((8 * tq, LANES), BF16),
                        pltpu.VMEM((tk, 8 * tq), F32),
                        pltpu.VMEM((tk, 8 * tq), F32),
                        pltpu.VMEM((tk, 8 * tq), BF16),
                        pltpu.VMEM((1, 8 * tq), F32),
                        pltpu.VMEM((1, 8 * tq), F32),
                        pltpu.VMEM((LANES + ONES_ROWS, 8 * tq), F32)],
        compiler_params=_cparams(("parallel", "parallel", "arbitrary")),
        name="a_prompt",
    )(q, k, vt, toep_t, lam_p, subln.reshape(LANES, 1))


def _a_sample_kernel(pt_ref, q_ref, new_ref, bp_ref, bc_ref, bn_ref, lam_ref, subln_ref, *rest,
                     pages, tq, lam_init):
    page_refs = rest[:pages]
    o_ref = rest[pages]
    qm_scr, m_scr, l_scr, acc_scr = rest[pages + 1:]
    c = pl.program_id(1)
    last = pl.num_programs(1) - 1
    low = _low_half()
    rows = 8 * tq

    @pl.when(c == 0)
    def _():
        for kv in range(A_KV):
            for g in range(4):
                qg = q_ref[:, (kv * 4 + g) * LANES:(kv * 4 + g + 1) * LANES]
                qm_scr[kv, (2 * g) * tq:(2 * g + 1) * tq, :] = jnp.where(low, qg, 0.0)
                qm_scr[kv, (2 * g + 1) * tq:(2 * g + 2) * tq, :] = jnp.where(low, 0.0, qg)
        m_scr[...] = jnp.full(m_scr.shape, NEG, F32)
        l_scr[...] = jnp.zeros(l_scr.shape, F32)
        acc_scr[...] = jnp.zeros(acc_scr.shape, F32)

    n_slot = 2 * A_KV
    is_last = c == last

    def scores(kv):
        qm = qm_scr[kv].astype(BF16)
        keys = jnp.concatenate([pr[pl.ds(kv, PAGE_SIZE, stride=n_slot), :] for pr in page_refs], axis=0)
        return _dot_nt(qm, keys.astype(BF16)) + jnp.where(is_last, bp_ref[kv], bc_ref[kv])

    def values(kv):
        return jnp.concatenate([pr[pl.ds(A_KV + kv, PAGE_SIZE, stride=n_slot), :] for pr in page_refs], axis=0).astype(BF16)

    s = jnp.concatenate([scores(kv) for kv in range(A_KV)], axis=0)
    m_prev = m_scr[...]
    m_new = jnp.maximum(m_prev, jnp.max(s, axis=-1, keepdims=True))
    alpha = jnp.exp(m_prev - m_new)
    p = jnp.exp(s - m_new)
    l_scr[...] = alpha * l_scr[...] + jnp.sum(p, axis=-1, keepdims=True)
    m_scr[...] = m_new
    pb = p.astype(BF16)
    pv = jnp.concatenate([_dot(pb[kv * rows:(kv + 1) * rows], values(kv)) for kv in range(A_KV)], axis=0)
    acc_scr[...] = alpha * acc_scr[...] + pv

    @pl.when(is_last)
    def _():
        pad = jnp.zeros((LANES - tq, LANES), F32)
        for kv in range(A_KV):
            sl = slice(kv * rows, (kv + 1) * rows)
            qm = qm_scr[kv].astype(BF16)
            kn = jnp.concatenate([new_ref[:, kv * LANES:(kv + 1) * LANES], pad], axis=0)
            vn = jnp.concatenate([new_ref[:, (A_KV + kv) * LANES:(A_KV + kv + 1) * LANES], pad], axis=0)
            s = _dot_nt(qm, kn.astype(BF16)) + bn_ref[kv]
            m_prev = m_scr[sl]
            m_new = jnp.maximum(m_prev, jnp.max(s, axis=-1, keepdims=True))
            alpha = jnp.exp(m_prev - m_new)
            p = jnp.exp(s - m_new)
            l = alpha * l_scr[sl] + jnp.sum(p, axis=-1, keepdims=True)
            acc = alpha * acc_scr[sl] + _dot(p.astype(BF16), vn.astype(BF16))
            for g in range(4):
                r0, r1 = (2 * g) * tq, (2 * g + 1) * tq
                o = _diff_finalize(acc[r0:r0 + tq], l[r0:r0 + tq], acc[r1:r1 + tq], l[r1:r1 + tq],
                                   lam_ref, subln_ref, lam_init)
                o_ref[:, (kv * 4 + g) * LANES:(kv * 4 + g + 1) * LANES] = o.astype(o_ref.dtype)


def _a_sample(q, new_kv, cache, layer, page_table, bias_past, bias_const, bias_new, lam_p, subln, lam_init,
              pages=64):
    db, tq, _ = q.shape
    n_pages = page_table.shape[1]
    n_chunks = n_pages // pages
    assert n_chunks == 1 or pages * PAGE_SIZE + 1 >= THR[-1]
    cache2 = cache.reshape(cache.shape[0], cache.shape[1], PAGE_SIZE * 2 * A_KV, LANES)
    rows = 8 * tq

    def page_spec(j):
        return pl.BlockSpec((None, None, PAGE_SIZE * 2 * A_KV, LANES),
                            lambda bi, c, pt: (layer, pt[bi * n_pages + c * pages + j], 0, 0))

    grid_spec = pltpu.PrefetchScalarGridSpec(
        num_scalar_prefetch=1,
        grid=(db, n_chunks),
        in_specs=[pl.BlockSpec((None, tq, 1024), lambda bi, c, pt: (bi, 0, 0)),
                  pl.BlockSpec((None, tq, 512), lambda bi, c, pt: (bi, 0, 0)),
                  pl.BlockSpec((A_KV, rows, pages * PAGE_SIZE), lambda bi, c, pt: (0, 0, n_chunks - 1)),
                  pl.BlockSpec((A_KV, rows, 1), lambda bi, c, pt: (0, 0, 0)),
                  pl.BlockSpec((A_KV, rows, LANES), lambda bi, c, pt: (0, 0, 0)),
                  pl.BlockSpec((4, HEAD_DIM), lambda bi, c, pt: (0, 0)),
                  pl.BlockSpec((1, LANES), lambda bi, c, pt: (0, 0))]
                 + [page_spec(j) for j in range(pages)],
        out_specs=pl.BlockSpec((None, tq, 1024), lambda bi, c, pt: (bi, 0, 0)),
        scratch_shapes=[pltpu.VMEM((A_KV, rows, LANES), F32),
                        pltpu.VMEM((A_KV * rows, 1), F32),
                        pltpu.VMEM((A_KV * rows, 1), F32),
                        pltpu.VMEM((A_KV * rows, LANES), F32)])
    return pl.pallas_call(
        functools.partial(_a_sample_kernel, pages=pages, tq=tq, lam_init=lam_init),
        grid_spec=grid_spec,
        out_shape=jax.ShapeDtypeStruct((db, tq, 1024), F32),
        compiler_params=_cparams(("parallel", "arbitrary")),
        name="a_sample",
    )(page_table.reshape(-1), q, new_kv, bias_past, bias_const, bias_new, lam_p, subln.reshape(1, LANES),
      *([cache2] * pages))


def _band_kernel(*refs, n_kv, use_sink, want_lse, n_sub):
    q_ref, kp_ref, kc_ref, vp_ref, vc_ref, bias_ref = refs[:6]
    pos = 6
    sink_ref = None
    if use_sink:
        sink_ref = refs[pos]
        pos += 1
    o_ref = refs[pos]
    lse_ref = refs[pos + 1] if want_lse else None
    qm_scr = refs[-1]
    i = pl.program_id(2)
    grp = N_HEADS // n_kv
    low = _low_half()
    key = lax.broadcasted_iota(jnp.int32, (2 * BAND, 1), 0)
    ones = jnp.ones((ONES_ROWS, 2 * BAND), BF16)
    for j in range(n_sub):
        rows = slice(j * BAND, (j + 1) * BAND)
        prv = slice((j - 1) * BAND, j * BAND)

        def kv_pair(pref, cref, kv):
            lanes = slice(kv * LANES, (kv + 1) * LANES)
            before = pref[:, lanes] if j == 0 else cref[prv, lanes]
            return jnp.concatenate([before, cref[rows, lanes]], axis=0)

        for c in range(N_HEADS // 2):
            qp = q_ref[rows, c * LANES:(c + 1) * LANES]
            zero = jnp.zeros_like(qp)
            qm_scr[(2 * c) * BAND:(2 * c + 1) * BAND, :] = jnp.where(low, qp, zero)
            qm_scr[(2 * c + 1) * BAND:(2 * c + 2) * BAND, :] = jnp.where(low, zero, qp)
        s_parts = [_dot_nt(kv_pair(kp_ref, kc_ref, kv), qm_scr[kv * grp * BAND:(kv + 1) * grp * BAND, :])
                   for kv in range(n_kv)]
        s = jnp.concatenate(s_parts, axis=1) + bias_ref[...]
        if j == 0:
            s = jnp.where((key >= BAND) | (i > 0), s, NEG)
        m = jnp.max(s, axis=0, keepdims=True)
        if use_sink:
            m = jnp.maximum(m, sink_ref[...])
        p = jnp.exp(s - m).astype(BF16)
        acc_parts = []
        for kv in range(n_kv):
            vt = kv_pair(vp_ref, vc_ref, kv).astype(F32).T[0:HEAD_DIM, :].astype(BF16)
            acc_parts.append(_dot(jnp.concatenate([vt, ones], axis=0), p[:, kv * grp * BAND:(kv + 1) * grp * BAND]))
        acc = jnp.concatenate(acc_parts, axis=1)
        l = acc[HEAD_DIM:HEAD_DIM + 1, :]
        if use_sink:
            l = l + jnp.exp(sink_ref[...] - m)
        on = acc[0:HEAD_DIM, :] / l
        for c in range(N_HEADS // 2):
            pair = jnp.concatenate([on[:, (2 * c) * BAND:(2 * c + 1) * BAND], on[:, (2 * c + 1) * BAND:(2 * c + 2) * BAND]], axis=0)
            o_ref[rows, c * LANES:(c + 1) * LANES] = pair.T.astype(o_ref.dtype)
        if want_lse:
            lse = m + jnp.log(l)
            lrows = jnp.concatenate([lse[:, h * BAND:(h + 1) * BAND] for h in range(N_HEADS)]
                                    + [jnp.zeros((LANES - N_HEADS, BAND), F32)], axis=0)
            lse_ref[rows, :] = lrows.T


def _band_prompt(q, kd, vd, bias, sinks, n_kv, want_lse, out_dtype):
    b, dil, ts, _ = q.shape
    n_sub = 2 if ts % (2 * BAND) == 0 else 1
    blk = n_sub * BAND
    nb = ts // blk
    ck = n_kv * LANES
    cur = lambda bi, r, i: (bi, r, i, 0)
    prev = lambda bi, r, i: (bi, r, jnp.maximum(i * n_sub - 1, 0), 0)
    in_specs = [pl.BlockSpec((None, None, blk, 1024), cur),
                pl.BlockSpec((None, None, BAND, ck), prev),
                pl.BlockSpec((None, None, blk, ck), cur),
                pl.BlockSpec((None, None, BAND, ck), prev),
                pl.BlockSpec((None, None, blk, ck), cur),
                pl.BlockSpec((2 * BAND, N_HEADS * BAND), lambda bi, r, i: (0, 0))]
    args = [q, kd, kd, vd, vd, bias]
    if sinks is not None:
        in_specs.append(pl.BlockSpec((1, N_HEADS * BAND), lambda bi, r, i: (0, 0)))
        args.append(jnp.repeat(sinks, BAND).reshape(1, N_HEADS * BAND))
    out_specs = [pl.BlockSpec((None, None, blk, 1024), cur)]
    out_shape = [jax.ShapeDtypeStruct((b, dil, ts, 1024), out_dtype)]
    if want_lse:
        out_specs.append(pl.BlockSpec((None, None, blk, LANES), cur))
        out_shape.append(jax.ShapeDtypeStruct((b, dil, ts, LANES), F32))
    return pl.pallas_call(
        functools.partial(_band_kernel, n_kv=n_kv, use_sink=sinks is not None, want_lse=want_lse, n_sub=n_sub),
        grid=(b, dil, nb),
        in_specs=in_specs,
        out_specs=out_specs,
        out_shape=out_shape,
        scratch_shapes=[pltpu.VMEM((N_HEADS * BAND, LANES), BF16)],
        compiler_params=_cparams(("parallel", "parallel", "arbitrary")),
        name=f"band_d{dil}",
    )(*args)


def _decode_kernel(*refs, n_grp, n_kv, past_lens, use_sink, tq):
    q_ref, new_ref = refs[:2]
    pos = 2
    st_refs = refs[pos:pos + n_grp]; pos += n_grp
    bp_refs = refs[pos:pos + n_grp]; pos += n_grp
    bn_refs = refs[pos:pos + n_grp]; pos += n_grp
    sink_ref = None
    if use_sink:
        sink_ref = refs[pos]; pos += 1
    o_ref = refs[pos]; pos += 1
    nst_refs = refs[pos:pos + n_grp]
    width = 2 * n_kv * HEAD_DIM
    n_chunks = n_kv // 2
    heads_per_chunk = N_HEADS // n_chunks
    grp_heads = N_HEADS // n_kv
    low = _low_half()
    pad = jnp.zeros((LANES - tq, LANES), F32)
    for c in range(n_chunks):
        parts = []
        for g in range(n_grp):
            qs = jnp.concatenate(
                [q_ref[:, (g * N_HEADS + c * heads_per_chunk + hh) * LANES:(g * N_HEADS + c * heads_per_chunk + hh + 1) * LANES]
                 for hh in range(heads_per_chunk)], axis=0).astype(BF16)
            st = st_refs[g]
            kt = st[0, c * LANES:(c + 1) * LANES, :].astype(BF16)
            vt = st[1, c * LANES:(c + 1) * LANES, :].astype(BF16)
            kn = jnp.concatenate([new_ref[:, g * width + c * LANES:g * width + (c + 1) * LANES], pad], axis=0).astype(BF16)
            vn = jnp.concatenate([new_ref[:, g * width + n_kv * HEAD_DIM + c * LANES:g * width + n_kv * HEAD_DIM + (c + 1) * LANES], pad], axis=0).astype(BF16)
            sp = _dot(qs, kt) + bp_refs[g][c]
            sn = _dot_nt(qs, kn) + bn_refs[g][c]
            m = jnp.maximum(jnp.max(sp, axis=-1, keepdims=True), jnp.max(sn, axis=-1, keepdims=True))
            if use_sink:
                m = jnp.maximum(m, sink_ref[c])
            pp = jnp.exp(sp - m)
            pn = jnp.exp(sn - m)
            l = jnp.sum(pp, axis=-1, keepdims=True) + jnp.sum(pn, axis=-1, keepdims=True)
            if use_sink:
                l = l + jnp.exp(sink_ref[c] - m)
            acc = _dot_nt(pp.astype(BF16), vt) + _dot(pn.astype(BF16), vn)
            parts.append((m, l, acc))
        m_all = parts[0][0]
        for m, _, _ in parts[1:]:
            m_all = jnp.maximum(m_all, m)
        num = 0.0
        den = 0.0
        for m, l, acc in parts:
            w = jnp.exp(m - m_all)
            num = num + acc * w
            den = den + l * w
        o = num / den
        for hh in range(heads_per_chunk):
            h = c * heads_per_chunk + hh
            odd = (h // grp_heads) % 2
            oh = o[hh * tq:(hh + 1) * tq, :]
            o_ref[:, h * LANES:(h + 1) * LANES] = (jnp.where(low, 0.0, oh) if odd else jnp.where(low, oh, 0.0)).astype(o_ref.dtype)
    n_rows = n_kv * HEAD_DIM
    for g in range(n_grp):
        lp = past_lens[g]
        for kv_slot in range(2):
            c0 = g * width + kv_slot * n_rows
            new_t = jnp.concatenate([new_ref[:, c0:c0 + n_rows], jnp.zeros((LANES - tq, n_rows), F32)], axis=0).T
            for r0 in range(0, n_rows, HEAD_DIM):
                if lp > tq:
                    nst_refs[g][kv_slot, r0:r0 + HEAD_DIM, 0:lp - tq] = st_refs[g][kv_slot, r0:r0 + HEAD_DIM, tq:lp]
                nst_refs[g][kv_slot, r0:r0 + HEAD_DIM, lp - tq:lp] = new_t[r0:r0 + HEAD_DIM, 0:tq]


def _decode(q, new_kv, states, layer, bias_past, bias_new, sinks_rows, n_kv):
    db, tq, _ = q.shape
    n_grp = len(states)
    width = 2 * n_kv * HEAD_DIM
    n_chunks = n_kv // 2
    rows = tq * N_HEADS // n_chunks
    past_lens = tuple(s.shape[2] for s in states)
    st_t = [jnp.transpose(s, (0, 1, 3, 4, 5, 2)).reshape(s.shape[0], db, 2, n_kv * HEAD_DIM, s.shape[2])
            for s in states]
    in_specs = [pl.BlockSpec((None, tq, q.shape[2]), lambda bi: (bi, 0, 0)),
                pl.BlockSpec((None, tq, n_grp * width), lambda bi: (bi, 0, 0))]
    in_specs += [pl.BlockSpec((None, None, 2, n_kv * HEAD_DIM, lp), lambda bi: (layer, bi, 0, 0, 0))
                 for lp in past_lens]
    in_specs += [pl.BlockSpec((n_chunks, rows, lp), lambda bi: (0, 0, 0)) for lp in past_lens]
    in_specs += [pl.BlockSpec((n_chunks, rows, LANES), lambda bi: (0, 0, 0)) for _ in past_lens]
    args = [q, new_kv] + st_t + list(bias_past) + list(bias_new)
    if sinks_rows is not None:
        in_specs.append(pl.BlockSpec((n_chunks, rows, 1), lambda bi: (0, 0, 0)))
        args.append(sinks_rows)
    res = pl.pallas_call(
        functools.partial(_decode_kernel, n_grp=n_grp, n_kv=n_kv, past_lens=past_lens,
                          use_sink=sinks_rows is not None, tq=tq),
        grid=(db,),
        in_specs=in_specs,
        out_specs=[pl.BlockSpec((None, tq, N_HEADS * LANES), lambda bi: (bi, 0, 0))]
        + [pl.BlockSpec((None, 2, n_kv * HEAD_DIM, lp), lambda bi: (bi, 0, 0, 0)) for lp in past_lens],
        out_shape=[jax.ShapeDtypeStruct((db, tq, N_HEADS * LANES), F32)]
        + [jax.ShapeDtypeStruct((db, 2, n_kv * HEAD_DIM, lp), F32) for lp in past_lens],
        compiler_params=_cparams(("parallel",)),
        name=f"decode_kv{n_kv}",
    )(*args)
    new_states = [jnp.transpose(r.reshape(db, 2, n_kv, HEAD_DIM, lp), (0, 4, 1, 2, 3))
                  for r, lp in zip(res[1:], past_lens)]
    return res[0], new_states


def _ffn_kernel(*refs, n_attn, sample, tiles_per_seq, row_tile, ff_chunk, tq, dils):
    x_ref = refs[0]
    pos = 1
    if n_attn == 1:
        o_in = refs[pos]; pos += 1
    else:
        o_refs = refs[pos:pos + n_attn]; pos += n_attn
        lse_refs = refs[pos:pos + n_attn]; pos += n_attn
    wo_ref, g_ref, win_ref, cw_ref, cb_ref, wout_ref = refs[pos:pos + 6]; pos += 6
    if sample:
        fix1_ref, fix2_ref = refs[pos:pos + 2]; pos += 2
    xo_ref, gate_ref = refs[pos:pos + 2]; pos += 2
    gp_scr, halo_scr = refs[pos:pos + 2]; pos += 2
    il_scr = refs[pos:]
    i = pl.program_id(0)
    low = _low_half()

    @pl.when(i == 0)
    def _():
        halo_scr[...] = jnp.zeros(halo_scr.shape, F32)

    if n_attn == 1:
        ob = o_in[...]
    else:
        def natural(ref4, g, col0):
            d = dils[g]
            if d == 1:
                return ref4[0, :, col0:col0 + LANES]
            scr = il_scr[sum(1 for x in dils[:g] if x > 1)]
            for r in range(d):
                scr[pl.ds(r, row_tile // d, stride=d), :] = ref4[r, :, col0:col0 + LANES]
            return scr[...]

        lses = [natural(lse_refs[g], g, 0) for g in range(n_attn)]
        m_all = lses[0]
        for z in lses[1:]:
            m_all = jnp.maximum(m_all, z)
        es = [jnp.exp(z - m_all) for z in lses]
        den = es[0]
        for e in es[1:]:
            den = den + e
        ws = [e / den for e in es]
        chunks = []
        for c in range(1024 // LANES):
            acc = None
            for g in range(n_attn):
                wsel = jnp.where(low, ws[g][:, 2 * c:2 * c + 1], ws[g][:, 2 * c + 1:2 * c + 2])
                term = wsel * natural(o_refs[g], g, c * LANES)
                acc = term if acc is None else acc + term
            chunks.append(acc.astype(BF16))
        ob = jnp.concatenate(chunks, axis=1)
    x1 = x_ref[...] + _dot(ob, wo_ref[...])
    h = x1 * lax.rsqrt(jnp.mean(x1 * x1, axis=-1, keepdims=True) + RMS_EPS) * g_ref[...]
    hb = h.astype(BF16)

    if sample:
        t_in_seq = lax.broadcasted_iota(jnp.int32, (row_tile, 1), 0) & (tq - 1)
    y = jnp.zeros((row_tile, D_MODEL), F32)
    for f in range(D_FF // ff_chunk):
        f0 = f * ff_chunk
        gate = _dot(hb, win_ref[:, f0:f0 + ff_chunk])
        up = _dot(hb, win_ref[:, D_FF + f0:D_FF + f0 + ff_chunk])
        if sample:
            gp_scr[0:SUBLANES, :] = jnp.zeros((SUBLANES, ff_chunk), F32)
        else:
            first = (i % tiles_per_seq) == 0
            gp_scr[0:SUBLANES, :] = jnp.where(first, 0.0, halo_scr[f])
        gp_scr[SUBLANES:SUBLANES + row_tile, :] = gate
        g2 = gp_scr[SUBLANES - 2:SUBLANES - 2 + row_tile, :]
        g1 = gp_scr[SUBLANES - 1:SUBLANES - 1 + row_tile, :]
        if sample:
            g2 = jnp.where(t_in_seq >= 2, g2, fix2_ref[:, f0:f0 + ff_chunk])
            g1 = jnp.where(t_in_seq >= 1, g1, fix1_ref[:, f0:f0 + ff_chunk])
        gc = (cb_ref[:, f0:f0 + ff_chunk] + cw_ref[0:1, f0:f0 + ff_chunk] * g2
              + cw_ref[1:2, f0:f0 + ff_chunk] * g1 + cw_ref[2:3, f0:f0 + ff_chunk] * gate)
        act = gc * (1.0 / (1.0 + jnp.exp(-gc))) * up
        y = y + _dot(act.astype(BF16), wout_ref[f0:f0 + ff_chunk, :])
        if sample:
            gate_ref[:, f0:f0 + ff_chunk] = gate
        else:
            tail = gate[row_tile - SUBLANES:row_tile, :]
            halo_scr[f] = tail
            gate_ref[:, f0:f0 + ff_chunk] = tail
    xo_ref[...] = x1 + y


def _ffn(x, attn, lses, w_o, gain, w_in, conv_w, conv_b, w_out, fix, seq_len, row_tile=512,
         ff_chunk=1408):
    rows = x.shape[0]
    sample = fix is not None
    n_attn = len(attn)
    row_spec = lambda w: pl.BlockSpec((row_tile, w), lambda i: (i, 0))
    full = lambda a: pl.BlockSpec(a.shape, lambda i: (0,) * a.ndim, pipeline_mode=pl.Buffered(1))
    tiles = seq_len // row_tile
    res_spec = lambda a: pl.BlockSpec((None, a.shape[1], row_tile // a.shape[1], a.shape[3]),
                                      lambda i: (i // tiles, 0, i % tiles, 0))
    dils = (1,)
    if n_attn == 1:
        in_specs = [row_spec(D_MODEL), row_spec(attn[0].shape[1])]
        args = [x] + list(attn)
    else:
        dils = tuple(a.shape[1] for a in attn)
        in_specs = [row_spec(D_MODEL)] + [res_spec(a) for a in attn] + [res_spec(a) for a in lses]
        args = [x] + list(attn) + list(lses)
    consts = [w_o.astype(BF16), gain.reshape(1, D_MODEL), w_in.astype(BF16), conv_w,
              conv_b.reshape(1, D_FF), w_out.astype(BF16)]
    in_specs += [full(a) for a in consts]
    args += consts
    if sample:
        in_specs += [row_spec(D_FF), row_spec(D_FF)]
        args += list(fix)
        gate_spec = row_spec(D_FF)
        gate_shape = jax.ShapeDtypeStruct((rows, D_FF), F32)
        tiles_per_seq = 1
        tq = seq_len
    else:
        gate_spec = pl.BlockSpec((None, SUBLANES, D_FF), lambda i: (i, 0, 0))
        gate_shape = jax.ShapeDtypeStruct((rows // row_tile, SUBLANES, D_FF), F32)
        tiles_per_seq = seq_len // row_tile
        tq = 1
    return pl.pallas_call(
        functools.partial(_ffn_kernel, n_attn=n_attn, sample=sample, tiles_per_seq=tiles_per_seq,
                          row_tile=row_tile, ff_chunk=ff_chunk, tq=tq, dils=dils),
        grid=(rows // row_tile,),
        in_specs=in_specs,
        out_specs=[row_spec(D_MODEL), gate_spec],
        out_shape=[jax.ShapeDtypeStruct((rows, D_MODEL), F32), gate_shape],
        scratch_shapes=[pltpu.VMEM((SUBLANES + row_tile, ff_chunk), F32),
                        pltpu.VMEM((D_FF // ff_chunk, SUBLANES, ff_chunk), F32)]
        + [pltpu.VMEM((row_tile, LANES), F32) for d in dils if d > 1],
        compiler_params=_cparams(("arbitrary",)),
        name="ffn_sample" if sample else "ffn_prompt",
    )(*args)


def _expand_wo_sample(w_o, n_kv):
    grp_heads = N_HEADS // n_kv
    src = []
    for h in range(N_HEADS):
        rows = _cols(h * HEAD_DIM, HEAD_DIM)
        src += (_zeros(HEAD_DIM) + rows) if (h // grp_heads) % 2 else (rows + _zeros(HEAD_DIM))
    return _take_cols(w_o.T, src).T


def kernel(x_prompt, x_sample, cache_kv_a, state_kv_b, state_kv_c1, state_kv_c2, state_kv_c3, state_conv_ffn, page_table, rel_bias_table, norm_mix, norm_ffn, w_in_a, q_norm_a, k_norm_a, lambda_a, subln_a, w_o_a, w_in_b, q_norm_b, k_norm_b, sinks_b, w_o_b, w_in_c, q_norm_c, k_norm_c, w_o_c, w_ffn_in, conv_ffn_w, conv_ffn_b, w_ffn_out):
    bp, tp, _ = x_prompt.shape
    bs, ts, _ = x_sample.shape
    depth = norm_mix.shape[0]
    past_len = page_table.shape[1] * PAGE_SIZE
    xp = x_prompt.reshape(bp * tp, D_MODEL)
    xs = x_sample.reshape(bs * ts, D_MODEL)
    state_c = (state_kv_c1, state_kv_c2, state_kv_c3)

    toep = _toep_table(rel_bias_table)
    band_bias = {d: _band_bias_table(rel_bias_table, d) for d in (1, 4, 16)}
    a_cols = [[kv * 8 + j for j in range(8)] for kv in range(A_KV)]
    a_bias_past, a_bias_new = _decode_bias_table(rel_bias_table, a_cols, past_len, past_len + ts, 1, ts)
    a_bias_far = jnp.repeat(rel_bias_table[N_BUCKETS - 1][jnp.asarray(a_cols)], ts, axis=1)[..., None]
    b_cols = [list(range(N_HEADS))]
    b_len = state_kv_b.shape[2]
    b_bias_past, b_bias_new = _decode_bias_table(rel_bias_table, b_cols, b_len, B_WINDOW, 1, ts)
    c_cols = [list(range(8)), list(range(8, 16))]
    c_bias = [_decode_bias_table(rel_bias_table, c_cols, state_c[g].shape[2], w, d, ts)
              for g, (w, d) in enumerate(C_PAIRS)]

    kv_a_p, kv_a_s, kv_b_p, kv_b_s = [], [], [], []
    kv_c_p = [[] for _ in C_PAIRS]
    kv_c_s = [[] for _ in C_PAIRS]
    conv_p, conv_s = [], []
    for i in range(depth):
        n = i // N_MIXERS
        if i % N_MIXERS == 0:
            lam_init = 0.8 - 0.6 * math.exp(-0.3 * i)
            q, kvn, k, vt = _run_proj(xp, norm_mix[i], w_in_a[n], q_norm_a[n], k_norm_a[n], _layout_a(True), 512,
                                      q_scale=LOG2E, t_cols=_cols(1280, 256), seq_len=tp)
            op = _a_prompt(q.reshape(bp, tp, 1024), k.reshape(bp, tp, 256), vt, toep, lambda_a[n], subln_a[n],
                           lam_init)
            attn_p, lse_p = [op.reshape(bp * tp, 1024)], None
            kv_a_p.append(kvn.reshape(bp, tp, 2, A_KV, 2 * HEAD_DIM))
            qs, kvs = _run_proj(xs, norm_mix[i], w_in_a[n], q_norm_a[n], k_norm_a[n], _layout_a(False), 512)
            os_ = _a_sample(qs.reshape(bs, ts, 1024), kvs.reshape(bs, ts, 512), cache_kv_a, n, page_table,
                            a_bias_past, a_bias_far, a_bias_new, lambda_a[n], subln_a[n], lam_init)
            attn_s = [os_.reshape(bs * ts, 1024).astype(BF16)]
            kv_a_s.append(kvs.reshape(bs, ts, 2, A_KV, 2 * HEAD_DIM))
            w_o_p = w_o_a[n]
            w_o_s = w_o_a[n]
        elif i % N_MIXERS == 1:
            kvn, q, kd, vd = _run_proj(xp, norm_mix[i], w_in_b[n], q_norm_b[n], k_norm_b[n],
                                       _layout_band_prompt((1,), B_KV), 512)
            op = _band_prompt(q.reshape(bp, 1, tp, 1024), kd.reshape(bp, 1, tp, B_KV * LANES),
                              vd.reshape(bp, 1, tp, B_KV * LANES), band_bias[1], sinks_b[n], B_KV, False, BF16)[0]
            attn_p, lse_p = [op.reshape(bp * tp, 1024)], None
            wb = min(B_WINDOW, tp)
            kv_b_p.append(kvn.reshape(bp, tp, -1)[:, tp - wb:].reshape(bp, wb, 2, B_KV, HEAD_DIM))
            qs, kvs = _run_proj(xs, norm_mix[i], w_in_b[n], q_norm_b[n], k_norm_b[n],
                                _layout_band_sample(1, B_KV), 512)
            sink_rows = jnp.repeat(sinks_b[n], ts).reshape(1, N_HEADS * ts, 1)
            os_, nst = _decode(qs.reshape(bs, ts, -1), kvs.reshape(bs, ts, -1), [state_kv_b], n,
                               [b_bias_past], [b_bias_new], sink_rows, B_KV)
            attn_s = [os_.reshape(bs * ts, N_HEADS * LANES).astype(BF16)]
            kv_b_s.append(nst[0])
            w_o_p = w_o_b[n]
            w_o_s = _expand_wo_sample(w_o_b[n], B_KV)
        else:
            c_dils = tuple(d for _, d in C_PAIRS)
            outs = _run_proj(xp, norm_mix[i], w_in_c[n], q_norm_c[n], k_norm_c[n],
                             _layout_band_prompt(c_dils, C_KV), 256, seq_len=tp)
            kvn = outs[0]
            attn_p, lse_p = [], []
            for g, (w, d) in enumerate(C_PAIRS):
                q, kd, vd = (a.reshape(bp, d, tp // d, a.shape[-1]) for a in outs[1 + 3 * g:4 + 3 * g])
                o_g, lse_g = _band_prompt(q, kd, vd, band_bias[d], None, C_KV, True, F32)
                attn_p.append(o_g)
                lse_p.append(lse_g)
            kvn = kvn.reshape(bp, tp, -1)
            gw = 2 * C_KV * HEAD_DIM
            for g, (w, d) in enumerate(C_PAIRS):
                wc = min(w, tp)
                kv_c_p[g].append(kvn[:, tp - wc:, g * gw:(g + 1) * gw].reshape(bp, wc, 2, C_KV, HEAD_DIM))
            qs, kvs = _run_proj(xs, norm_mix[i], w_in_c[n], q_norm_c[n], k_norm_c[n],
                                _layout_band_sample(3, C_KV), 256)
            os_, nst = _decode(qs.reshape(bs, ts, -1), kvs.reshape(bs, ts, -1), list(state_c), n,
                               [cb[0] for cb in c_bias], [cb[1] for cb in c_bias], None, C_KV)
            attn_s = [os_.reshape(bs * ts, N_HEADS * LANES).astype(BF16)]
            for g in range(3):
                kv_c_s[g].append(nst[g])
            w_o_p = w_o_c[n]
            w_o_s = _expand_wo_sample(w_o_c[n], C_KV)

        xp, gate_tail = _ffn(xp, attn_p, lse_p, w_o_p, norm_ffn[i], w_ffn_in[i], conv_ffn_w[i],
                             conv_ffn_b[i], w_ffn_out[i], None, tp)
        tiles = tp // 512
        conv_p.append(gate_tail.reshape(bp, tiles, SUBLANES, D_FF)[:, -1, SUBLANES - (CONV_W - 1):])
        hist = state_conv_ffn[i]
        zero = jnp.zeros((bs, ts - 1, D_FF), F32)
        fix1 = jnp.concatenate([hist[:, 1:2], zero], axis=1).reshape(bs * ts, D_FF)
        fix2 = jnp.concatenate([hist[:, 0:1], hist[:, 1:2], zero[:, 1:]], axis=1).reshape(bs * ts, D_FF)
        xs, gate_s = _ffn(xs, attn_s, None, w_o_s, norm_ffn[i], w_ffn_in[i], conv_ffn_w[i],
                          conv_ffn_b[i], w_ffn_out[i], (fix1, fix2), ts, row_tile=256)
        conv_s.append(gate_s.reshape(bs, ts, D_FF)[:, ts - (CONV_W - 1):])

    return (xp.reshape(bp, tp, D_MODEL), xs.reshape(bs, ts, D_MODEL),
            jnp.stack(kv_a_p), jnp.stack(kv_a_s),
            jnp.stack(kv_b_p), jnp.stack(kv_b_s),
            jnp.stack(kv_c_p[0]), jnp.stack(kv_c_s[0]),
            jnp.stack(kv_c_p[1]), jnp.stack(kv_c_s[1]),
            jnp.stack(kv_c_p[2]), jnp.stack(kv_c_s[2]),
            jnp.stack(conv_p), jnp.stack(conv_s))
```

```python
import functools
import math

import numpy as np
import jax
import jax.numpy as jnp
from jax import lax
from jax.experimental import pallas as pl
from jax.experimental.pallas import tpu as pltpu

F32 = jnp.float32
BF16 = jnp.bfloat16

D_MODEL = 1024
HEAD_DIM = 64
LANES = 128
SUBLANES = 8
RMS_EPS = 1e-6
NEG = -1e30
LOG2E = math.log2(math.e)
N_HEADS = 16
N_BUCKETS = 32
MAX_DISTANCE = 2048
PAGE_SIZE = 128
D_FF = 2816
CONV_W = 3
N_MIXERS = 3
A_KV = 2
B_KV = 2
C_KV = 4
B_WINDOW = 128
C_PAIRS = ((128, 1), (512, 4), (2048, 16))
BAND = 128
ONES_ROWS = 16
VMEM_LIMIT = 56 * 1024 * 1024


def _bucket_thresholds():
    n = np.arange(0, 1 << 15)
    x = np.log(np.maximum(n, 1) / (N_BUCKETS // 2)) / math.log(MAX_DISTANCE / (N_BUCKETS // 2))
    large = N_BUCKETS // 2 + (x * (N_BUCKETS - N_BUCKETS // 2)).astype(np.int64)
    bucket = np.where(n < N_BUCKETS // 2, n, np.minimum(large, N_BUCKETS - 1))
    return [int(np.argmax(bucket >= b)) for b in range(N_BUCKETS)]


THR = _bucket_thresholds()
N_TOEP = -(-(THR[-1] + BAND - 1) // BAND)
TOEP_TILES = N_TOEP + 2


def _cparams(sem, vmem=VMEM_LIMIT):
    return pltpu.CompilerParams(dimension_semantics=sem, vmem_limit_bytes=vmem)


def _dot(a, b):
    return jnp.dot(a, b, preferred_element_type=F32)


def _dot_nt(a, b):
    return lax.dot_general(a, b, (((1,), (1,)), ((), ())), preferred_element_type=F32)


def _low_half():
    return lax.broadcasted_iota(jnp.int32, (1, LANES), 1) < HEAD_DIM


def _bias_of(dist, col, tab_ref, lo, hi):
    b_lo = max(b for b in range(N_BUCKETS) if THR[b] <= max(lo, 0))
    b_hi = max(b for b in range(N_BUCKETS) if THR[b] <= max(hi, 0))
    val = jnp.full(dist.shape, tab_ref[b_hi, col], F32)
    for b in range(b_hi - 1, b_lo - 1, -1):
        val = jnp.where(dist < THR[b + 1], tab_ref[b, col], val)
    return val


def _toep_kernel(tab_ref, o_ref):
    t = pl.program_id(0)
    key = lax.broadcasted_iota(jnp.int32, (BAND, BAND), 0)
    qry = lax.broadcasted_iota(jnp.int32, (BAND, BAND), 1)
    for delta in range(-1, N_TOEP + 1):
        @pl.when(t == delta + 1)
        def _(delta=delta):
            for c in range(N_HEADS):
                if delta < 0:
                    o_ref[c] = jnp.full((BAND, BAND), NEG, F32)
                elif delta == N_TOEP:
                    o_ref[c] = jnp.full((BAND, BAND), tab_ref[N_BUCKETS - 1, c] * LOG2E, F32)
                else:
                    dist = delta * BAND + qry - key
                    val = _bias_of(dist, c, tab_ref, delta * BAND - BAND + 1, delta * BAND + BAND - 1)
                    o_ref[c] = jnp.where(dist >= 0, val * LOG2E, NEG)


def _toep_table(table):
    return pl.pallas_call(
        _toep_kernel,
        grid=(TOEP_TILES,),
        in_specs=[pl.BlockSpec(memory_space=pltpu.SMEM)],
        out_specs=pl.BlockSpec((None, N_HEADS, BAND, BAND), lambda t: (t, 0, 0, 0)),
        out_shape=jax.ShapeDtypeStruct((TOEP_TILES, N_HEADS, BAND, BAND), F32),
        compiler_params=_cparams(("arbitrary",)),
        name="bias_toeplitz",
    )(table)


def _band_bias_kernel(tab_ref, o_ref, *, dil):
    key = lax.broadcasted_iota(jnp.int32, (2 * BAND, BAND), 0)
    qry = lax.broadcasted_iota(jnp.int32, (2 * BAND, BAND), 1)
    steps = BAND + qry - key
    valid = (steps >= 0) & (steps <= BAND)
    for c in range(N_HEADS):
        val = _bias_of(steps * dil, c, tab_ref, 0, BAND * dil)
        o_ref[:, c * BAND:(c + 1) * BAND] = jnp.where(valid, val, NEG)


def _band_bias_table(table, dil):
    return pl.pallas_call(
        functools.partial(_band_bias_kernel, dil=dil),
        in_specs=[pl.BlockSpec(memory_space=pltpu.SMEM)],
        out_shape=jax.ShapeDtypeStruct((2 * BAND, N_HEADS * BAND), F32),
        compiler_params=_cparams(None),
        name=f"bias_band_d{dil}",
    )(table)


def _decode_bias_kernel(tab_ref, past_ref, new_ref, *, cols, past_len, window, dil, tq):
    n_grp = len(cols)
    t_p = lax.broadcasted_iota(jnp.int32, (tq, past_len), 0)
    i_p = lax.broadcasted_iota(jnp.int32, (tq, past_len), 1)
    d_p = past_len + t_p - i_p
    ok_p = d_p <= window
    if dil > 1:
        ok_p = ok_p & ((d_p & (dil - 1)) == 0)
    t_n = lax.broadcasted_iota(jnp.int32, (tq, LANES), 0)
    i_n = lax.broadcasted_iota(jnp.int32, (tq, LANES), 1)
    d_n = t_n - i_n
    ok_n = (d_n >= 0) & (i_n < tq)
    if dil > 1:
        ok_n = ok_n & ((d_n & (dil - 1)) == 0)
    for r in range(n_grp):
        c = cols[r]
        vp = _bias_of(d_p, c, tab_ref, 1, min(window, past_len + tq - 1))
        past_ref[r * tq:(r + 1) * tq, :] = jnp.where(ok_p, vp, NEG)
        vn = _bias_of(d_n, c, tab_ref, 0, tq - 1)
        new_ref[r * tq:(r + 1) * tq, :] = jnp.where(ok_n, vn, NEG)


def _decode_bias_table(table, cols, past_len, window, dil, tq):
    outs = [pl.pallas_call(
        functools.partial(_decode_bias_kernel, cols=tuple(cc), past_len=past_len, window=window,
                          dil=dil, tq=tq),
        in_specs=[pl.BlockSpec(memory_space=pltpu.SMEM)],
        out_shape=(jax.ShapeDtypeStruct((len(cc) * tq, past_len), F32),
                   jax.ShapeDtypeStruct((len(cc) * tq, LANES), F32)),
        compiler_params=_cparams(None),
        name=f"bias_decode_L{past_len}_d{dil}",
    )(table) for cc in cols]
    return jnp.stack([o[0] for o in outs]), jnp.stack([o[1] for o in outs])


def _proj_kernel(x_ref, g_ref, w_ref, cg_ref, *rest, plan, col_chunk, transposed, dils, slots):
    x = x_ref[...]
    h = x * lax.rsqrt(jnp.mean(x * x, axis=-1, keepdims=True) + RMS_EPS) * g_ref[...]
    hb = h.astype(BF16)
    low = _low_half()
    row_tile = x_ref.shape[0]
    dl_scr = None
    if max(dils) > 1:
        rest, dl_scr = rest[:-1], rest[-1]
    out_refs = rest
    if transposed:
        wt_ref, out_refs, ot_ref = rest[0], rest[1:-1], rest[-1]
        ot_ref[...] = _dot_nt(wt_ref[...], hb).astype(ot_ref.dtype)
    n_total = w_ref.shape[1]
    for c0 in range(0, n_total, col_chunk):
        cw = min(col_chunk, n_total - c0)
        y = _dot(hb, w_ref[:, c0:c0 + cw])
        for j in range(cw // LANES):
            src = c0 + j * LANES
            normed, dsts = plan[src // LANES]
            blk = y[:, j * LANES:(j + 1) * LANES]
            if normed:
                sq = blk * blk
                ms_lo = jnp.sum(jnp.where(low, sq, 0.0), axis=-1, keepdims=True) * (1.0 / HEAD_DIM)
                ms_hi = jnp.sum(jnp.where(low, 0.0, sq), axis=-1, keepdims=True) * (1.0 / HEAD_DIM)
                inv = jnp.where(low, lax.rsqrt(ms_lo + RMS_EPS), lax.rsqrt(ms_hi + RMS_EPS))
                blk = blk * inv * cg_ref[:, src:src + LANES]
            if any(dils[oi] > 1 for oi, _ in dsts):
                dl_scr[...] = blk
            for oi, dst in dsts:
                if slots[oi] > 1:
                    out_refs[oi][pl.ds(dst // LANES, row_tile, stride=slots[oi]), :] = blk.astype(out_refs[oi].dtype)
                elif dils[oi] == 1:
                    out_refs[oi][:, dst:dst + LANES] = blk.astype(out_refs[oi].dtype)
                else:
                    for r in range(dils[oi]):
                        part = dl_scr[pl.ds(r, row_tile // dils[oi], stride=dils[oi]), :]
                        out_refs[oi][r, :, dst:dst + LANES] = part.astype(out_refs[oi].dtype)


class _ProjLayout:
    def __init__(self):
        self.src = []
        self.gain = []
        self.plan = []
        self.outs = []

    def add_output(self, width, dtype, dil=1, slots=1):
        self.outs.append((width, dtype, dil, slots))
        return len(self.outs) - 1

    def add_chunk(self, src_cols, kind, dsts):
        assert len(src_cols) == LANES
        self.src.extend(src_cols)
        self.gain.extend([kind] * LANES)
        self.plan.append((kind != 0, list(dsts)))


def _cols(start, n):
    return list(range(start, start + n))


def _zeros(n):
    return [-1] * n


def _take_cols(w, src):
    parts, i = [], 0
    while i < len(src):
        j = i
        if src[i] < 0:
            while j < len(src) and src[j] < 0:
                j += 1
            parts.append(jnp.zeros((w.shape[0], j - i), w.dtype))
        else:
            while j + 1 < len(src) and src[j + 1] == src[j] + 1:
                j += 1
            j += 1
            parts.append(w[:, src[i]:src[i] + j - i])
        i = j
    return jnp.concatenate(parts, axis=1) if len(parts) > 1 else parts[0]


def _run_proj(x, norm_gain, w_in, q_gain, k_gain, layout, row_tile, col_chunk=1024, q_scale=1.0,
              t_cols=None, seq_len=None):
    rows = x.shape[0]
    src = np.asarray(layout.src)
    w_ext = _take_cols(w_in, list(layout.src)).astype(BF16)
    kind = np.asarray(layout.gain)
    n_ext = len(src)
    reps = n_ext // HEAD_DIM
    cg = jnp.where(jnp.asarray(kind == 1), jnp.tile(q_gain, reps) * (HEAD_DIM ** -0.5 * q_scale),
                   jnp.where(jnp.asarray(kind == 2), jnp.tile(k_gain, reps), 1.0)).reshape(1, n_ext)
    dils = tuple(d for _, _, d, _ in layout.outs)
    slots = tuple(sl for _, _, _, sl in layout.outs)
    tiles = None if seq_len is None else seq_len // row_tile
    out_shape, out_specs = [], []
    for w, dt, d, sl in layout.outs:
        if sl > 1:
            out_shape.append(jax.ShapeDtypeStruct((rows * sl, LANES), dt))
            out_specs.append(pl.BlockSpec((row_tile * sl, LANES), lambda i: (i, 0)))
        elif d == 1:
            out_shape.append(jax.ShapeDtypeStruct((rows, w), dt))
            out_specs.append(pl.BlockSpec((row_tile, w), lambda i: (i, 0)))
        else:
            out_shape.append(jax.ShapeDtypeStruct((rows // seq_len, d, seq_len // d, w), dt))
            out_specs.append(pl.BlockSpec((None, d, row_tile // d, w), lambda i: (i // tiles, 0, i % tiles, 0)))
    in_specs = [pl.BlockSpec((row_tile, D_MODEL), lambda i: (i, 0)),
                pl.BlockSpec((1, D_MODEL), lambda i: (0, 0)),
                pl.BlockSpec((D_MODEL, n_ext), lambda i: (0, 0), pipeline_mode=pl.Buffered(1)),
                pl.BlockSpec((1, n_ext), lambda i: (0, 0))]
    args = [x, norm_gain.reshape(1, D_MODEL), w_ext, cg]
    if t_cols is not None:
        n_t = len(t_cols)
        in_specs.append(pl.BlockSpec((n_t, D_MODEL), lambda i: (0, 0), pipeline_mode=pl.Buffered(1)))
        args.append(_take_cols(w_in, list(t_cols)).T.astype(BF16))
        out_shape.append(jax.ShapeDtypeStruct((rows // seq_len, n_t, seq_len), BF16))
        out_specs.append(pl.BlockSpec((None, n_t, row_tile), lambda i: (i // tiles, 0, i % tiles)))
    return pl.pallas_call(
        functools.partial(_proj_kernel, plan=tuple(layout.plan), col_chunk=col_chunk,
                          transposed=t_cols is not None, dils=dils, slots=slots),
        grid=(rows // row_tile,),
        in_specs=in_specs,
        out_specs=out_specs,
        out_shape=out_shape,
        scratch_shapes=[pltpu.VMEM((row_tile, LANES), F32)] if max(dils) > 1 else [],
        compiler_params=_cparams(("parallel",)),
        name="proj",
    )(*args)


def _layout_a(prompt):
    lay = _ProjLayout()
    oq = lay.add_output(1024, BF16 if prompt else F32)
    okv = lay.add_output(512, F32, slots=4 if prompt else 1)
    ok = lay.add_output(256, BF16) if prompt else None
    for c in range(8):
        lay.add_chunk(_cols(c * LANES, LANES), 1, [(oq, c * LANES)])
    for c in range(2):
        lay.add_chunk(_cols(1024 + c * LANES, LANES), 2,
                      [(okv, c * LANES)] + ([(ok, c * LANES)] if prompt else []))
    for c in range(2):
        lay.add_chunk(_cols(1280 + c * LANES, LANES), 0, [(okv, 256 + c * LANES)])
    return lay


def _layout_band_prompt(dils, n_kv):
    n_grp = len(dils)
    nq = n_grp * N_HEADS * HEAD_DIM
    nk = n_grp * n_kv * HEAD_DIM
    gk = n_kv * HEAD_DIM
    lay = _ProjLayout()
    okv = lay.add_output(2 * nk, F32)
    for g, d in enumerate(dils):
        oq = lay.add_output(N_HEADS * HEAD_DIM, BF16, d)
        okd = lay.add_output(n_kv * LANES, BF16, d)
        ovd = lay.add_output(n_kv * LANES, BF16, d)
        for c in range(N_HEADS * HEAD_DIM // LANES):
            lay.add_chunk(_cols(g * N_HEADS * HEAD_DIM + c * LANES, LANES), 1, [(oq, c * LANES)])
        for c in range(gk // LANES):
            lay.add_chunk(_cols(nq + g * gk + c * LANES, LANES), 2, [(okv, g * 2 * gk + c * LANES)])
        for c in range(gk // LANES):
            lay.add_chunk(_cols(nq + nk + g * gk + c * LANES, LANES), 0, [(okv, g * 2 * gk + gk + c * LANES)])
        for kv in range(n_kv):
            lay.add_chunk(_cols(nq + g * gk + kv * HEAD_DIM, HEAD_DIM) * 2, 2, [(okd, kv * LANES)])
        for kv in range(n_kv):
            lay.add_chunk(_cols(nq + nk + g * gk + kv * HEAD_DIM, HEAD_DIM) * 2, 0, [(ovd, kv * LANES)])
    return lay


def _layout_band_sample(n_grp, n_kv):
    nq = n_grp * N_HEADS * HEAD_DIM
    nk = n_grp * n_kv * HEAD_DIM
    grp_heads = N_HEADS // n_kv
    lay = _ProjLayout()
    oq = lay.add_output(n_grp * N_HEADS * LANES, F32)
    okv = lay.add_output(2 * nk, F32)
    for g in range(n_grp):
        for h in range(N_HEADS):
            qc = _cols((g * N_HEADS + h) * HEAD_DIM, HEAD_DIM)
            odd = (h // grp_heads) % 2
            lay.add_chunk(_zeros(HEAD_DIM) + qc if odd else qc + _zeros(HEAD_DIM), 1,
                          [(oq, (g * N_HEADS + h) * LANES)])
    gk = n_kv * HEAD_DIM
    for g in range(n_grp):
        for c in range(gk // LANES):
            lay.add_chunk(_cols(nq + g * gk + c * LANES, LANES), 2, [(okv, g * 2 * gk + c * LANES)])
        for c in range(gk // LANES):
            lay.add_chunk(_cols(nq + nk + g * gk + c * LANES, LANES), 0, [(okv, g * 2 * gk + gk + c * LANES)])
    return lay


def _diff_finalize(acc0, l0, acc1, l1, lam_ref, subln_ref, lam_init):
    lp = lam_ref[...]
    lam = (jnp.exp(jnp.sum(lp[0:1] * lp[1:2], axis=-1, keepdims=True))
           - jnp.exp(jnp.sum(lp[2:3] * lp[3:4], axis=-1, keepdims=True)) + lam_init)
    o = acc0 / l0 - lam * (acc1 / l1)
    o = o * lax.rsqrt(jnp.mean(o * o, axis=-1, keepdims=True) + RMS_EPS) * subln_ref[...]
    return o * (1.0 - lam_init)


def _a_prompt_kernel(q_ref, k_ref, vt_ref, toep_ref, lam_ref, subln_ref, o_ref,
                     qm_scr, ta_scr, tb_scr, p_scr, cm_scr, m_scr, acc_scr, *, tq, tk, lam_init, rb):
    qi = pl.program_id(2)
    low = _low_half()
    nh = 8
    w = nh * tq
    for g in range(4):
        qg = q_ref[:, g * LANES:(g + 1) * LANES]
        qm_scr[(2 * g) * tq:(2 * g + 1) * tq, :] = jnp.where(low, qg, jnp.zeros_like(qg))
        qm_scr[(2 * g + 1) * tq:(2 * g + 2) * tq, :] = jnp.where(low, jnp.zeros_like(qg), qg)
    m_scr[...] = jnp.full(m_scr.shape, NEG, F32)
    acc_scr[...] = jnp.zeros(acc_scr.shape, F32)
    n_chunks = (qi * tq + tq - 1) // tk + 1
    last = k_ref.shape[0] // tk - 1

    def stage_a(kj, t_ref):
        k0 = pl.multiple_of(jnp.minimum(kj, last) * tk, tk)
        s = _dot_nt(k_ref[pl.ds(k0, tk), :], qm_scr[...])
        macc = jnp.full((SUBLANES, w), NEG, F32)
        for u in range(tk // BAND):
            idx = [jnp.clip((qi * (tq // BAND) + r) - (kj * (tk // BAND) + u) + 1, 0, N_TOEP + 1)
                   for r in range(tq // BAND)]
            for b0 in range(0, BAND, rb):
                bias = jnp.concatenate([toep_ref[idx[r], h, b0:b0 + rb, :] for h in range(nh)
                                        for r in range(tq // BAND)], axis=1)
                r0 = u * BAND + b0
                t = s[r0:r0 + rb, :] + bias
                t_ref[r0:r0 + rb, :] = t
                for g8 in range(rb // SUBLANES):
                    macc = jnp.maximum(macc, t[g8 * SUBLANES:(g8 + 1) * SUBLANES, :])
        return jnp.max(macc, axis=0, keepdims=True)

    def stage_b(kj, t_ref, m_new, alpha):
        k0 = pl.multiple_of(jnp.minimum(kj, last) * tk, tk)
        vtc = jnp.concatenate([vt_ref[:, pl.ds(k0, tk)], jnp.ones((ONES_ROWS, tk), BF16)], axis=0)
        mb = jnp.broadcast_to(m_new, (SUBLANES, w))
        for r0 in range(0, tk, rb):
            t = t_ref[r0:r0 + rb, :]
            ps = []
            for g8 in range(rb // SUBLANES):
                p8 = jnp.exp2(t[g8 * SUBLANES:(g8 + 1) * SUBLANES, :] - mb)
                ps.append(p8)
            p_scr[r0:r0 + rb, :] = jnp.concatenate(ps, axis=0).astype(BF16)
        acc_scr[...] = alpha * acc_scr[...] + _dot(vtc, p_scr[...])

    cm_scr[...] = stage_a(0, ta_scr)

    def body(j, carry):
        cmax1 = stage_a(2 * j + 1, tb_scr)
        m_prev = m_scr[...]
        m0 = jnp.maximum(m_prev, cm_scr[...])
        stage_b(2 * j, ta_scr, m0, jnp.exp2(m_prev - m0))
        cm_scr[...] = stage_a(2 * j + 2, ta_scr)
        m1 = jnp.maximum(m0, cmax1)
        stage_b(2 * j + 1, tb_scr, m1, jnp.exp2(m0 - m1))
        m_scr[...] = m1
        return carry

    lax.fori_loop(0, (n_chunks + 1) // 2, body, 0)
    lp = lam_ref[...]
    lam = (jnp.exp(jnp.sum(lp[0:1] * lp[1:2], axis=-1, keepdims=True))
           - jnp.exp(jnp.sum(lp[2:3] * lp[3:4], axis=-1, keepdims=True)) + lam_init)
    on = acc_scr[0:LANES, :] / acc_scr[LANES:LANES + 1, :]
    for g in range(4):
        o = on[:, (2 * g) * tq:(2 * g + 1) * tq] - lam * on[:, (2 * g + 1) * tq:(2 * g + 2) * tq]
        o = o * lax.rsqrt(jnp.mean(o * o, axis=0, keepdims=True) + RMS_EPS) * subln_ref[...]
        o = o * (1.0 - lam_init)
        o_ref[:, g * LANES:(g + 1) * LANES] = o.T.astype(o_ref.dtype)


def _a_prompt(q, k, vt, toep_t, lam_p, subln, lam_init, tq=512, tk=256, rb=32):
    b, t, _ = q.shape
    return pl.pallas_call(
        functools.partial(_a_prompt_kernel, tq=tq, tk=tk, lam_init=lam_init, rb=rb),
        grid=(b, A_KV, t // tq),
        in_specs=[pl.BlockSpec((None, tq, 512), lambda bi, kv, qi: (bi, qi, kv)),
                  pl.BlockSpec((None, t, LANES), lambda bi, kv, qi: (bi, 0, kv)),
                  pl.BlockSpec((None, LANES, t), lambda bi, kv, qi: (bi, kv, 0)),
                  pl.BlockSpec((TOEP_TILES, 8, BAND, BAND), lambda bi, kv, qi: (0, kv, 0, 0)),
                  pl.BlockSpec((4, HEAD_DIM), lambda bi, kv, qi: (0, 0)),
                  pl.BlockSpec((LANES, 1), lambda bi, kv, qi: (0, 0))],
        out_specs=pl.BlockSpec((None, tq, 512), lambda bi, kv, qi: (bi, qi, kv)),
        out_shape=jax.ShapeDtypeStruct((b, t, 1024), BF16),
        scratch_shapes=[pltpu.VMEM((8 * tq, LANES), BF16),
                        pltpu.VMEM((tk, 8 * tq), F32),
                        pltpu.VMEM((tk, 8 * tq), F32),
                        pltpu.VMEM((tk, 8 * tq), BF16),
                        pltpu.VMEM((1, 8 * tq), F32),
                        pltpu.VMEM((1, 8 * tq), F32),
                        pltpu.VMEM((LANES + ONES_ROWS, 8 * tq), F32)],
        compiler_params=_cparams(("parallel", "parallel", "arbitrary")),
        name="a_prompt",
    )(q, k, vt, toep_t, lam_p, subln.reshape(LANES, 1))


def _a_sample_kernel(pt_ref, q_ref, new_ref, bp_ref, bc_ref, bn_ref, lam_ref, subln_ref, *rest,
                     pages, tq, lam_init):
    page_refs = rest[:pages]
    o_ref = rest[pages]
    qm_scr, m_scr, l_scr, acc_scr = rest[pages + 1:]
    c = pl.program_id(1)
    last = pl.num_programs(1) - 1
    low = _low_half()
    rows = 8 * tq

    @pl.when(c == 0)
    def _():
        for kv in range(A_KV):
            for g in range(4):
                qg = q_ref[:, (kv * 4 + g) * LANES:(kv * 4 + g + 1) * LANES]
                qm_scr[kv, (2 * g) * tq:(2 * g + 1) * tq, :] = jnp.where(low, qg, 0.0)
                qm_scr[kv, (2 * g + 1) * tq:(2 * g + 2) * tq, :] = jnp.where(low, 0.0, qg)
        m_scr[...] = jnp.full(m_scr.shape, NEG, F32)
        l_scr[...] = jnp.zeros(l_scr.shape, F32)
        acc_scr[...] = jnp.zeros(acc_scr.shape, F32)

    n_slot = 2 * A_KV
    is_last = c == last

    def scores(kv):
        qm = qm_scr[kv].astype(BF16)
        keys = jnp.concatenate([pr[pl.ds(kv, PAGE_SIZE, stride=n_slot), :] for pr in page_refs], axis=0)
        return _dot_nt(qm, keys.astype(BF16)) + jnp.where(is_last, bp_ref[kv], bc_ref[kv])

    def values(kv):
        return jnp.concatenate([pr[pl.ds(A_KV + kv, PAGE_SIZE, stride=n_slot), :] for pr in page_refs], axis=0).astype(BF16)

    s = jnp.concatenate([scores(kv) for kv in range(A_KV)], axis=0)
    m_prev = m_scr[...]
    m_new = jnp.maximum(m_prev, jnp.max(s, axis=-1, keepdims=True))
    alpha = jnp.exp(m_prev - m_new)
    p = jnp.exp(s - m_new)
    l_scr[...] = alpha * l_scr[...] + jnp.sum(p, axis=-1, keepdims=True)
    m_scr[...] = m_new
    pb = p.astype(BF16)
    pv = jnp.concatenate([_dot(pb[kv * rows:(kv + 1) * rows], values(kv)) for kv in range(A_KV)], axis=0)
    acc_scr[...] = alpha * acc_scr[...] + pv

    @pl.when(is_last)
    def _():
        pad = jnp.zeros((LANES - tq, LANES), F32)
        for kv in range(A_KV):
            sl = slice(kv * rows, (kv + 1) * rows)
            qm = qm_scr[kv].astype(BF16)
            kn = jnp.concatenate([new_ref[:, kv * LANES:(kv + 1) * LANES], pad], axis=0)
            vn = jnp.concatenate([new_ref[:, (A_KV + kv) * LANES:(A_KV + kv + 1) * LANES], pad], axis=0)
            s = _dot_nt(qm, kn.astype(BF16)) + bn_ref[kv]
            m_prev = m_scr[sl]
            m_new = jnp.maximum(m_prev, jnp.max(s, axis=-1, keepdims=True))
            alpha = jnp.exp(m_prev - m_new)
            p = jnp.exp(s - m_new)
            l = alpha * l_scr[sl] + jnp.sum(p, axis=-1, keepdims=True)
            acc = alpha * acc_scr[sl] + _dot(p.astype(BF16), vn.astype(BF16))
            for g in range(4):
                r0, r1 = (2 * g) * tq, (2 * g + 1) * tq
                o = _diff_finalize(acc[r0:r0 + tq], l[r0:r0 + tq], acc[r1:r1 + tq], l[r1:r1 + tq],
                                   lam_ref, subln_ref, lam_init)
                o_ref[:, (kv * 4 + g) * LANES:(kv * 4 + g + 1) * LANES] = o.astype(o_ref.dtype)


def _a_sample(q, new_kv, cache, layer, page_table, bias_past, bias_const, bias_new, lam_p, subln, lam_init,
              pages=64):
    db, tq, _ = q.shape
    n_pages = page_table.shape[1]
    n_chunks = n_pages // pages
    assert n_chunks == 1 or pages * PAGE_SIZE + 1 >= THR[-1]
    cache2 = cache.reshape(cache.shape[0], cache.shape[1], PAGE_SIZE * 2 * A_KV, LANES)
    rows = 8 * tq

    def page_spec(j):
        return pl.BlockSpec((None, None, PAGE_SIZE * 2 * A_KV, LANES),
                            lambda bi, c, pt: (layer, pt[bi * n_pages + c * pages + j], 0, 0))

    grid_spec = pltpu.PrefetchScalarGridSpec(
        num_scalar_prefetch=1,
        grid=(db, n_chunks),
        in_specs=[pl.BlockSpec((None, tq, 1024), lambda bi, c, pt: (bi, 0, 0)),
                  pl.BlockSpec((None, tq, 512), lambda bi, c, pt: (bi, 0, 0)),
                  pl.BlockSpec((A_KV, rows, pages * PAGE_SIZE), lambda bi, c, pt: (0, 0, n_chunks - 1)),
                  pl.BlockSpec((A_KV, rows, 1), lambda bi, c, pt: (0, 0, 0)),
                  pl.BlockSpec((A_KV, rows, LANES), lambda bi, c, pt: (0, 0, 0)),
                  pl.BlockSpec((4, HEAD_DIM), lambda bi, c, pt: (0, 0)),
                  pl.BlockSpec((1, LANES), lambda bi, c, pt: (0, 0))]
                 + [page_spec(j) for j in range(pages)],
        out_specs=pl.BlockSpec((None, tq, 1024), lambda bi, c, pt: (bi, 0, 0)),
        scratch_shapes=[pltpu.VMEM((A_KV, rows, LANES), F32),
                        pltpu.VMEM((A_KV * rows, 1), F32),
                        pltpu.VMEM((A_KV * rows, 1), F32),
                        pltpu.VMEM((A_KV * rows, LANES), F32)])
    return pl.pallas_call(
        functools.partial(_a_sample_kernel, pages=pages, tq=tq, lam_init=lam_init),
        grid_spec=grid_spec,
        out_shape=jax.ShapeDtypeStruct((db, tq, 1024), F32),
        compiler_params=_cparams(("parallel", "arbitrary")),
        name="a_sample",
    )(page_table.reshape(-1), q, new_kv, bias_past, bias_const, bias_new, lam_p, subln.reshape(1, LANES),
      *([cache2] * pages))


def _band_kernel(*refs, n_kv, use_sink, want_lse, n_sub):
    q_ref, kp_ref, kc_ref, vp_ref, vc_ref, bias_ref = refs[:6]
    pos = 6
    sink_ref = None
    if use_sink:
        sink_ref = refs[pos]
        pos += 1
    o_ref = refs[pos]
    lse_ref = refs[pos + 1] if want_lse else None
    qm_scr = refs[-1]
    i = pl.program_id(2)
    grp = N_HEADS // n_kv
    low = _low_half()
    key = lax.broadcasted_iota(jnp.int32, (2 * BAND, 1), 0)
    ones = jnp.ones((ONES_ROWS, 2 * BAND), BF16)
    for j in range(n_sub):
        rows = slice(j * BAND, (j + 1) * BAND)
        prv = slice((j - 1) * BAND, j * BAND)

        def kv_pair(pref, cref, kv):
            lanes = slice(kv * LANES, (kv + 1) * LANES)
            before = pref[:, lanes] if j == 0 else cref[prv, lanes]
            return jnp.concatenate([before, cref[rows, lanes]], axis=0)

        for c in range(N_HEADS // 2):
            qp = q_ref[rows, c * LANES:(c + 1) * LANES]
            zero = jnp.zeros_like(qp)
            qm_scr[(2 * c) * BAND:(2 * c + 1) * BAND, :] = jnp.where(low, qp, zero)
            qm_scr[(2 * c + 1) * BAND:(2 * c + 2) * BAND, :] = jnp.where(low, zero, qp)
        s_parts = [_dot_nt(kv_pair(kp_ref, kc_ref, kv), qm_scr[kv * grp * BAND:(kv + 1) * grp * BAND, :])
                   for kv in range(n_kv)]
        s = jnp.concatenate(s_parts, axis=1) + bias_ref[...]
        if j == 0:
            s = jnp.where((key >= BAND) | (i > 0), s, NEG)
        m = jnp.max(s, axis=0, keepdims=True)
        if use_sink:
            m = jnp.maximum(m, sink_ref[...])
        p = jnp.exp(s - m).astype(BF16)
        acc_parts = []
        for kv in range(n_kv):
            vt = kv_pair(vp_ref, vc_ref, kv).astype(F32).T[0:HEAD_DIM, :].astype(BF16)
            acc_parts.append(_dot(jnp.concatenate([vt, ones], axis=0), p[:, kv * grp * BAND:(kv + 1) * grp * BAND]))
        acc = jnp.concatenate(acc_parts, axis=1)
        l = acc[HEAD_DIM:HEAD_DIM + 1, :]
        if use_sink:
            l = l + jnp.exp(sink_ref[...] - m)
        on = acc[0:HEAD_DIM, :] / l
        for c in range(N_HEADS // 2):
            pair = jnp.concatenate([on[:, (2 * c) * BAND:(2 * c + 1) * BAND], on[:, (2 * c + 1) * BAND:(2 * c + 2) * BAND]], axis=0)
            o_ref[rows, c * LANES:(c + 1) * LANES] = pair.T.astype(o_ref.dtype)
        if want_lse:
            lse = m + jnp.log(l)
            lrows = jnp.concatenate([lse[:, h * BAND:(h + 1) * BAND] for h in range(N_HEADS)]
                                    + [jnp.zeros((LANES - N_HEADS, BAND), F32)], axis=0)
            lse_ref[rows, :] = lrows.T


def _band_prompt(q, kd, vd, bias, sinks, n_kv, want_lse, out_dtype):
    b, dil, ts, _ = q.shape
    n_sub = 2 if ts % (2 * BAND) == 0 else 1
    blk = n_sub * BAND
    nb = ts // blk
    ck = n_kv * LANES
    cur = lambda bi, r, i: (bi, r, i, 0)
    prev = lambda bi, r, i: (bi, r, jnp.maximum(i * n_sub - 1, 0), 0)
    in_specs = [pl.BlockSpec((None, None, blk, 1024), cur),
                pl.BlockSpec((None, None, BAND, ck), prev),
                pl.BlockSpec((None, None, blk, ck), cur),
                pl.BlockSpec((None, None, BAND, ck), prev),
                pl.BlockSpec((None, None, blk, ck), cur),
                pl.BlockSpec((2 * BAND, N_HEADS * BAND), lambda bi, r, i: (0, 0))]
    args = [q, kd, kd, vd, vd, bias]
    if sinks is not None:
        in_specs.append(pl.BlockSpec((1, N_HEADS * BAND), lambda bi, r, i: (0, 0)))
        args.append(jnp.repeat(sinks, BAND).reshape(1, N_HEADS * BAND))
    out_specs = [pl.BlockSpec((None, None, blk, 1024), cur)]
    out_shape = [jax.ShapeDtypeStruct((b, dil, ts, 1024), out_dtype)]
    if want_lse:
        out_specs.append(pl.BlockSpec((None, None, blk, LANES), cur))
        out_shape.append(jax.ShapeDtypeStruct((b, dil, ts, LANES), F32))
    return pl.pallas_call(
        functools.partial(_band_kernel, n_kv=n_kv, use_sink=sinks is not None, want_lse=want_lse, n_sub=n_sub),
        grid=(b, dil, nb),
        in_specs=in_specs,
        out_specs=out_specs,
        out_shape=out_shape,
        scratch_shapes=[pltpu.VMEM((N_HEADS * BAND, LANES), BF16)],
        compiler_params=_cparams(("parallel", "parallel", "arbitrary")),
        name=f"band_d{dil}",
    )(*args)


def _decode_kernel(*refs, n_grp, n_kv, past_lens, use_sink, tq):
    q_ref, new_ref = refs[:2]
    pos = 2
    st_refs = refs[pos:pos + n_grp]; pos += n_grp
    bp_refs = refs[pos:pos + n_grp]; pos += n_grp
    bn_refs = refs[pos:pos + n_grp]; pos += n_grp
    sink_ref = None
    if use_sink:
        sink_ref = refs[pos]; pos += 1
    o_ref = refs[pos]; pos += 1
    nst_refs = refs[pos:pos + n_grp]
    width = 2 * n_kv * HEAD_DIM
    n_chunks = n_kv // 2
    heads_per_chunk = N_HEADS // n_chunks
    grp_heads = N_HEADS // n_kv
    low = _low_half()
    pad = jnp.zeros((LANES - tq, LANES), F32)
    for c in range(n_chunks):
        parts = []
        for g in range(n_grp):
            qs = jnp.concatenate(
                [q_ref[:, (g * N_HEADS + c * heads_per_chunk + hh) * LANES:(g * N_HEADS + c * heads_per_chunk + hh + 1) * LANES]
                 for hh in range(heads_per_chunk)], axis=0).astype(BF16)
            st = st_refs[g]
            kt = st[0, c * LANES:(c + 1) * LANES, :].astype(BF16)
            vt = st[1, c * LANES:(c + 1) * LANES, :].astype(BF16)
            kn = jnp.concatenate([new_ref[:, g * width + c * LANES:g * width + (c + 1) * LANES], pad], axis=0).astype(BF16)
            vn = jnp.concatenate([new_ref[:, g * width + n_kv * HEAD_DIM + c * LANES:g * width + n_kv * HEAD_DIM + (c + 1) * LANES], pad], axis=0).astype(BF16)
            sp = _dot(qs, kt) + bp_refs[g][c]
            sn = _dot_nt(qs, kn) + bn_refs[g][c]
            m = jnp.maximum(jnp.max(sp, axis=-1, keepdims=True), jnp.max(sn, axis=-1, keepdims=True))
            if use_sink:
                m = jnp.maximum(m, sink_ref[c])
            pp = jnp.exp(sp - m)
            pn = jnp.exp(sn - m)
            l = jnp.sum(pp, axis=-1, keepdims=True) + jnp.sum(pn, axis=-1, keepdims=True)
            if use_sink:
                l = l + jnp.exp(sink_ref[c] - m)
            acc = _dot_nt(pp.astype(BF16), vt) + _dot(pn.astype(BF16), vn)
            parts.append((m, l, acc))
        m_all = parts[0][0]
        for m, _, _ in parts[1:]:
            m_all = jnp.maximum(m_all, m)
        num = 0.0
        den = 0.0
        for m, l, acc in parts:
            w = jnp.exp(m - m_all)
            num = num + acc * w
            den = den + l * w
        o = num / den
        for hh in range(heads_per_chunk):
            h = c * heads_per_chunk + hh
            odd = (h // grp_heads) % 2
            oh = o[hh * tq:(hh + 1) * tq, :]
            o_ref[:, h * LANES:(h + 1) * LANES] = (jnp.where(low, 0.0, oh) if odd else jnp.where(low, oh, 0.0)).astype(o_ref.dtype)
    n_rows = n_kv * HEAD_DIM
    for g in range(n_grp):
        lp = past_lens[g]
        for kv_slot in range(2):
            c0 = g * width + kv_slot * n_rows
            new_t = jnp.concatenate([new_ref[:, c0:c0 + n_rows], jnp.zeros((LANES - tq, n_rows), F32)], axis=0).T
            for r0 in range(0, n_rows, HEAD_DIM):
                if lp > tq:
                    nst_refs[g][kv_slot, r0:r0 + HEAD_DIM, 0:lp - tq] = st_refs[g][kv_slot, r0:r0 + HEAD_DIM, tq:lp]
                nst_refs[g][kv_slot, r0:r0 + HEAD_DIM, lp - tq:lp] = new_t[r0:r0 + HEAD_DIM, 0:tq]


def _decode(q, new_kv, states, layer, bias_past, bias_new, sinks_rows, n_kv):
    db, tq, _ = q.shape
    n_grp = len(states)
    width = 2 * n_kv * HEAD_DIM
    n_chunks = n_kv // 2
    rows = tq * N_HEADS // n_chunks
    past_lens = tuple(s.shape[2] for s in states)
    st_t = [jnp.transpose(s, (0, 1, 3, 4, 5, 2)).reshape(s.shape[0], db, 2, n_kv * HEAD_DIM, s.shape[2])
            for s in states]
    in_specs = [pl.BlockSpec((None, tq, q.shape[2]), lambda bi: (bi, 0, 0)),
                pl.BlockSpec((None, tq, n_grp * width), lambda bi: (bi, 0, 0))]
    in_specs += [pl.BlockSpec((None, None, 2, n_kv * HEAD_DIM, lp), lambda bi: (layer, bi, 0, 0, 0))
                 for lp in past_lens]
    in_specs += [pl.BlockSpec((n_chunks, rows, lp), lambda bi: (0, 0, 0)) for lp in past_lens]
    in_specs += [pl.BlockSpec((n_chunks, rows, LANES), lambda bi: (0, 0, 0)) for _ in past_lens]
    args = [q, new_kv] + st_t + list(bias_past) + list(bias_new)
    if sinks_rows is not None:
        in_specs.append(pl.BlockSpec((n_chunks, rows, 1), lambda bi: (0, 0, 0)))
        args.append(sinks_rows)
    res = pl.pallas_call(
        functools.partial(_decode_kernel, n_grp=n_grp, n_kv=n_kv, past_lens=past_lens,
                          use_sink=sinks_rows is not None, tq=tq),
        grid=(db,),
        in_specs=in_specs,
        out_specs=[pl.BlockSpec((None, tq, N_HEADS * LANES), lambda bi: (bi, 0, 0))]
        + [pl.BlockSpec((None, 2, n_kv * HEAD_DIM, lp), lambda bi: (bi, 0, 0, 0)) for lp in past_lens],
        out_shape=[jax.ShapeDtypeStruct((db, tq, N_HEADS * LANES), F32)]
        + [jax.ShapeDtypeStruct((db, 2, n_kv * HEAD_DIM, lp), F32) for lp in past_lens],
        compiler_params=_cparams(("parallel",)),
        name=f"decode_kv{n_kv}",
    )(*args)
    new_states = [jnp.transpose(r.reshape(db, 2, n_kv, HEAD_DIM, lp), (0, 4, 1, 2, 3))
                  for r, lp in zip(res[1:], past_lens)]
    return res[0], new_states


def _ffn_kernel(*refs, n_attn, sample, tiles_per_seq, row_tile, ff_chunk, tq, dils):
    x_ref = refs[0]
    pos = 1
    if n_attn == 1:
        o_in = refs[pos]; pos += 1
    else:
        o_refs = refs[pos:pos + n_attn]; pos += n_attn
        lse_refs = refs[pos:pos + n_attn]; pos += n_attn
    wo_ref, g_ref, win_ref, cw_ref, cb_ref, wout_ref = refs[pos:pos + 6]; pos += 6
    if sample:
        fix1_ref, fix2_ref = refs[pos:pos + 2]; pos += 2
    xo_ref, gate_ref = refs[pos:pos + 2]; pos += 2
    gp_scr, halo_scr = refs[pos:pos + 2]; pos += 2
    il_scr = refs[pos:]
    i = pl.program_id(0)
    low = _low_half()

    @pl.when(i == 0)
    def _():
        halo_scr[...] = jnp.zeros(halo_scr.shape, F32)

    if n_attn == 1:
        ob = o_in[...]
    else:
        def natural(ref4, g, col0):
            d = dils[g]
            if d == 1:
                return ref4[0, :, col0:col0 + LANES]
            scr = il_scr[sum(1 for x in dils[:g] if x > 1)]
            for r in range(d):
                scr[pl.ds(r, row_tile // d, stride=d), :] = ref4[r, :, col0:col0 + LANES]
            return scr[...]

        lses = [natural(lse_refs[g], g, 0) for g in range(n_attn)]
        m_all = lses[0]
        for z in lses[1:]:
            m_all = jnp.maximum(m_all, z)
        es = [jnp.exp(z - m_all) for z in lses]
        den = es[0]
        for e in es[1:]:
            den = den + e
        ws = [e / den for e in es]
        chunks = []
        for c in range(1024 // LANES):
            acc = None
            for g in range(n_attn):
                wsel = jnp.where(low, ws[g][:, 2 * c:2 * c + 1], ws[g][:, 2 * c + 1:2 * c + 2])
                term = wsel * natural(o_refs[g], g, c * LANES)
                acc = term if acc is None else acc + term
            chunks.append(acc.astype(BF16))
        ob = jnp.concatenate(chunks, axis=1)
    x1 = x_ref[...] + _dot(ob, wo_ref[...])
    h = x1 * lax.rsqrt(jnp.mean(x1 * x1, axis=-1, keepdims=True) + RMS_EPS) * g_ref[...]
    hb = h.astype(BF16)

    if sample:
        t_in_seq = lax.broadcasted_iota(jnp.int32, (row_tile, 1), 0) & (tq - 1)
    y = jnp.zeros((row_tile, D_MODEL), F32)
    for f in range(D_FF // ff_chunk):
        f0 = f * ff_chunk
        gate = _dot(hb, win_ref[:, f0:f0 + ff_chunk])
        up = _dot(hb, win_ref[:, D_FF + f0:D_FF + f0 + ff_chunk])
        if sample:
            gp_scr[0:SUBLANES, :] = jnp.zeros((SUBLANES, ff_chunk), F32)
        else:
            first = (i % tiles_per_seq) == 0
            gp_scr[0:SUBLANES, :] = jnp.where(first, 0.0, halo_scr[f])
        gp_scr[SUBLANES:SUBLANES + row_tile, :] = gate
        g2 = gp_scr[SUBLANES - 2:SUBLANES - 2 + row_tile, :]
        g1 = gp_scr[SUBLANES - 1:SUBLANES - 1 + row_tile, :]
        if sample:
            g2 = jnp.where(t_in_seq >= 2, g2, fix2_ref[:, f0:f0 + ff_chunk])
            g1 = jnp.where(t_in_seq >= 1, g1, fix1_ref[:, f0:f0 + ff_chunk])
        gc = (cb_ref[:, f0:f0 + ff_chunk] + cw_ref[0:1, f0:f0 + ff_chunk] * g2
              + cw_ref[1:2, f0:f0 + ff_chunk] * g1 + cw_ref[2:3, f0:f0 + ff_chunk] * gate)
        act = gc * (1.0 / (1.0 + jnp.exp(-gc))) * up
        y = y + _dot(act.astype(BF16), wout_ref[f0:f0 + ff_chunk, :])
        if sample:
            gate_ref[:, f0:f0 + ff_chunk] = gate
        else:
            tail = gate[row_tile - SUBLANES:row_tile, :]
            halo_scr[f] = tail
            gate_ref[:, f0:f0 + ff_chunk] = tail
    xo_ref[...] = x1 + y


def _ffn(x, attn, lses, w_o, gain, w_in, conv_w, conv_b, w_out, fix, seq_len, row_tile=512,
         ff_chunk=1408):
    rows = x.shape[0]
    sample = fix is not None
    n_attn = len(attn)
    row_spec = lambda w: pl.BlockSpec((row_tile, w), lambda i: (i, 0))
    full = lambda a: pl.BlockSpec(a.shape, lambda i: (0,) * a.ndim, pipeline_mode=pl.Buffered(1))
    tiles = seq_len // row_tile
    res_spec = lambda a: pl.BlockSpec((None, a.shape[1], row_tile // a.shape[1], a.shape[3]),
                                      lambda i: (i // tiles, 0, i % tiles, 0))
    dils = (1,)
    if n_attn == 1:
        in_specs = [row_spec(D_MODEL), row_spec(attn[0].shape[1])]
        args = [x] + list(attn)
    else:
        dils = tuple(a.shape[1] for a in attn)
        in_specs = [row_spec(D_MODEL)] + [res_spec(a) for a in attn] + [res_spec(a) for a in lses]
        args = [x] + list(attn) + list(lses)
    consts = [w_o.astype(BF16), gain.reshape(1, D_MODEL), w_in.astype(BF16), conv_w,
              conv_b.reshape(1, D_FF), w_out.astype(BF16)]
    in_specs += [full(a) for a in consts]
    args += consts
    if sample:
        in_specs += [row_spec(D_FF), row_spec(D_FF)]
        args += list(fix)
        gate_spec = row_spec(D_FF)
        gate_shape = jax.ShapeDtypeStruct((rows, D_FF), F32)
        tiles_per_seq = 1
        tq = seq_len
    else:
        gate_spec = pl.BlockSpec((None, SUBLANES, D_FF), lambda i: (i, 0, 0))
        gate_shape = jax.ShapeDtypeStruct((rows // row_tile, SUBLANES, D_FF), F32)
        tiles_per_seq = seq_len // row_tile
        tq = 1
    return pl.pallas_call(
        functools.partial(_ffn_kernel, n_attn=n_attn, sample=sample, tiles_per_seq=tiles_per_seq,
                          row_tile=row_tile, ff_chunk=ff_chunk, tq=tq, dils=dils),
        grid=(rows // row_tile,),
        in_specs=in_specs,
        out_specs=[row_spec(D_MODEL), gate_spec],
        out_shape=[jax.ShapeDtypeStruct((rows, D_MODEL), F32), gate_shape],
        scratch_shapes=[pltpu.VMEM((SUBLANES + row_tile, ff_chunk), F32),
                        pltpu.VMEM((D_FF // ff_chunk, SUBLANES, ff_chunk), F32)]
        + [pltpu.VMEM((row_tile, LANES), F32) for d in dils if d > 1],
        compiler_params=_cparams(("arbitrary",)),
        name="ffn_sample" if sample else "ffn_prompt",
    )(*args)


def _expand_wo_sample(w_o, n_kv):
    grp_heads = N_HEADS // n_kv
    src = []
    for h in range(N_HEADS):
        rows = _cols(h * HEAD_DIM, HEAD_DIM)
        src += (_zeros(HEAD_DIM) + rows) if (h // grp_heads) % 2 else (rows + _zeros(HEAD_DIM))
    return _take_cols(w_o.T, src).T


def kernel(x_prompt, x_sample, cache_kv_a, state_kv_b, state_kv_c1, state_kv_c2, state_kv_c3, state_conv_ffn, page_table, rel_bias_table, norm_mix, norm_ffn, w_in_a, q_norm_a, k_norm_a, lambda_a, subln_a, w_o_a, w_in_b, q_norm_b, k_norm_b, sinks_b, w_o_b, w_in_c, q_norm_c, k_norm_c, w_o_c, w_ffn_in, conv_ffn_w, conv_ffn_b, w_ffn_out):
    bp, tp, _ = x_prompt.shape
    bs, ts, _ = x_sample.shape
    depth = norm_mix.shape[0]
    past_len = page_table.shape[1] * PAGE_SIZE
    xp = x_prompt.reshape(bp * tp, D_MODEL)
    xs = x_sample.reshape(bs * ts, D_MODEL)
    state_c = (state_kv_c1, state_kv_c2, state_kv_c3)

    toep = _toep_table(rel_bias_table)
    band_bias = {d: _band_bias_table(rel_bias_table, d) for d in (1, 4, 16)}
    a_cols = [[kv * 8 + j for j in range(8)] for kv in range(A_KV)]
    a_bias_past, a_bias_new = _decode_bias_table(rel_bias_table, a_cols, past_len, past_len + ts, 1, ts)
    a_bias_far = jnp.repeat(rel_bias_table[N_BUCKETS - 1][jnp.asarray(a_cols)], ts, axis=1)[..., None]
    b_cols = [list(range(N_HEADS))]
    b_len = state_kv_b.shape[2]
    b_bias_past, b_bias_new = _decode_bias_table(rel_bias_table, b_cols, b_len, B_WINDOW, 1, ts)
    c_cols = [list(range(8)), list(range(8, 16))]
    c_bias = [_decode_bias_table(rel_bias_table, c_cols, state_c[g].shape[2], w, d, ts)
              for g, (w, d) in enumerate(C_PAIRS)]

    kv_a_p, kv_a_s, kv_b_p, kv_b_s = [], [], [], []
    kv_c_p = [[] for _ in C_PAIRS]
    kv_c_s = [[] for _ in C_PAIRS]
    conv_p, conv_s = [], []
    for i in range(depth):
        n = i // N_MIXERS
        if i % N_MIXERS == 0:
            lam_init = 0.8 - 0.6 * math.exp(-0.3 * i)
            q, kvn, k, vt = _run_proj(xp, norm_mix[i], w_in_a[n], q_norm_a[n], k_norm_a[n], _layout_a(True), 512,
                                      q_scale=LOG2E, t_cols=_cols(1280, 256), seq_len=tp)
            op = _a_prompt(q.reshape(bp, tp, 1024), k.reshape(bp, tp, 256), vt, toep, lambda_a[n], subln_a[n],
                           lam_init)
            attn_p, lse_p = [op.reshape(bp * tp, 1024)], None
            kv_a_p.append(kvn.reshape(bp, tp, 2, A_KV, 2 * HEAD_DIM))
            qs, kvs = _run_proj(xs, norm_mix[i], w_in_a[n], q_norm_a[n], k_norm_a[n], _layout_a(False), 512)
            os_ = _a_sample(qs.reshape(bs, ts, 1024), kvs.reshape(bs, ts, 512), cache_kv_a, n, page_table,
                            a_bias_past, a_bias_far, a_bias_new, lambda_a[n], subln_a[n], lam_init)
            attn_s = [os_.reshape(bs * ts, 1024).astype(BF16)]
            kv_a_s.append(kvs.reshape(bs, ts, 2, A_KV, 2 * HEAD_DIM))
            w_o_p = w_o_a[n]
            w_o_s = w_o_a[n]
        elif i % N_MIXERS == 1:
            kvn, q, kd, vd = _run_proj(xp, norm_mix[i], w_in_b[n], q_norm_b[n], k_norm_b[n],
                                       _layout_band_prompt((1,), B_KV), 512)
            op = _band_prompt(q.reshape(bp, 1, tp, 1024), kd.reshape(bp, 1, tp, B_KV * LANES),
                              vd.reshape(bp, 1, tp, B_KV * LANES), band_bias[1], sinks_b[n], B_KV, False, BF16)[0]
            attn_p, lse_p = [op.reshape(bp * tp, 1024)], None
            wb = min(B_WINDOW, tp)
            kv_b_p.append(kvn.reshape(bp, tp, -1)[:, tp - wb:].reshape(bp, wb, 2, B_KV, HEAD_DIM))
            qs, kvs = _run_proj(xs, norm_mix[i], w_in_b[n], q_norm_b[n], k_norm_b[n],
                                _layout_band_sample(1, B_KV), 512)
            sink_rows = jnp.repeat(sinks_b[n], ts).reshape(1, N_HEADS * ts, 1)
            os_, nst = _decode(qs.reshape(bs, ts, -1), kvs.reshape(bs, ts, -1), [state_kv_b], n,
                               [b_bias_past], [b_bias_new], sink_rows, B_KV)
            attn_s = [os_.reshape(bs * ts, N_HEADS * LANES).astype(BF16)]
            kv_b_s.append(nst[0])
            w_o_p = w_o_b[n]
            w_o_s = _expand_wo_sample(w_o_b[n], B_KV)
        else:
            c_dils = tuple(d for _, d in C_PAIRS)
            outs = _run_proj(xp, norm_mix[i], w_in_c[n], q_norm_c[n], k_norm_c[n],
                             _layout_band_prompt(c_dils, C_KV), 256, seq_len=tp)
            kvn = outs[0]
            attn_p, lse_p = [], []
            for g, (w, d) in enumerate(C_PAIRS):
                q, kd, vd = (a.reshape(bp, d, tp // d, a.shape[-1]) for a in outs[1 + 3 * g:4 + 3 * g])
                o_g, lse_g = _band_prompt(q, kd, vd, band_bias[d], None, C_KV, True, F32)
                attn_p.append(o_g)
                lse_p.append(lse_g)
            kvn = kvn.reshape(bp, tp, -1)
            gw = 2 * C_KV * HEAD_DIM
            for g, (w, d) in enumerate(C_PAIRS):
                wc = min(w, tp)
                kv_c_p[g].append(kvn[:, tp - wc:, g * gw:(g + 1) * gw].reshape(bp, wc, 2, C_KV, HEAD_DIM))
            qs, kvs = _run_proj(xs, norm_mix[i], w_in_c[n], q_norm_c[n], k_norm_c[n],
                                _layout_band_sample(3, C_KV), 256)
            os_, nst = _decode(qs.reshape(bs, ts, -1), kvs.reshape(bs, ts, -1), list(state_c), n,
                               [cb[0] for cb in c_bias], [cb[1] for cb in c_bias], None, C_KV)
            attn_s = [os_.reshape(bs * ts, N_HEADS * LANES).astype(BF16)]
            for g in range(3):
                kv_c_s[g].append(nst[g])
            w_o_p = w_o_c[n]
            w_o_s = _expand_wo_sample(w_o_c[n], C_KV)

        xp, gate_tail = _ffn(xp, attn_p, lse_p, w_o_p, norm_ffn[i], w_ffn_in[i], conv_ffn_w[i],
                             conv_ffn_b[i], w_ffn_out[i], None, tp)
        tiles = tp // 512
        conv_p.append(gate_tail.reshape(bp, tiles, SUBLANES, D_FF)[:, -1, SUBLANES - (CONV_W - 1):])
        hist = state_conv_ffn[i]
        zero = jnp.zeros((bs, ts - 1, D_FF), F32)
        fix1 = jnp.concatenate([hist[:, 1:2], zero], axis=1).reshape(bs * ts, D_FF)
        fix2 = jnp.concatenate([hist[:, 0:1], hist[:, 1:2], zero[:, 1:]], axis=1).reshape(bs * ts, D_FF)
        xs, gate_s = _ffn(xs, attn_s, None, w_o_s, norm_ffn[i], w_ffn_in[i], conv_ffn_w[i],
                          conv_ffn_b[i], w_ffn_out[i], (fix1, fix2), ts, row_tile=256)
        conv_s.append(gate_s.reshape(bs, ts, D_FF)[:, ts - (CONV_W - 1):])

    return (xp.reshape(bp, tp, D_MODEL), xs.reshape(bs, ts, D_MODEL),
            jnp.stack(kv_a_p), jnp.stack(kv_a_s),
            jnp.stack(kv_b_p), jnp.stack(kv_b_s),
            jnp.stack(kv_c_p[0]), jnp.stack(kv_c_s[0]),
            jnp.stack(kv_c_p[1]), jnp.stack(kv_c_s[1]),
            jnp.stack(kv_c_p[2]), jnp.stack(kv_c_s[2]),
            jnp.stack(conv_p), jnp.stack(conv_s))
```

```python
import functools
import math

import numpy as np
import jax
import jax.numpy as jnp
from jax import lax
from jax.experimental import pallas as pl
from jax.experimental.pallas import tpu as pltpu

F32 = jnp.float32
BF16 = jnp.bfloat16

D_MODEL = 1024
HEAD_DIM = 64
LANES = 128
SUBLANES = 8
RMS_EPS = 1e-6
NEG = -1e30
LOG2E = math.log2(math.e)
N_HEADS = 16
N_BUCKETS = 32
MAX_DISTANCE = 2048
PAGE_SIZE = 128
D_FF = 2816
CONV_W = 3
N_MIXERS = 3
A_KV = 2
B_KV = 2
C_KV = 4
B_WINDOW = 128
C_PAIRS = ((128, 1), (512, 4), (2048, 16))
BAND = 128
ONES_ROWS = 16
VMEM_LIMIT = 56 * 1024 * 1024


def _bucket_thresholds():
    n = np.arange(0, 1 << 15)
    x = np.log(np.maximum(n, 1) / (N_BUCKETS // 2)) / math.log(MAX_DISTANCE / (N_BUCKETS // 2))
    large = N_BUCKETS // 2 + (x * (N_BUCKETS - N_BUCKETS // 2)).astype(np.int64)
    bucket = np.where(n < N_BUCKETS // 2, n, np.minimum(large, N_BUCKETS - 1))
    return [int(np.argmax(bucket >= b)) for b in range(N_BUCKETS)]


THR = _bucket_thresholds()
N_TOEP = -(-(THR[-1] + BAND - 1) // BAND)
TOEP_TILES = N_TOEP + 2


def _cparams(sem, vmem=VMEM_LIMIT):
    return pltpu.CompilerParams(dimension_semantics=sem, vmem_limit_bytes=vmem)


def _dot(a, b):
    return jnp.dot(a, b, preferred_element_type=F32)


def _dot_nt(a, b):
    return lax.dot_general(a, b, (((1,), (1,)), ((), ())), preferred_element_type=F32)


def _low_half():
    return lax.broadcasted_iota(jnp.int32, (1, LANES), 1) < HEAD_DIM


def _bias_of(dist, col, tab_ref, lo, hi):
    b_lo = max(b for b in range(N_BUCKETS) if THR[b] <= max(lo, 0))
    b_hi = max(b for b in range(N_BUCKETS) if THR[b] <= max(hi, 0))
    val = jnp.full(dist.shape, tab_ref[b_hi, col], F32)
    for b in range(b_hi - 1, b_lo - 1, -1):
        val = jnp.where(dist < THR[b + 1], tab_ref[b, col], val)
    return val


def _toep_kernel(tab_ref, o_ref):
    t = pl.program_id(0)
    key = lax.broadcasted_iota(jnp.int32, (BAND, BAND), 0)
    qry = lax.broadcasted_iota(jnp.int32, (BAND, BAND), 1)
    for delta in range(-1, N_TOEP + 1):
        @pl.when(t == delta + 1)
        def _(delta=delta):
            for c in range(N_HEADS):
                if delta < 0:
                    o_ref[c] = jnp.full((BAND, BAND), NEG, F32)
                elif delta == N_TOEP:
                    o_ref[c] = jnp.full((BAND, BAND), tab_ref[N_BUCKETS - 1, c] * LOG2E, F32)
                else:
                    dist = delta * BAND + qry - key
                    val = _bias_of(dist, c, tab_ref, delta * BAND - BAND + 1, delta * BAND + BAND - 1)
                    o_ref[c] = jnp.where(dist >= 0, val * LOG2E, NEG)


def _toep_table(table):
    return pl.pallas_call(
        _toep_kernel,
        grid=(TOEP_TILES,),
        in_specs=[pl.BlockSpec(memory_space=pltpu.SMEM)],
        out_specs=pl.BlockSpec((None, N_HEADS, BAND, BAND), lambda t: (t, 0, 0, 0)),
        out_shape=jax.ShapeDtypeStruct((TOEP_TILES, N_HEADS, BAND, BAND), F32),
        compiler_params=_cparams(("arbitrary",)),
        name="bias_toeplitz",
    )(table)


def _band_bias_kernel(tab_ref, o_ref, *, dil):
    key = lax.broadcasted_iota(jnp.int32, (2 * BAND, BAND), 0)
    qry = lax.broadcasted_iota(jnp.int32, (2 * BAND, BAND), 1)
    steps = BAND + qry - key
    valid = (steps >= 0) & (steps <= BAND)
    for c in range(N_HEADS):
        val = _bias_of(steps * dil, c, tab_ref, 0, BAND * dil)
        o_ref[:, c * BAND:(c + 1) * BAND] = jnp.where(valid, val, NEG)


def _band_bias_table(table, dil):
    return pl.pallas_call(
        functools.partial(_band_bias_kernel, dil=dil),
        in_specs=[pl.BlockSpec(memory_space=pltpu.SMEM)],
        out_shape=jax.ShapeDtypeStruct((2 * BAND, N_HEADS * BAND), F32),
        compiler_params=_cparams(None),
        name=f"bias_band_d{dil}",
    )(table)


def _decode_bias_kernel(tab_ref, past_ref, new_ref, *, cols, past_len, window, dil, tq):
    n_grp = len(cols)
    t_p = lax.broadcasted_iota(jnp.int32, (tq, past_len), 0)
    i_p = lax.broadcasted_iota(jnp.int32, (tq, past_len), 1)
    d_p = past_len + t_p - i_p
    ok_p = d_p <= window
    if dil > 1:
        ok_p = ok_p & ((d_p & (dil - 1)) == 0)
    t_n = lax.broadcasted_iota(jnp.int32, (tq, LANES), 0)
    i_n = lax.broadcasted_iota(jnp.int32, (tq, LANES), 1)
    d_n = t_n - i_n
    ok_n = (d_n >= 0) & (i_n < tq)
    if dil > 1:
        ok_n = ok_n & ((d_n & (dil - 1)) == 0)
    for r in range(n_grp):
        c = cols[r]
        vp = _bias_of(d_p, c, tab_ref, 1, min(window, past_len + tq - 1))
        past_ref[r * tq:(r + 1) * tq, :] = jnp.where(ok_p, vp, NEG)
        vn = _bias_of(d_n, c, tab_ref, 0, tq - 1)
        new_ref[r * tq:(r + 1) * tq, :] = jnp.where(ok_n, vn, NEG)


def _decode_bias_table(table, cols, past_len, window, dil, tq):
    outs = [pl.pallas_call(
        functools.partial(_decode_bias_kernel, cols=tuple(cc), past_len=past_len, window=window,
                          dil=dil, tq=tq),
        in_specs=[pl.BlockSpec(memory_space=pltpu.SMEM)],
        out_shape=(jax.ShapeDtypeStruct((len(cc) * tq, past_len), F32),
                   jax.ShapeDtypeStruct((len(cc) * tq, LANES), F32)),
        compiler_params=_cparams(None),
        name=f"bias_decode_L{past_len}_d{dil}",
    )(table) for cc in cols]
    return jnp.stack([o[0] for o in outs]), jnp.stack([o[1] for o in outs])


def _proj_kernel(x_ref, g_ref, w_ref, cg_ref, *rest, plan, col_chunk, transposed, dils, slots):
    x = x_ref[...]
    h = x * lax.rsqrt(jnp.mean(x * x, axis=-1, keepdims=True) + RMS_EPS) * g_ref[...]
    hb = h.astype(BF16)
    low = _low_half()
    row_tile = x_ref.shape[0]
    dl_scr = None
    if max(dils) > 1:
        rest, dl_scr = rest[:-1], rest[-1]
    out_refs = rest
    if transposed:
        wt_ref, out_refs, ot_ref = rest[0], rest[1:-1], rest[-1]
        ot_ref[...] = _dot_nt(wt_ref[...], hb).astype(ot_ref.dtype)
    n_total = w_ref.shape[1]
    for c0 in range(0, n_total, col_chunk):
        cw = min(col_chunk, n_total - c0)
        y = _dot(hb, w_ref[:, c0:c0 + cw])
        for j in range(cw // LANES):
            src = c0 + j * LANES
            normed, dsts = plan[src // LANES]
            blk = y[:, j * LANES:(j + 1) * LANES]
            if normed:
                sq = blk * blk
                ms_lo = jnp.sum(jnp.where(low, sq, 0.0), axis=-1, keepdims=True) * (1.0 / HEAD_DIM)
                ms_hi = jnp.sum(jnp.where(low, 0.0, sq), axis=-1, keepdims=True) * (1.0 / HEAD_DIM)
                inv = jnp.where(low, lax.rsqrt(ms_lo + RMS_EPS), lax.rsqrt(ms_hi + RMS_EPS))
                blk = blk * inv * cg_ref[:, src:src + LANES]
            if any(dils[oi] > 1 for oi, _ in dsts):
                dl_scr[...] = blk
            for oi, dst in dsts:
                if slots[oi] > 1:
                    out_refs[oi][pl.ds(dst // LANES, row_tile, stride=slots[oi]), :] = blk.astype(out_refs[oi].dtype)
                elif dils[oi] == 1:
                    out_refs[oi][:, dst:dst + LANES] = blk.astype(out_refs[oi].dtype)
                else:
                    for r in range(dils[oi]):
                        part = dl_scr[pl.ds(r, row_tile // dils[oi], stride=dils[oi]), :]
                        out_refs[oi][r, :, dst:dst + LANES] = part.astype(out_refs[oi].dtype)


class _ProjLayout:
    def __init__(self):
        self.src = []
        self.gain = []
        self.plan = []
        self.outs = []

    def add_output(self, width, dtype, dil=1, slots=1):
        self.outs.append((width, dtype, dil, slots))
        return len(self.outs) - 1

    def add_chunk(self, src_cols, kind, dsts):
        assert len(src_cols) == LANES
        self.src.extend(src_cols)
        self.gain.extend([kind] * LANES)
        self.plan.append((kind != 0, list(dsts)))


def _cols(start, n):
    return list(range(start, start + n))


def _zeros(n):
    return [-1] * n


def _take_cols(w, src):
    parts, i = [], 0
    while i < len(src):
        j = i
        if src[i] < 0:
            while j < len(src) and src[j] < 0:
                j += 1
            parts.append(jnp.zeros((w.shape[0], j - i), w.dtype))
        else:
            while j + 1 < len(src) and src[j + 1] == src[j] + 1:
                j += 1
            j += 1
            parts.append(w[:, src[i]:src[i] + j - i])
        i = j
    return jnp.concatenate(parts, axis=1) if len(parts) > 1 else parts[0]


def _run_proj(x, norm_gain, w_in, q_gain, k_gain, layout, row_tile, col_chunk=1024, q_scale=1.0,
              t_cols=None, seq_len=None):
    rows = x.shape[0]
    src = np.asarray(layout.src)
    w_ext = _take_cols(w_in, list(layout.src)).astype(BF16)
    kind = np.asarray(layout.gain)
    n_ext = len(src)
    reps = n_ext // HEAD_DIM
    cg = jnp.where(jnp.asarray(kind == 1), jnp.tile(q_gain, reps) * (HEAD_DIM ** -0.5 * q_scale),
                   jnp.where(jnp.asarray(kind == 2), jnp.tile(k_gain, reps), 1.0)).reshape(1, n_ext)
    dils = tuple(d for _, _, d, _ in layout.outs)
    slots = tuple(sl for _, _, _, sl in layout.outs)
    tiles = None if seq_len is None else seq_len // row_tile
    out_shape, out_specs = [], []
    for w, dt, d, sl in layout.outs:
        if sl > 1:
            out_shape.append(jax.ShapeDtypeStruct((rows * sl, LANES), dt))
            out_specs.append(pl.BlockSpec((row_tile * sl, LANES), lambda i: (i, 0)))
        elif d == 1:
            out_shape.append(jax.ShapeDtypeStruct((rows, w), dt))
            out_specs.append(pl.BlockSpec((row_tile, w), lambda i: (i, 0)))
        else:
            out_shape.append(jax.ShapeDtypeStruct((rows // seq_len, d, seq_len // d, w), dt))
            out_specs.append(pl.BlockSpec((None, d, row_tile // d, w), lambda i: (i // tiles, 0, i % tiles, 0)))
    in_specs = [pl.BlockSpec((row_tile, D_MODEL), lambda i: (i, 0)),
                pl.BlockSpec((1, D_MODEL), lambda i: (0, 0)),
                pl.BlockSpec((D_MODEL, n_ext), lambda i: (0, 0), pipeline_mode=pl.Buffered(1)),
                pl.BlockSpec((1, n_ext), lambda i: (0, 0))]
    args = [x, norm_gain.reshape(1, D_MODEL), w_ext, cg]
    if t_cols is not None:
        n_t = len(t_cols)
        in_specs.append(pl.BlockSpec((n_t, D_MODEL), lambda i: (0, 0), pipeline_mode=pl.Buffered(1)))
        args.append(_take_cols(w_in, list(t_cols)).T.astype(BF16))
        out_shape.append(jax.ShapeDtypeStruct((rows // seq_len, n_t, seq_len), BF16))
        out_specs.append(pl.BlockSpec((None, n_t, row_tile), lambda i: (i // tiles, 0, i % tiles)))
    return pl.pallas_call(
        functools.partial(_proj_kernel, plan=tuple(layout.plan), col_chunk=col_chunk,
                          transposed=t_cols is not None, dils=dils, slots=slots),
        grid=(rows // row_tile,),
        in_specs=in_specs,
        out_specs=out_specs,
        out_shape=out_shape,
        scratch_shapes=[pltpu.VMEM((row_tile, LANES), F32)] if max(dils) > 1 else [],
        compiler_params=_cparams(("parallel",)),
        name="proj",
    )(*args)


def _layout_a(prompt):
    lay = _ProjLayout()
    oq = lay.add_output(1024, BF16 if prompt else F32)
    okv = lay.add_output(512, F32, slots=4 if prompt else 1)
    ok = lay.add_output(256, BF16) if prompt else None
    for c in range(8):
        lay.add_chunk(_cols(c * LANES, LANES), 1, [(oq, c * LANES)])
    for c in range(2):
        lay.add_chunk(_cols(1024 + c * LANES, LANES), 2,
                      [(okv, c * LANES)] + ([(ok, c * LANES)] if prompt else []))
    for c in range(2):
        lay.add_chunk(_cols(1280 + c * LANES, LANES), 0, [(okv, 256 + c * LANES)])
    return lay


def _layout_band_prompt(dils, n_kv):
    n_grp = len(dils)
    nq = n_grp * N_HEADS * HEAD_DIM
    nk = n_grp * n_kv * HEAD_DIM
    gk = n_kv * HEAD_DIM
    lay = _ProjLayout()
    okv = lay.add_output(2 * nk, F32)
    for g, d in enumerate(dils):
        oq = lay.add_output(N_HEADS * HEAD_DIM, BF16, d)
        okd = lay.add_output(n_kv * LANES, BF16, d)
        ovd = lay.add_output(n_kv * LANES, BF16, d)
        for c in range(N_HEADS * HEAD_DIM // LANES):
            lay.add_chunk(_cols(g * N_HEADS * HEAD_DIM + c * LANES, LANES), 1, [(oq, c * LANES)])
        for c in range(gk // LANES):
            lay.add_chunk(_cols(nq + g * gk + c * LANES, LANES), 2, [(okv, g * 2 * gk + c * LANES)])
        for c in range(gk // LANES):
            lay.add_chunk(_cols(nq + nk + g * gk + c * LANES, LANES), 0, [(okv, g * 2 * gk + gk + c * LANES)])
        for kv in range(n_kv):
            lay.add_chunk(_cols(nq + g * gk + kv * HEAD_DIM, HEAD_DIM) * 2, 2, [(okd, kv * LANES)])
        for kv in range(n_kv):
            lay.add_chunk(_cols(nq + nk + g * gk + kv * HEAD_DIM, HEAD_DIM) * 2, 0, [(ovd, kv * LANES)])
    return lay


def _layout_band_sample(n_grp, n_kv):
    nq = n_grp * N_HEADS * HEAD_DIM
    nk = n_grp * n_kv * HEAD_DIM
    grp_heads = N_HEADS // n_kv
    lay = _ProjLayout()
    oq = lay.add_output(n_grp * N_HEADS * LANES, F32)
    okv = lay.add_output(2 * nk, F32)
    for g in range(n_grp):
        for h in range(N_HEADS):
            qc = _cols((g * N_HEADS + h) * HEAD_DIM, HEAD_DIM)
            odd = (h // grp_heads) % 2
            lay.add_chunk(_zeros(HEAD_DIM) + qc if odd else qc + _zeros(HEAD_DIM), 1,
                          [(oq, (g * N_HEADS + h) * LANES)])
    gk = n_kv * HEAD_DIM
    for g in range(n_grp):
        for c in range(gk // LANES):
            lay.add_chunk(_cols(nq + g * gk + c * LANES, LANES), 2, [(okv, g * 2 * gk + c * LANES)])
        for c in range(gk // LANES):
            lay.add_chunk(_cols(nq + nk + g * gk + c * LANES, LANES), 0, [(okv, g * 2 * gk + gk + c * LANES)])
    return lay


def _diff_finalize(acc0, l0, acc1, l1, lam_ref, subln_ref, lam_init):
    lp = lam_ref[...]
    lam = (jnp.exp(jnp.sum(lp[0:1] * lp[1:2], axis=-1, keepdims=True))
           - jnp.exp(jnp.sum(lp[2:3] * lp[3:4], axis=-1, keepdims=True)) + lam_init)
    o = acc0 / l0 - lam * (acc1 / l1)
    o = o * lax.rsqrt(jnp.mean(o * o, axis=-1, keepdims=True) + RMS_EPS) * subln_ref[...]
    return o * (1.0 - lam_init)


def _a_prompt_kernel(q_ref, k_ref, vt_ref, toep_ref, lam_ref, subln_ref, o_ref,
                     qm_scr, ta_scr, tb_scr, p_scr, cm_scr, m_scr, acc_scr, *, tq, tk, lam_init, rb):
    qi = pl.program_id(2)
    low = _low_half()
    nh = 8
    w = nh * tq
    for g in range(4):
        qg = q_ref[:, g * LANES:(g + 1) * LANES]
        qm_scr[(2 * g) * tq:(2 * g + 1) * tq, :] = jnp.where(low, qg, jnp.zeros_like(qg))
        qm_scr[(2 * g + 1) * tq:(2 * g + 2) * tq, :] = jnp.where(low, jnp.zeros_like(qg), qg)
    m_scr[...] = jnp.full(m_scr.shape, NEG, F32)
    acc_scr[...] = jnp.zeros(acc_scr.shape, F32)
    n_chunks = (qi * tq + tq - 1) // tk + 1
    last = k_ref.shape[0] // tk - 1

    def stage_a(kj, t_ref):
        k0 = pl.multiple_of(jnp.minimum(kj, last) * tk, tk)
        s = _dot_nt(k_ref[pl.ds(k0, tk), :], qm_scr[...])
        macc = jnp.full((SUBLANES, w), NEG, F32)
        for u in range(tk // BAND):
            idx = [jnp.clip((qi * (tq // BAND) + r) - (kj * (tk // BAND) + u) + 1, 0, N_TOEP + 1)
                   for r in range(tq // BAND)]
            for b0 in range(0, BAND, rb):
                bias = jnp.concatenate([toep_ref[idx[r], h, b0:b0 + rb, :] for h in range(nh)
                                        for r in range(tq // BAND)], axis=1)
                r0 = u * BAND + b0
                t = s[r0:r0 + rb, :] + bias
                t_ref[r0:r0 + rb, :] = t
                for g8 in range(rb // SUBLANES):
                    macc = jnp.maximum(macc, t[g8 * SUBLANES:(g8 + 1) * SUBLANES, :])
        return jnp.max(macc, axis=0, keepdims=True)

    def stage_b(kj, t_ref, m_new, alpha):
        k0 = pl.multiple_of(jnp.minimum(kj, last) * tk, tk)
        vtc = jnp.concatenate([vt_ref[:, pl.ds(k0, tk)], jnp.ones((ONES_ROWS, tk), BF16)], axis=0)
        mb = jnp.broadcast_to(m_new, (SUBLANES, w))
        for r0 in range(0, tk, rb):
            t = t_ref[r0:r0 + rb, :]
            ps = []
            for g8 in range(rb // SUBLANES):
                p8 = jnp.exp2(t[g8 * SUBLANES:(g8 + 1) * SUBLANES, :] - mb)
                ps.append(p8)
            p_scr[r0:r0 + rb, :] = jnp.concatenate(ps, axis=0).astype(BF16)
        acc_scr[...] = alpha * acc_scr[...] + _dot(vtc, p_scr[...])

    cm_scr[...] = stage_a(0, ta_scr)

    def body(j, carry):
        cmax1 = stage_a(2 * j + 1, tb_scr)
        m_prev = m_scr[...]
        m0 = jnp.maximum(m_prev, cm_scr[...])
        stage_b(2 * j, ta_scr, m0, jnp.exp2(m_prev - m0))
        cm_scr[...] = stage_a(2 * j + 2, ta_scr)
        m1 = jnp.maximum(m0, cmax1)
        stage_b(2 * j + 1, tb_scr, m1, jnp.exp2(m0 - m1))
        m_scr[...] = m1
        return carry

    lax.fori_loop(0, (n_chunks + 1) // 2, body, 0)
    lp = lam_ref[...]
    lam = (jnp.exp(jnp.sum(lp[0:1] * lp[1:2], axis=-1, keepdims=True))
           - jnp.exp(jnp.sum(lp[2:3] * lp[3:4], axis=-1, keepdims=True)) + lam_init)
    on = acc_scr[0:LANES, :] / acc_scr[LANES:LANES + 1, :]
    for g in range(4):
        o = on[:, (2 * g) * tq:(2 * g + 1) * tq] - lam * on[:, (2 * g + 1) * tq:(2 * g + 2) * tq]
        o = o * lax.rsqrt(jnp.mean(o * o, axis=0, keepdims=True) + RMS_EPS) * subln_ref[...]
        o = o * (1.0 - lam_init)
        o_ref[:, g * LANES:(g + 1) * LANES] = o.T.astype(o_ref.dtype)


def _a_prompt(q, k, vt, toep_t, lam_p, subln, lam_init, tq=512, tk=512, rb=32):
    b, t, _ = q.shape
    return pl.pallas_call(
        functools.partial(_a_prompt_kernel, tq=tq, tk=tk, lam_init=lam_init, rb=rb),
        grid=(b, A_KV, t // tq),
        in_specs=[pl.BlockSpec((None, tq, 512), lambda bi, kv, qi: (bi, qi, kv)),
                  pl.BlockSpec((None, t, LANES), lambda bi, kv, qi: (bi, 0, kv), pipeline_mode=pl.Buffered(1)),
                  pl.BlockSpec((None, LANES, t), lambda bi, kv, qi: (bi, kv, 0), pipeline_mode=pl.Buffered(1)),
                  pl.BlockSpec((TOEP_TILES, 8, BAND, BAND), lambda bi, kv, qi: (0, kv, 0, 0),
                               pipeline_mode=pl.Buffered(1)),
                  pl.BlockSpec((4, HEAD_DIM), lambda bi, kv, qi: (0, 0)),
                  pl.BlockSpec((LANES, 1), lambda bi, kv, qi: (0, 0))],
        out_specs=pl.BlockSpec((None, tq, 512), lambda bi, kv, qi: (bi, qi, kv)),
        out_shape=jax.ShapeDtypeStruct((b, t, 1024), BF16),
        scratch_shapes=[pltpu.VMEM((8 * tq, LANES), BF16),
                        pltpu.VMEM((tk, 8 * tq), F32),
                        pltpu.VMEM((tk, 8 * tq), F32),
                        pltpu.VMEM((tk, 8 * tq), BF16),
                        pltpu.VMEM((1, 8 * tq), F32),
                        pltpu.VMEM((1, 8 * tq), F32),
                        pltpu.VMEM((LANES + ONES_ROWS, 8 * tq), F32)],
        compiler_params=_cparams(("parallel", "parallel", "arbitrary")),
        name="a_prompt",
    )(q, k, vt, toep_t, lam_p, subln.reshape(LANES, 1))


def _a_sample_kernel(pt_ref, q_ref, new_ref, bp_ref, bc_ref, bn_ref, lam_ref, subln_ref, *rest,
                     pages, tq, lam_init):
    page_refs = rest[:pages]
    o_ref = rest[pages]
    qm_scr, m_scr, l_scr, acc_scr = rest[pages + 1:]
    c = pl.program_id(1)
    last = pl.num_programs(1) - 1
    low = _low_half()
    rows = 8 * tq

    @pl.when(c == 0)
    def _():
        for kv in range(A_KV):
            for g in range(4):
                qg = q_ref[:, (kv * 4 + g) * LANES:(kv * 4 + g + 1) * LANES]
                qm_scr[kv, (2 * g) * tq:(2 * g + 1) * tq, :] = jnp.where(low, qg, 0.0)
                qm_scr[kv, (2 * g + 1) * tq:(2 * g + 2) * tq, :] = jnp.where(low, 0.0, qg)
        m_scr[...] = jnp.full(m_scr.shape, NEG, F32)
        l_scr[...] = jnp.zeros(l_scr.shape, F32)
        acc_scr[...] = jnp.zeros(acc_scr.shape, F32)

    n_slot = 2 * A_KV
    is_last = c == last

    def scores(kv):
        qm = qm_scr[kv].astype(BF16)
        keys = jnp.concatenate([pr[pl.ds(kv, PAGE_SIZE, stride=n_slot), :] for pr in page_refs], axis=0)
        return _dot_nt(qm, keys.astype(BF16)) + jnp.where(is_last, bp_ref[kv], bc_ref[kv])

    def values(kv):
        return jnp.concatenate([pr[pl.ds(A_KV + kv, PAGE_SIZE, stride=n_slot), :] for pr in page_refs], axis=0).astype(BF16)

    s = jnp.concatenate([scores(kv) for kv in range(A_KV)], axis=0)
    m_prev = m_scr[...]
    m_new = jnp.maximum(m_prev, jnp.max(s, axis=-1, keepdims=True))
    alpha = jnp.exp(m_prev - m_new)
    p = jnp.exp(s - m_new)
    l_scr[...] = alpha * l_scr[...] + jnp.sum(p, axis=-1, keepdims=True)
    m_scr[...] = m_new
    pb = p.astype(BF16)
    pv = jnp.concatenate([_dot(pb[kv * rows:(kv + 1) * rows], values(kv)) for kv in range(A_KV)], axis=0)
    acc_scr[...] = alpha * acc_scr[...] + pv

    @pl.when(is_last)
    def _():
        pad = jnp.zeros((LANES - tq, LANES), F32)
        for kv in range(A_KV):
            sl = slice(kv * rows, (kv + 1) * rows)
            qm = qm_scr[kv].astype(BF16)
            kn = jnp.concatenate([new_ref[:, kv * LANES:(kv + 1) * LANES], pad], axis=0)
            vn = jnp.concatenate([new_ref[:, (A_KV + kv) * LANES:(A_KV + kv + 1) * LANES], pad], axis=0)
            s = _dot_nt(qm, kn.astype(BF16)) + bn_ref[kv]
            m_prev = m_scr[sl]
            m_new = jnp.maximum(m_prev, jnp.max(s, axis=-1, keepdims=True))
            alpha = jnp.exp(m_prev - m_new)
            p = jnp.exp(s - m_new)
            l = alpha * l_scr[sl] + jnp.sum(p, axis=-1, keepdims=True)
            acc = alpha * acc_scr[sl] + _dot(p.astype(BF16), vn.astype(BF16))
            for g in range(4):
                r0, r1 = (2 * g) * tq, (2 * g + 1) * tq
                o = _diff_finalize(acc[r0:r0 + tq], l[r0:r0 + tq], acc[r1:r1 + tq], l[r1:r1 + tq],
                                   lam_ref, subln_ref, lam_init)
                o_ref[:, (kv * 4 + g) * LANES:(kv * 4 + g + 1) * LANES] = o.astype(o_ref.dtype)


def _a_sample(q, new_kv, cache, layer, page_table, bias_past, bias_const, bias_new, lam_p, subln, lam_init,
              pages=64):
    db, tq, _ = q.shape
    n_pages = page_table.shape[1]
    n_chunks = n_pages // pages
    assert n_chunks == 1 or pages * PAGE_SIZE + 1 >= THR[-1]
    cache2 = cache.reshape(cache.shape[0], cache.shape[1], PAGE_SIZE * 2 * A_KV, LANES)
    rows = 8 * tq

    def page_spec(j):
        return pl.BlockSpec((None, None, PAGE_SIZE * 2 * A_KV, LANES),
                            lambda bi, c, pt: (layer, pt[bi * n_pages + c * pages + j], 0, 0))

    grid_spec = pltpu.PrefetchScalarGridSpec(
        num_scalar_prefetch=1,
        grid=(db, n_chunks),
        in_specs=[pl.BlockSpec((None, tq, 1024), lambda bi, c, pt: (bi, 0, 0)),
                  pl.BlockSpec((None, tq, 512), lambda bi, c, pt: (bi, 0, 0)),
                  pl.BlockSpec((A_KV, rows, pages * PAGE_SIZE), lambda bi, c, pt: (0, 0, n_chunks - 1)),
                  pl.BlockSpec((A_KV, rows, 1), lambda bi, c, pt: (0, 0, 0)),
                  pl.BlockSpec((A_KV, rows, LANES), lambda bi, c, pt: (0, 0, 0)),
                  pl.BlockSpec((4, HEAD_DIM), lambda bi, c, pt: (0, 0)),
                  pl.BlockSpec((1, LANES), lambda bi, c, pt: (0, 0))]
                 + [page_spec(j) for j in range(pages)],
        out_specs=pl.BlockSpec((None, tq, 1024), lambda bi, c, pt: (bi, 0, 0)),
        scratch_shapes=[pltpu.VMEM((A_KV, rows, LANES), F32),
                        pltpu.VMEM((A_KV * rows, 1), F32),
                        pltpu.VMEM((A_KV * rows, 1), F32),
                        pltpu.VMEM((A_KV * rows, LANES), F32)])
    return pl.pallas_call(
        functools.partial(_a_sample_kernel, pages=pages, tq=tq, lam_init=lam_init),
        grid_spec=grid_spec,
        out_shape=jax.ShapeDtypeStruct((db, tq, 1024), F32),
        compiler_params=_cparams(("parallel", "arbitrary")),
        name="a_sample",
    )(page_table.reshape(-1), q, new_kv, bias_past, bias_const, bias_new, lam_p, subln.reshape(1, LANES),
      *([cache2] * pages))


def _band_kernel(*refs, n_kv, use_sink, want_lse, n_sub):
    q_ref, kp_ref, kc_ref, vp_ref, vc_ref, bias_ref = refs[:6]
    pos = 6
    sink_ref = None
    if use_sink:
        sink_ref = refs[pos]
        pos += 1
    o_ref = refs[pos]
    lse_ref = refs[pos + 1] if want_lse else None
    qm_scr = refs[-1]
    i = pl.program_id(2)
    grp = N_HEADS // n_kv
    low = _low_half()
    key = lax.broadcasted_iota(jnp.int32, (2 * BAND, 1), 0)
    ones = jnp.ones((ONES_ROWS, 2 * BAND), BF16)
    for j in range(n_sub):
        rows = slice(j * BAND, (j + 1) * BAND)
        prv = slice((j - 1) * BAND, j * BAND)

        def kv_pair(pref, cref, kv):
            lanes = slice(kv * LANES, (kv + 1) * LANES)
            before = pref[:, lanes] if j == 0 else cref[prv, lanes]
            return jnp.concatenate([before, cref[rows, lanes]], axis=0)

        for c in range(N_HEADS // 2):
            qp = q_ref[rows, c * LANES:(c + 1) * LANES]
            zero = jnp.zeros_like(qp)
            qm_scr[(2 * c) * BAND:(2 * c + 1) * BAND, :] = jnp.where(low, qp, zero)
            qm_scr[(2 * c + 1) * BAND:(2 * c + 2) * BAND, :] = jnp.where(low, zero, qp)
        s_parts = [_dot_nt(kv_pair(kp_ref, kc_ref, kv), qm_scr[kv * grp * BAND:(kv + 1) * grp * BAND, :])
                   for kv in range(n_kv)]
        s = jnp.concatenate(s_parts, axis=1) + bias_ref[...]
        if j == 0:
            s = jnp.where((key >= BAND) | (i > 0), s, NEG)
        m = jnp.max(s, axis=0, keepdims=True)
        if use_sink:
            m = jnp.maximum(m, sink_ref[...])
        p = jnp.exp(s - m).astype(BF16)
        acc_parts = []
        for kv in range(n_kv):
            vt = kv_pair(vp_ref, vc_ref, kv).astype(F32).T[0:HEAD_DIM, :].astype(BF16)
            acc_parts.append(_dot(jnp.concatenate([vt, ones], axis=0), p[:, kv * grp * BAND:(kv + 1) * grp * BAND]))
        acc = jnp.concatenate(acc_parts, axis=1)
        l = acc[HEAD_DIM:HEAD_DIM + 1, :]
        if use_sink:
            l = l + jnp.exp(sink_ref[...] - m)
        on = acc[0:HEAD_DIM, :] / l
        for c in range(N_HEADS // 2):
            pair = jnp.concatenate([on[:, (2 * c) * BAND:(2 * c + 1) * BAND], on[:, (2 * c + 1) * BAND:(2 * c + 2) * BAND]], axis=0)
            o_ref[rows, c * LANES:(c + 1) * LANES] = pair.T.astype(o_ref.dtype)
        if want_lse:
            lse = m + jnp.log(l)
            lrows = jnp.concatenate([lse[:, h * BAND:(h + 1) * BAND] for h in range(N_HEADS)]
                                    + [jnp.zeros((LANES - N_HEADS, BAND), F32)], axis=0)
            lse_ref[rows, :] = lrows.T


def _band_prompt(q, kd, vd, bias, sinks, n_kv, want_lse, out_dtype):
    b, dil, ts, _ = q.shape
    n_sub = 2 if ts % (2 * BAND) == 0 else 1
    blk = n_sub * BAND
    nb = ts // blk
    ck = n_kv * LANES
    cur = lambda bi, r, i: (bi, r, i, 0)
    prev = lambda bi, r, i: (bi, r, jnp.maximum(i * n_sub - 1, 0), 0)
    in_specs = [pl.BlockSpec((None, None, blk, 1024), cur),
                pl.BlockSpec((None, None, BAND, ck), prev),
                pl.BlockSpec((None, None, blk, ck), cur),
                pl.BlockSpec((None, None, BAND, ck), prev),
                pl.BlockSpec((None, None, blk, ck), cur),
                pl.BlockSpec((2 * BAND, N_HEADS * BAND), lambda bi, r, i: (0, 0))]
    args = [q, kd, kd, vd, vd, bias]
    if sinks is not None:
        in_specs.append(pl.BlockSpec((1, N_HEADS * BAND), lambda bi, r, i: (0, 0)))
        args.append(jnp.repeat(sinks, BAND).reshape(1, N_HEADS * BAND))
    out_specs = [pl.BlockSpec((None, None, blk, 1024), cur)]
    out_shape = [jax.ShapeDtypeStruct((b, dil, ts, 1024), out_dtype)]
    if want_lse:
        out_specs.append(pl.BlockSpec((None, None, blk, LANES), cur))
        out_shape.append(jax.ShapeDtypeStruct((b, dil, ts, LANES), F32))
    return pl.pallas_call(
        functools.partial(_band_kernel, n_kv=n_kv, use_sink=sinks is not None, want_lse=want_lse, n_sub=n_sub),
        grid=(b, dil, nb),
        in_specs=in_specs,
        out_specs=out_specs,
        out_shape=out_shape,
        scratch_shapes=[pltpu.VMEM((N_HEADS * BAND, LANES), BF16)],
        compiler_params=_cparams(("parallel", "parallel", "arbitrary")),
        name=f"band_d{dil}",
    )(*args)


def _decode_kernel(*refs, n_grp, n_kv, past_lens, use_sink, tq):
    q_ref, new_ref = refs[:2]
    pos = 2
    st_refs = refs[pos:pos + n_grp]; pos += n_grp
    bp_refs = refs[pos:pos + n_grp]; pos += n_grp
    bn_refs = refs[pos:pos + n_grp]; pos += n_grp
    sink_ref = None
    if use_sink:
        sink_ref = refs[pos]; pos += 1
    o_ref = refs[pos]; pos += 1
    nst_refs = refs[pos:pos + n_grp]
    width = 2 * n_kv * HEAD_DIM
    n_chunks = n_kv // 2
    heads_per_chunk = N_HEADS // n_chunks
    grp_heads = N_HEADS // n_kv
    low = _low_half()
    pad = jnp.zeros((LANES - tq, LANES), F32)
    for c in range(n_chunks):
        parts = []
        for g in range(n_grp):
            qs = jnp.concatenate(
                [q_ref[:, (g * N_HEADS + c * heads_per_chunk + hh) * LANES:(g * N_HEADS + c * heads_per_chunk + hh + 1) * LANES]
                 for hh in range(heads_per_chunk)], axis=0).astype(BF16)
            st = st_refs[g]
            kt = st[0, c * LANES:(c + 1) * LANES, :].astype(BF16)
            vt = st[1, c * LANES:(c + 1) * LANES, :].astype(BF16)
            kn = jnp.concatenate([new_ref[:, g * width + c * LANES:g * width + (c + 1) * LANES], pad], axis=0).astype(BF16)
            vn = jnp.concatenate([new_ref[:, g * width + n_kv * HEAD_DIM + c * LANES:g * width + n_kv * HEAD_DIM + (c + 1) * LANES], pad], axis=0).astype(BF16)
            sp = _dot(qs, kt) + bp_refs[g][c]
            sn = _dot_nt(qs, kn) + bn_refs[g][c]
            m = jnp.maximum(jnp.max(sp, axis=-1, keepdims=True), jnp.max(sn, axis=-1, keepdims=True))
            if use_sink:
                m = jnp.maximum(m, sink_ref[c])
            pp = jnp.exp(sp - m)
            pn = jnp.exp(sn - m)
            l = jnp.sum(pp, axis=-1, keepdims=True) + jnp.sum(pn, axis=-1, keepdims=True)
            if use_sink:
                l = l + jnp.exp(sink_ref[c] - m)
            acc = _dot_nt(pp.astype(BF16), vt) + _dot(pn.astype(BF16), vn)
            parts.append((m, l, acc))
        m_all = parts[0][0]
        for m, _, _ in parts[1:]:
            m_all = jnp.maximum(m_all, m)
        num = 0.0
        den = 0.0
        for m, l, acc in parts:
            w = jnp.exp(m - m_all)
            num = num + acc * w
            den = den + l * w
        o = num / den
        for hh in range(heads_per_chunk):
            h = c * heads_per_chunk + hh
            odd = (h // grp_heads) % 2
            oh = o[hh * tq:(hh + 1) * tq, :]
            o_ref[:, h * LANES:(h + 1) * LANES] = (jnp.where(low, 0.0, oh) if odd else jnp.where(low, oh, 0.0)).astype(o_ref.dtype)
    n_rows = n_kv * HEAD_DIM
    for g in range(n_grp):
        lp = past_lens[g]
        for kv_slot in range(2):
            c0 = g * width + kv_slot * n_rows
            new_t = jnp.concatenate([new_ref[:, c0:c0 + n_rows], jnp.zeros((LANES - tq, n_rows), F32)], axis=0).T
            for r0 in range(0, n_rows, HEAD_DIM):
                if lp > tq:
                    nst_refs[g][kv_slot, r0:r0 + HEAD_DIM, 0:lp - tq] = st_refs[g][kv_slot, r0:r0 + HEAD_DIM, tq:lp]
                nst_refs[g][kv_slot, r0:r0 + HEAD_DIM, lp - tq:lp] = new_t[r0:r0 + HEAD_DIM, 0:tq]


def _decode(q, new_kv, states, layer, bias_past, bias_new, sinks_rows, n_kv):
    db, tq, _ = q.shape
    n_grp = len(states)
    width = 2 * n_kv * HEAD_DIM
    n_chunks = n_kv // 2
    rows = tq * N_HEADS // n_chunks
    past_lens = tuple(s.shape[2] for s in states)
    st_t = [jnp.transpose(s, (0, 1, 3, 4, 5, 2)).reshape(s.shape[0], db, 2, n_kv * HEAD_DIM, s.shape[2])
            for s in states]
    in_specs = [pl.BlockSpec((None, tq, q.shape[2]), lambda bi: (bi, 0, 0)),
                pl.BlockSpec((None, tq, n_grp * width), lambda bi: (bi, 0, 0))]
    in_specs += [pl.BlockSpec((None, None, 2, n_kv * HEAD_DIM, lp), lambda bi: (layer, bi, 0, 0, 0))
                 for lp in past_lens]
    in_specs += [pl.BlockSpec((n_chunks, rows, lp), lambda bi: (0, 0, 0)) for lp in past_lens]
    in_specs += [pl.BlockSpec((n_chunks, rows, LANES), lambda bi: (0, 0, 0)) for _ in past_lens]
    args = [q, new_kv] + st_t + list(bias_past) + list(bias_new)
    if sinks_rows is not None:
        in_specs.append(pl.BlockSpec((n_chunks, rows, 1), lambda bi: (0, 0, 0)))
        args.append(sinks_rows)
    res = pl.pallas_call(
        functools.partial(_decode_kernel, n_grp=n_grp, n_kv=n_kv, past_lens=past_lens,
                          use_sink=sinks_rows is not None, tq=tq),
        grid=(db,),
        in_specs=in_specs,
        out_specs=[pl.BlockSpec((None, tq, N_HEADS * LANES), lambda bi: (bi, 0, 0))]
        + [pl.BlockSpec((None, 2, n_kv * HEAD_DIM, lp), lambda bi: (bi, 0, 0, 0)) for lp in past_lens],
        out_shape=[jax.ShapeDtypeStruct((db, tq, N_HEADS * LANES), F32)]
        + [jax.ShapeDtypeStruct((db, 2, n_kv * HEAD_DIM, lp), F32) for lp in past_lens],
        compiler_params=_cparams(("parallel",)),
        name=f"decode_kv{n_kv}",
    )(*args)
    new_states = [jnp.transpose(r.reshape(db, 2, n_kv, HEAD_DIM, lp), (0, 4, 1, 2, 3))
                  for r, lp in zip(res[1:], past_lens)]
    return res[0], new_states


def _ffn_kernel(*refs, n_attn, sample, tiles_per_seq, row_tile, ff_chunk, tq, dils):
    x_ref = refs[0]
    pos = 1
    if n_attn == 1:
        o_in = refs[pos]; pos += 1
    else:
        o_refs = refs[pos:pos + n_attn]; pos += n_attn
        lse_refs = refs[pos:pos + n_attn]; pos += n_attn
    wo_ref, g_ref, win_ref, cw_ref, cb_ref, wout_ref = refs[pos:pos + 6]; pos += 6
    if sample:
        fix1_ref, fix2_ref = refs[pos:pos + 2]; pos += 2
    xo_ref, gate_ref = refs[pos:pos + 2]; pos += 2
    gp_scr, halo_scr = refs[pos:pos + 2]; pos += 2
    il_scr = refs[pos:]
    i = pl.program_id(0)
    low = _low_half()

    @pl.when(i == 0)
    def _():
        halo_scr[...] = jnp.zeros(halo_scr.shape, F32)

    if n_attn == 1:
        ob = o_in[...]
    else:
        def natural(ref4, g, col0):
            d = dils[g]
            if d == 1:
                return ref4[0, :, col0:col0 + LANES]
            scr = il_scr[sum(1 for x in dils[:g] if x > 1)]
            for r in range(d):
                scr[pl.ds(r, row_tile // d, stride=d), :] = ref4[r, :, col0:col0 + LANES]
            return scr[...]

        lses = [natural(lse_refs[g], g, 0) for g in range(n_attn)]
        m_all = lses[0]
        for z in lses[1:]:
            m_all = jnp.maximum(m_all, z)
        es = [jnp.exp(z - m_all) for z in lses]
        den = es[0]
        for e in es[1:]:
            den = den + e
        ws = [e / den for e in es]
        chunks = []
        for c in range(1024 // LANES):
            acc = None
            for g in range(n_attn):
                wsel = jnp.where(low, ws[g][:, 2 * c:2 * c + 1], ws[g][:, 2 * c + 1:2 * c + 2])
                term = wsel * natural(o_refs[g], g, c * LANES)
                acc = term if acc is None else acc + term
            chunks.append(acc.astype(BF16))
        ob = jnp.concatenate(chunks, axis=1)
    x1 = x_ref[...] + _dot(ob, wo_ref[...])
    h = x1 * lax.rsqrt(jnp.mean(x1 * x1, axis=-1, keepdims=True) + RMS_EPS) * g_ref[...]
    hb = h.astype(BF16)

    if sample:
        t_in_seq = lax.broadcasted_iota(jnp.int32, (row_tile, 1), 0) & (tq - 1)
    y = jnp.zeros((row_tile, D_MODEL), F32)
    for f in range(D_FF // ff_chunk):
        f0 = f * ff_chunk
        gate = _dot(hb, win_ref[:, f0:f0 + ff_chunk])
        up = _dot(hb, win_ref[:, D_FF + f0:D_FF + f0 + ff_chunk])
        if sample:
            gp_scr[0:SUBLANES, :] = jnp.zeros((SUBLANES, ff_chunk), F32)
        else:
            first = (i % tiles_per_seq) == 0
            gp_scr[0:SUBLANES, :] = jnp.where(first, 0.0, halo_scr[f])
        gp_scr[SUBLANES:SUBLANES + row_tile, :] = gate
        g2 = gp_scr[SUBLANES - 2:SUBLANES - 2 + row_tile, :]
        g1 = gp_scr[SUBLANES - 1:SUBLANES - 1 + row_tile, :]
        if sample:
            g2 = jnp.where(t_in_seq >= 2, g2, fix2_ref[:, f0:f0 + ff_chunk])
            g1 = jnp.where(t_in_seq >= 1, g1, fix1_ref[:, f0:f0 + ff_chunk])
        gc = (cb_ref[:, f0:f0 + ff_chunk] + cw_ref[0:1, f0:f0 + ff_chunk] * g2
              + cw_ref[1:2, f0:f0 + ff_chunk] * g1 + cw_ref[2:3, f0:f0 + ff_chunk] * gate)
        act = gc * (1.0 / (1.0 + jnp.exp(-gc))) * up
        y = y + _dot(act.astype(BF16), wout_ref[f0:f0 + ff_chunk, :])
        if sample:
            gate_ref[:, f0:f0 + ff_chunk] = gate
        else:
            tail = gate[row_tile - SUBLANES:row_tile, :]
            halo_scr[f] = tail
            gate_ref[:, f0:f0 + ff_chunk] = tail
    xo_ref[...] = x1 + y


def _ffn(x, attn, lses, w_o, gain, w_in, conv_w, conv_b, w_out, fix, seq_len, row_tile=512,
         ff_chunk=1408):
    rows = x.shape[0]
    sample = fix is not None
    n_attn = len(attn)
    row_spec = lambda w: pl.BlockSpec((row_tile, w), lambda i: (i, 0))
    full = lambda a: pl.BlockSpec(a.shape, lambda i: (0,) * a.ndim, pipeline_mode=pl.Buffered(1))
    tiles = seq_len // row_tile
    res_spec = lambda a: pl.BlockSpec((None, a.shape[1], row_tile // a.shape[1], a.shape[3]),
                                      lambda i: (i // tiles, 0, i % tiles, 0))
    dils = (1,)
    if n_attn == 1:
        in_specs = [row_spec(D_MODEL), row_spec(attn[0].shape[1])]
        args = [x] + list(attn)
    else:
        dils = tuple(a.shape[1] for a in attn)
        in_specs = [row_spec(D_MODEL)] + [res_spec(a) for a in attn] + [res_spec(a) for a in lses]
        args = [x] + list(attn) + list(lses)
    consts = [w_o.astype(BF16), gain.reshape(1, D_MODEL), w_in.astype(BF16), conv_w,
              conv_b.reshape(1, D_FF), w_out.astype(BF16)]
    in_specs += [full(a) for a in consts]
    args += consts
    if sample:
        in_specs += [row_spec(D_FF), row_spec(D_FF)]
        args += list(fix)
        gate_spec = row_spec(D_FF)
        gate_shape = jax.ShapeDtypeStruct((rows, D_FF), F32)
        tiles_per_seq = 1
        tq = seq_len
    else:
        gate_spec = pl.BlockSpec((None, SUBLANES, D_FF), lambda i: (i, 0, 0))
        gate_shape = jax.ShapeDtypeStruct((rows // row_tile, SUBLANES, D_FF), F32)
        tiles_per_seq = seq_len // row_tile
        tq = 1
    return pl.pallas_call(
        functools.partial(_ffn_kernel, n_attn=n_attn, sample=sample, tiles_per_seq=tiles_per_seq,
                          row_tile=row_tile, ff_chunk=ff_chunk, tq=tq, dils=dils),
        grid=(rows // row_tile,),
        in_specs=in_specs,
        out_specs=[row_spec(D_MODEL), gate_spec],
        out_shape=[jax.ShapeDtypeStruct((rows, D_MODEL), F32), gate_shape],
        scratch_shapes=[pltpu.VMEM((SUBLANES + row_tile, ff_chunk), F32),
                        pltpu.VMEM((D_FF // ff_chunk, SUBLANES, ff_chunk), F32)]
        + [pltpu.VMEM((row_tile, LANES), F32) for d in dils if d > 1],
        compiler_params=_cparams(("arbitrary",)),
        name="ffn_sample" if sample else "ffn_prompt",
    )(*args)


def _expand_wo_sample(w_o, n_kv):
    grp_heads = N_HEADS // n_kv
    src = []
    for h in range(N_HEADS):
        rows = _cols(h * HEAD_DIM, HEAD_DIM)
        src += (_zeros(HEAD_DIM) + rows) if (h // grp_heads) % 2 else (rows + _zeros(HEAD_DIM))
    return _take_cols(w_o.T, src).T


def kernel(x_prompt, x_sample, cache_kv_a, state_kv_b, state_kv_c1, state_kv_c2, state_kv_c3, state_conv_ffn, page_table, rel_bias_table, norm_mix, norm_ffn, w_in_a, q_norm_a, k_norm_a, lambda_a, subln_a, w_o_a, w_in_b, q_norm_b, k_norm_b, sinks_b, w_o_b, w_in_c, q_norm_c, k_norm_c, w_o_c, w_ffn_in, conv_ffn_w, conv_ffn_b, w_ffn_out):
    bp, tp, _ = x_prompt.shape
    bs, ts, _ = x_sample.shape
    depth = norm_mix.shape[0]
    past_len = page_table.shape[1] * PAGE_SIZE
    xp = x_prompt.reshape(bp * tp, D_MODEL)
    xs = x_sample.reshape(bs * ts, D_MODEL)
    state_c = (state_kv_c1, state_kv_c2, state_kv_c3)

    toep = _toep_table(rel_bias_table)
    band_bias = {d: _band_bias_table(rel_bias_table, d) for d in (1, 4, 16)}
    a_cols = [[kv * 8 + j for j in range(8)] for kv in range(A_KV)]
    a_bias_past, a_bias_new = _decode_bias_table(rel_bias_table, a_cols, past_len, past_len + ts, 1, ts)
    a_bias_far = jnp.repeat(rel_bias_table[N_BUCKETS - 1][jnp.asarray(a_cols)], ts, axis=1)[..., None]
    b_cols = [list(range(N_HEADS))]
    b_len = state_kv_b.shape[2]
    b_bias_past, b_bias_new = _decode_bias_table(rel_bias_table, b_cols, b_len, B_WINDOW, 1, ts)
    c_cols = [list(range(8)), list(range(8, 16))]
    c_bias = [_decode_bias_table(rel_bias_table, c_cols, state_c[g].shape[2], w, d, ts)
              for g, (w, d) in enumerate(C_PAIRS)]

    kv_a_p, kv_a_s, kv_b_p, kv_b_s = [], [], [], []
    kv_c_p = [[] for _ in C_PAIRS]
    kv_c_s = [[] for _ in C_PAIRS]
    conv_p, conv_s = [], []
    for i in range(depth):
        n = i // N_MIXERS
        if i % N_MIXERS == 0:
            lam_init = 0.8 - 0.6 * math.exp(-0.3 * i)
            q, kvn, k, vt = _run_proj(xp, norm_mix[i], w_in_a[n], q_norm_a[n], k_norm_a[n], _layout_a(True), 512,
                                      q_scale=LOG2E, t_cols=_cols(1280, 256), seq_len=tp)
            op = _a_prompt(q.reshape(bp, tp, 1024), k.reshape(bp, tp, 256), vt, toep, lambda_a[n], subln_a[n],
                           lam_init)
            attn_p, lse_p = [op.reshape(bp * tp, 1024)], None
            kv_a_p.append(kvn.reshape(bp, tp, 2, A_KV, 2 * HEAD_DIM))
            qs, kvs = _run_proj(xs, norm_mix[i], w_in_a[n], q_norm_a[n], k_norm_a[n], _layout_a(False), 512)
            os_ = _a_sample(qs.reshape(bs, ts, 1024), kvs.reshape(bs, ts, 512), cache_kv_a, n, page_table,
                            a_bias_past, a_bias_far, a_bias_new, lambda_a[n], subln_a[n], lam_init)
            attn_s = [os_.reshape(bs * ts, 1024).astype(BF16)]
            kv_a_s.append(kvs.reshape(bs, ts, 2, A_KV, 2 * HEAD_DIM))
            w_o_p = w_o_a[n]
            w_o_s = w_o_a[n]
        elif i % N_MIXERS == 1:
            kvn, q, kd, vd = _run_proj(xp, norm_mix[i], w_in_b[n], q_norm_b[n], k_norm_b[n],
                                       _layout_band_prompt((1,), B_KV), 512)
            op = _band_prompt(q.reshape(bp, 1, tp, 1024), kd.reshape(bp, 1, tp, B_KV * LANES),
                              vd.reshape(bp, 1, tp, B_KV * LANES), band_bias[1], sinks_b[n], B_KV, False, BF16)[0]
            attn_p, lse_p = [op.reshape(bp * tp, 1024)], None
            wb = min(B_WINDOW, tp)
            kv_b_p.append(kvn.reshape(bp, tp, -1)[:, tp - wb:].reshape(bp, wb, 2, B_KV, HEAD_DIM))
            qs, kvs = _run_proj(xs, norm_mix[i], w_in_b[n], q_norm_b[n], k_norm_b[n],
                                _layout_band_sample(1, B_KV), 512)
            sink_rows = jnp.repeat(sinks_b[n], ts).reshape(1, N_HEADS * ts, 1)
            os_, nst = _decode(qs.reshape(bs, ts, -1), kvs.reshape(bs, ts, -1), [state_kv_b], n,
                               [b_bias_past], [b_bias_new], sink_rows, B_KV)
            attn_s = [os_.reshape(bs * ts, N_HEADS * LANES).astype(BF16)]
            kv_b_s.append(nst[0])
            w_o_p = w_o_b[n]
            w_o_s = _expand_wo_sample(w_o_b[n], B_KV)
        else:
            c_dils = tuple(d for _, d in C_PAIRS)
            outs = _run_proj(xp, norm_mix[i], w_in_c[n], q_norm_c[n], k_norm_c[n],
                             _layout_band_prompt(c_dils, C_KV), 256, seq_len=tp)
            kvn = outs[0]
            attn_p, lse_p = [], []
            for g, (w, d) in enumerate(C_PAIRS):
                q, kd, vd = (a.reshape(bp, d, tp // d, a.shape[-1]) for a in outs[1 + 3 * g:4 + 3 * g])
                o_g, lse_g = _band_prompt(q, kd, vd, band_bias[d], None, C_KV, True, F32)
                attn_p.append(o_g)
                lse_p.append(lse_g)
            kvn = kvn.reshape(bp, tp, -1)
            gw = 2 * C_KV * HEAD_DIM
            for g, (w, d) in enumerate(C_PAIRS):
                wc = min(w, tp)
                kv_c_p[g].append(kvn[:, tp - wc:, g * gw:(g + 1) * gw].reshape(bp, wc, 2, C_KV, HEAD_DIM))
            qs, kvs = _run_proj(xs, norm_mix[i], w_in_c[n], q_norm_c[n], k_norm_c[n],
                                _layout_band_sample(3, C_KV), 256)
            os_, nst = _decode(qs.reshape(bs, ts, -1), kvs.reshape(bs, ts, -1), list(state_c), n,
                               [cb[0] for cb in c_bias], [cb[1] for cb in c_bias], None, C_KV)
            attn_s = [os_.reshape(bs * ts, N_HEADS * LANES).astype(BF16)]
            for g in range(3):
                kv_c_s[g].append(nst[g])
            w_o_p = w_o_c[n]
            w_o_s = _expand_wo_sample(w_o_c[n], C_KV)

        xp, gate_tail = _ffn(xp, attn_p, lse_p, w_o_p, norm_ffn[i], w_ffn_in[i], conv_ffn_w[i],
                             conv_ffn_b[i], w_ffn_out[i], None, tp)
        tiles = tp // 512
        conv_p.append(gate_tail.reshape(bp, tiles, SUBLANES, D_FF)[:, -1, SUBLANES - (CONV_W - 1):])
        hist = state_conv_ffn[i]
        zero = jnp.zeros((bs, ts - 1, D_FF), F32)
        fix1 = jnp.concatenate([hist[:, 1:2], zero], axis=1).reshape(bs * ts, D_FF)
        fix2 = jnp.concatenate([hist[:, 0:1], hist[:, 1:2], zero[:, 1:]], axis=1).reshape(bs * ts, D_FF)
        xs, gate_s = _ffn(xs, attn_s, None, w_o_s, norm_ffn[i], w_ffn_in[i], conv_ffn_w[i],
                          conv_ffn_b[i], w_ffn_out[i], (fix1, fix2), ts, row_tile=256)
        conv_s.append(gate_s.reshape(bs, ts, D_FF)[:, ts - (CONV_W - 1):])

    return (xp.reshape(bp, tp, D_MODEL), xs.reshape(bs, ts, D_MODEL),
            jnp.stack(kv_a_p), jnp.stack(kv_a_s),
            jnp.stack(kv_b_p), jnp.stack(kv_b_s),
            jnp.stack(kv_c_p[0]), jnp.stack(kv_c_s[0]),
            jnp.stack(kv_c_p[1]), jnp.stack(kv_c_s[1]),
            jnp.stack(kv_c_p[2]), jnp.stack(kv_c_s[2]),
            jnp.stack(conv_p), jnp.stack(conv_s))
```

```python
import functools
import math

import numpy as np
import jax
import jax.numpy as jnp
from jax import lax
from jax.experimental import pallas as pl
from jax.experimental.pallas import tpu as pltpu

F32 = jnp.float32
BF16 = jnp.bfloat16

D_MODEL = 1024
HEAD_DIM = 64
LANES = 128
SUBLANES = 8
RMS_EPS = 1e-6
NEG = -1e30
LOG2E = math.log2(math.e)
N_HEADS = 16
N_BUCKETS = 32
MAX_DISTANCE = 2048
PAGE_SIZE = 128
D_FF = 2816
CONV_W = 3
N_MIXERS = 3
A_KV = 2
B_KV = 2
C_KV = 4
B_WINDOW = 128
C_PAIRS = ((128, 1), (512, 4), (2048, 16))
BAND = 128
ONES_ROWS = 16
VMEM_LIMIT = 56 * 1024 * 1024


def _bucket_thresholds():
    n = np.arange(0, 1 << 15)
    x = np.log(np.maximum(n, 1) / (N_BUCKETS // 2)) / math.log(MAX_DISTANCE / (N_BUCKETS // 2))
    large = N_BUCKETS // 2 + (x * (N_BUCKETS - N_BUCKETS // 2)).astype(np.int64)
    bucket = np.where(n < N_BUCKETS // 2, n, np.minimum(large, N_BUCKETS - 1))
    return [int(np.argmax(bucket >= b)) for b in range(N_BUCKETS)]


THR = _bucket_thresholds()
N_TOEP = -(-(THR[-1] + BAND - 1) // BAND)
TOEP_TILES = N_TOEP + 2


def _cparams(sem, vmem=VMEM_LIMIT):
    return pltpu.CompilerParams(dimension_semantics=sem, vmem_limit_bytes=vmem)


def _dot(a, b):
    return jnp.dot(a, b, preferred_element_type=F32)


def _dot_nt(a, b):
    return lax.dot_general(a, b, (((1,), (1,)), ((), ())), preferred_element_type=F32)


def _low_half():
    return lax.broadcasted_iota(jnp.int32, (1, LANES), 1) < HEAD_DIM


def _bias_of(dist, col, tab_ref, lo, hi):
    b_lo = max(b for b in range(N_BUCKETS) if THR[b] <= max(lo, 0))
    b_hi = max(b for b in range(N_BUCKETS) if THR[b] <= max(hi, 0))
    val = jnp.full(dist.shape, tab_ref[b_hi, col], F32)
    for b in range(b_hi - 1, b_lo - 1, -1):
        val = jnp.where(dist < THR[b + 1], tab_ref[b, col], val)
    return val


def _toep_kernel(tab_ref, o_ref):
    t = pl.program_id(0)
    key = lax.broadcasted_iota(jnp.int32, (BAND, BAND), 0)
    qry = lax.broadcasted_iota(jnp.int32, (BAND, BAND), 1)
    for delta in range(-1, N_TOEP + 1):
        @pl.when(t == delta + 1)
        def _(delta=delta):
            for c in range(N_HEADS):
                if delta < 0:
                    o_ref[c] = jnp.full((BAND, BAND), NEG, F32)
                elif delta == N_TOEP:
                    o_ref[c] = jnp.full((BAND, BAND), tab_ref[N_BUCKETS - 1, c] * LOG2E, F32)
                else:
                    dist = delta * BAND + qry - key
                    val = _bias_of(dist, c, tab_ref, delta * BAND - BAND + 1, delta * BAND + BAND - 1)
                    o_ref[c] = jnp.where(dist >= 0, val * LOG2E, NEG)


def _toep_table(table):
    return pl.pallas_call(
        _toep_kernel,
        grid=(TOEP_TILES,),
        in_specs=[pl.BlockSpec(memory_space=pltpu.SMEM)],
        out_specs=pl.BlockSpec((None, N_HEADS, BAND, BAND), lambda t: (t, 0, 0, 0)),
        out_shape=jax.ShapeDtypeStruct((TOEP_TILES, N_HEADS, BAND, BAND), F32),
        compiler_params=_cparams(("arbitrary",)),
        name="bias_toeplitz",
    )(table)


def _band_bias_kernel(tab_ref, o_ref, *, dil):
    key = lax.broadcasted_iota(jnp.int32, (2 * BAND, BAND), 0)
    qry = lax.broadcasted_iota(jnp.int32, (2 * BAND, BAND), 1)
    steps = BAND + qry - key
    valid = (steps >= 0) & (steps <= BAND)
    for c in range(N_HEADS):
        val = _bias_of(steps * dil, c, tab_ref, 0, BAND * dil)
        o_ref[:, c * BAND:(c + 1) * BAND] = jnp.where(valid, val, NEG)


def _band_bias_table(table, dil):
    return pl.pallas_call(
        functools.partial(_band_bias_kernel, dil=dil),
        in_specs=[pl.BlockSpec(memory_space=pltpu.SMEM)],
        out_shape=jax.ShapeDtypeStruct((2 * BAND, N_HEADS * BAND), F32),
        compiler_params=_cparams(None),
        name=f"bias_band_d{dil}",
    )(table)


def _decode_bias_kernel(tab_ref, past_ref, new_ref, *, cols, past_len, window, dil, tq):
    n_grp = len(cols)
    t_p = lax.broadcasted_iota(jnp.int32, (tq, past_len), 0)
    i_p = lax.broadcasted_iota(jnp.int32, (tq, past_len), 1)
    d_p = past_len + t_p - i_p
    ok_p = d_p <= window
    if dil > 1:
        ok_p = ok_p & ((d_p & (dil - 1)) == 0)
    t_n = lax.broadcasted_iota(jnp.int32, (tq, LANES), 0)
    i_n = lax.broadcasted_iota(jnp.int32, (tq, LANES), 1)
    d_n = t_n - i_n
    ok_n = (d_n >= 0) & (i_n < tq)
    if dil > 1:
        ok_n = ok_n & ((d_n & (dil - 1)) == 0)
    for r in range(n_grp):
        c = cols[r]
        vp = _bias_of(d_p, c, tab_ref, 1, min(window, past_len + tq - 1))
        past_ref[r * tq:(r + 1) * tq, :] = jnp.where(ok_p, vp, NEG)
        vn = _bias_of(d_n, c, tab_ref, 0, tq - 1)
        new_ref[r * tq:(r + 1) * tq, :] = jnp.where(ok_n, vn, NEG)


def _decode_bias_table(table, cols, past_len, window, dil, tq):
    outs = [pl.pallas_call(
        functools.partial(_decode_bias_kernel, cols=tuple(cc), past_len=past_len, window=window,
                          dil=dil, tq=tq),
        in_specs=[pl.BlockSpec(memory_space=pltpu.SMEM)],
        out_shape=(jax.ShapeDtypeStruct((len(cc) * tq, past_len), F32),
                   jax.ShapeDtypeStruct((len(cc) * tq, LANES), F32)),
        compiler_params=_cparams(None),
        name=f"bias_decode_L{past_len}_d{dil}",
    )(table) for cc in cols]
    return jnp.stack([o[0] for o in outs]), jnp.stack([o[1] for o in outs])


def _proj_kernel(x_ref, g_ref, w_ref, cg_ref, *rest, plan, col_chunk, transposed, dils, slots):
    x = x_ref[...]
    h = x * lax.rsqrt(jnp.mean(x * x, axis=-1, keepdims=True) + RMS_EPS) * g_ref[...]
    hb = h.astype(BF16)
    low = _low_half()
    row_tile = x_ref.shape[0]
    dl_scr = None
    if max(dils) > 1:
        rest, dl_scr = rest[:-1], rest[-1]
    out_refs = rest
    if transposed:
        wt_ref, out_refs, ot_ref = rest[0], rest[1:-1], rest[-1]
        ot_ref[...] = _dot_nt(wt_ref[...], hb).astype(ot_ref.dtype)
    n_total = w_ref.shape[1]
    for c0 in range(0, n_total, col_chunk):
        cw = min(col_chunk, n_total - c0)
        y = _dot(hb, w_ref[:, c0:c0 + cw])
        for j in range(cw // LANES):
            src = c0 + j * LANES
            normed, dsts = plan[src // LANES]
            blk = y[:, j * LANES:(j + 1) * LANES]
            if normed:
                sq = blk * blk
                ms_lo = jnp.sum(jnp.where(low, sq, 0.0), axis=-1, keepdims=True) * (1.0 / HEAD_DIM)
                ms_hi = jnp.sum(jnp.where(low, 0.0, sq), axis=-1, keepdims=True) * (1.0 / HEAD_DIM)
                inv = jnp.where(low, lax.rsqrt(ms_lo + RMS_EPS), lax.rsqrt(ms_hi + RMS_EPS))
                blk = blk * inv * cg_ref[:, src:src + LANES]
            if any(dils[oi] > 1 for oi, _ in dsts):
                dl_scr[...] = blk
            for oi, dst in dsts:
                if slots[oi] > 1:
                    out_refs[oi][pl.ds(dst // LANES, row_tile, stride=slots[oi]), :] = blk.astype(out_refs[oi].dtype)
                elif dils[oi] == 1:
                    out_refs[oi][:, dst:dst + LANES] = blk.astype(out_refs[oi].dtype)
                else:
                    for r in range(dils[oi]):
                        part = dl_scr[pl.ds(r, row_tile // dils[oi], stride=dils[oi]), :]
                        out_refs[oi][r, :, dst:dst + LANES] = part.astype(out_refs[oi].dtype)


class _ProjLayout:
    def __init__(self):
        self.src = []
        self.gain = []
        self.plan = []
        self.outs = []

    def add_output(self, width, dtype, dil=1, slots=1):
        self.outs.append((width, dtype, dil, slots))
        return len(self.outs) - 1

    def add_chunk(self, src_cols, kind, dsts):
        assert len(src_cols) == LANES
        self.src.extend(src_cols)
        self.gain.extend([kind] * LANES)
        self.plan.append((kind != 0, list(dsts)))


def _cols(start, n):
    return list(range(start, start + n))


def _zeros(n):
    return [-1] * n


def _take_cols(w, src):
    parts, i = [], 0
    while i < len(src):
        j = i
        if src[i] < 0:
            while j < len(src) and src[j] < 0:
                j += 1
            parts.append(jnp.zeros((w.shape[0], j - i), w.dtype))
        else:
            while j + 1 < len(src) and src[j + 1] == src[j] + 1:
                j += 1
            j += 1
            parts.append(w[:, src[i]:src[i] + j - i])
        i = j
    return jnp.concatenate(parts, axis=1) if len(parts) > 1 else parts[0]


def _run_proj(x, norm_gain, w_in, q_gain, k_gain, layout, row_tile, col_chunk=1024, q_scale=1.0,
              t_cols=None, seq_len=None):
    rows = x.shape[0]
    src = np.asarray(layout.src)
    w_ext = _take_cols(w_in, list(layout.src)).astype(BF16)
    kind = np.asarray(layout.gain)
    n_ext = len(src)
    reps = n_ext // HEAD_DIM
    cg = jnp.where(jnp.asarray(kind == 1), jnp.tile(q_gain, reps) * (HEAD_DIM ** -0.5 * q_scale),
                   jnp.where(jnp.asarray(kind == 2), jnp.tile(k_gain, reps), 1.0)).reshape(1, n_ext)
    dils = tuple(d for _, _, d, _ in layout.outs)
    slots = tuple(sl for _, _, _, sl in layout.outs)
    tiles = None if seq_len is None else seq_len // row_tile
    out_shape, out_specs = [], []
    for w, dt, d, sl in layout.outs:
        if sl > 1:
            out_shape.append(jax.ShapeDtypeStruct((rows * sl, LANES), dt))
            out_specs.append(pl.BlockSpec((row_tile * sl, LANES), lambda i: (i, 0)))
        elif d == 1:
            out_shape.append(jax.ShapeDtypeStruct((rows, w), dt))
            out_specs.append(pl.BlockSpec((row_tile, w), lambda i: (i, 0)))
        else:
            out_shape.append(jax.ShapeDtypeStruct((rows // seq_len, d, seq_len // d, w), dt))
            out_specs.append(pl.BlockSpec((None, d, row_tile // d, w), lambda i: (i // tiles, 0, i % tiles, 0)))
    in_specs = [pl.BlockSpec((row_tile, D_MODEL), lambda i: (i, 0)),
                pl.BlockSpec((1, D_MODEL), lambda i: (0, 0)),
                pl.BlockSpec((D_MODEL, n_ext), lambda i: (0, 0), pipeline_mode=pl.Buffered(1)),
                pl.BlockSpec((1, n_ext), lambda i: (0, 0))]
    args = [x, norm_gain.reshape(1, D_MODEL), w_ext, cg]
    if t_cols is not None:
        n_t = len(t_cols)
        in_specs.append(pl.BlockSpec((n_t, D_MODEL), lambda i: (0, 0), pipeline_mode=pl.Buffered(1)))
        args.append(_take_cols(w_in, list(t_cols)).T.astype(BF16))
        out_shape.append(jax.ShapeDtypeStruct((rows // seq_len, n_t, seq_len), BF16))
        out_specs.append(pl.BlockSpec((None, n_t, row_tile), lambda i: (i // tiles, 0, i % tiles)))
    return pl.pallas_call(
        functools.partial(_proj_kernel, plan=tuple(layout.plan), col_chunk=col_chunk,
                          transposed=t_cols is not None, dils=dils, slots=slots),
        grid=(rows // row_tile,),
        in_specs=in_specs,
        out_specs=out_specs,
        out_shape=out_shape,
        scratch_shapes=[pltpu.VMEM((row_tile, LANES), F32)] if max(dils) > 1 else [],
        compiler_params=_cparams(("parallel",)),
        name="proj",
    )(*args)


def _layout_a(prompt):
    lay = _ProjLayout()
    oq = lay.add_output(1024, BF16 if prompt else F32)
    okv = lay.add_output(512, F32, slots=4 if prompt else 1)
    ok = lay.add_output(256, BF16) if prompt else None
    for c in range(8):
        lay.add_chunk(_cols(c * LANES, LANES), 1, [(oq, c * LANES)])
    for c in range(2):
        lay.add_chunk(_cols(1024 + c * LANES, LANES), 2,
                      [(okv, c * LANES)] + ([(ok, c * LANES)] if prompt else []))
    for c in range(2):
        lay.add_chunk(_cols(1280 + c * LANES, LANES), 0, [(okv, 256 + c * LANES)])
    return lay


def _layout_band_prompt(dils, n_kv):
    n_grp = len(dils)
    nq = n_grp * N_HEADS * HEAD_DIM
    nk = n_grp * n_kv * HEAD_DIM
    gk = n_kv * HEAD_DIM
    lay = _ProjLayout()
    okv = lay.add_output(2 * nk, F32)
    for g, d in enumerate(dils):
        oq = lay.add_output(N_HEADS * HEAD_DIM, BF16, d)
        okd = lay.add_output(n_kv * LANES, BF16, d)
        ovd = lay.add_output(n_kv * LANES, BF16, d)
        for c in range(N_HEADS * HEAD_DIM // LANES):
            lay.add_chunk(_cols(g * N_HEADS * HEAD_DIM + c * LANES, LANES), 1, [(oq, c * LANES)])
        for c in range(gk // LANES):
            lay.add_chunk(_cols(nq + g * gk + c * LANES, LANES), 2, [(okv, g * 2 * gk + c * LANES)])
        for c in range(gk // LANES):
            lay.add_chunk(_cols(nq + nk + g * gk + c * LANES, LANES), 0, [(okv, g * 2 * gk + gk + c * LANES)])
        for kv in range(n_kv):
            lay.add_chunk(_cols(nq + g * gk + kv * HEAD_DIM, HEAD_DIM) * 2, 2, [(okd, kv * LANES)])
        for kv in range(n_kv):
            lay.add_chunk(_cols(nq + nk + g * gk + kv * HEAD_DIM, HEAD_DIM) * 2, 0, [(ovd, kv * LANES)])
    return lay


def _layout_band_sample(n_grp, n_kv):
    nq = n_grp * N_HEADS * HEAD_DIM
    nk = n_grp * n_kv * HEAD_DIM
    grp_heads = N_HEADS // n_kv
    lay = _ProjLayout()
    oq = lay.add_output(n_grp * N_HEADS * LANES, F32)
    okv = lay.add_output(2 * nk, F32)
    for g in range(n_grp):
        for h in range(N_HEADS):
            qc = _cols((g * N_HEADS + h) * HEAD_DIM, HEAD_DIM)
            odd = (h // grp_heads) % 2
            lay.add_chunk(_zeros(HEAD_DIM) + qc if odd else qc + _zeros(HEAD_DIM), 1,
                          [(oq, (g * N_HEADS + h) * LANES)])
    gk = n_kv * HEAD_DIM
    for g in range(n_grp):
        for c in range(gk // LANES):
            lay.add_chunk(_cols(nq + g * gk + c * LANES, LANES), 2, [(okv, g * 2 * gk + c * LANES)])
        for c in range(gk // LANES):
            lay.add_chunk(_cols(nq + nk + g * gk + c * LANES, LANES), 0, [(okv, g * 2 * gk + gk + c * LANES)])
    return lay


def _diff_finalize(acc0, l0, acc1, l1, lam_ref, subln_ref, lam_init):
    lp = lam_ref[...]
    lam = (jnp.exp(jnp.sum(lp[0:1] * lp[1:2], axis=-1, keepdims=True))
           - jnp.exp(jnp.sum(lp[2:3] * lp[3:4], axis=-1, keepdims=True)) + lam_init)
    o = acc0 / l0 - lam * (acc1 / l1)
    o = o * lax.rsqrt(jnp.mean(o * o, axis=-1, keepdims=True) + RMS_EPS) * subln_ref[...]
    return o * (1.0 - lam_init)


def _a_prompt_kernel(q_ref, k_ref, vt_ref, toep_ref, lam_ref, subln_ref, o_ref,
                     qm_scr, ta_scr, tb_scr, p_scr, cm_scr, m_scr, acc_scr, *, tq, tk, lam_init, rb):
    qi = pl.program_id(2)
    low = _low_half()
    nh = 8
    w = nh * tq
    for g in range(4):
        qg = q_ref[:, g * LANES:(g + 1) * LANES]
        qm_scr[(2 * g) * tq:(2 * g + 1) * tq, :] = jnp.where(low, qg, jnp.zeros_like(qg))
        qm_scr[(2 * g + 1) * tq:(2 * g + 2) * tq, :] = jnp.where(low, jnp.zeros_like(qg), qg)
    m_scr[...] = jnp.full(m_scr.shape, NEG, F32)
    acc_scr[...] = jnp.zeros(acc_scr.shape, F32)
    n_chunks = (qi * tq + tq - 1) // tk + 1
    last = k_ref.shape[0] // tk - 1

    def stage_a(kj, t_ref):
        k0 = pl.multiple_of(jnp.minimum(kj, last) * tk, tk)
        s = _dot_nt(k_ref[pl.ds(k0, tk), :], qm_scr[...])
        macc = jnp.full((SUBLANES, w), NEG, F32)
        for u in range(tk // BAND):
            idx = [jnp.clip((qi * (tq // BAND) + r) - (kj * (tk // BAND) + u) + 1, 0, N_TOEP + 1)
                   for r in range(tq // BAND)]
            for b0 in range(0, BAND, rb):
                bias = jnp.concatenate([toep_ref[idx[r], h, b0:b0 + rb, :] for h in range(nh)
                                        for r in range(tq // BAND)], axis=1)
                r0 = u * BAND + b0
                t = s[r0:r0 + rb, :] + bias
                t_ref[r0:r0 + rb, :] = t
                for g8 in range(rb // SUBLANES):
                    macc = jnp.maximum(macc, t[g8 * SUBLANES:(g8 + 1) * SUBLANES, :])
        return jnp.max(macc, axis=0, keepdims=True)

    def stage_b(kj, t_ref, m_new, alpha):
        k0 = pl.multiple_of(jnp.minimum(kj, last) * tk, tk)
        vtc = jnp.concatenate([vt_ref[:, pl.ds(k0, tk)], jnp.ones((ONES_ROWS, tk), BF16)], axis=0)
        mb = jnp.broadcast_to(m_new, (SUBLANES, w))
        for r0 in range(0, tk, rb):
            t = t_ref[r0:r0 + rb, :]
            ps = []
            for g8 in range(rb // SUBLANES):
                p8 = jnp.exp2(t[g8 * SUBLANES:(g8 + 1) * SUBLANES, :] - mb)
                ps.append(p8)
            p_scr[r0:r0 + rb, :] = jnp.concatenate(ps, axis=0).astype(BF16)
        acc_scr[...] = alpha * acc_scr[...] + _dot(vtc, p_scr[...])

    cm_scr[...] = stage_a(0, ta_scr)

    def body(j, carry):
        cmax1 = stage_a(2 * j + 1, tb_scr)
        m_prev = m_scr[...]
        m0 = jnp.maximum(m_prev, cm_scr[...])
        stage_b(2 * j, ta_scr, m0, jnp.exp2(m_prev - m0))
        cm_scr[...] = stage_a(2 * j + 2, ta_scr)
        m1 = jnp.maximum(m0, cmax1)
        stage_b(2 * j + 1, tb_scr, m1, jnp.exp2(m0 - m1))
        m_scr[...] = m1
        return carry

    lax.fori_loop(0, (n_chunks + 1) // 2, body, 0)
    lp = lam_ref[...]
    lam = (jnp.exp(jnp.sum(lp[0:1] * lp[1:2], axis=-1, keepdims=True))
           - jnp.exp(jnp.sum(lp[2:3] * lp[3:4], axis=-1, keepdims=True)) + lam_init)
    on = acc_scr[0:LANES, :] / acc_scr[LANES:LANES + 1, :]
    for g in range(4):
        o = on[:, (2 * g) * tq:(2 * g + 1) * tq] - lam * on[:, (2 * g + 1) * tq:(2 * g + 2) * tq]
        o = o * lax.rsqrt(jnp.mean(o * o, axis=0, keepdims=True) + RMS_EPS) * subln_ref[...]
        o = o * (1.0 - lam_init)
        o_ref[:, g * LANES:(g + 1) * LANES] = o.T.astype(o_ref.dtype)


def _a_prompt(q, k, vt, toep_t, lam_p, subln, lam_init, tq=512, tk=512, rb=32):
    b, t, _ = q.shape
    return pl.pallas_call(
        functools.partial(_a_prompt_kernel, tq=tq, tk=tk, lam_init=lam_init, rb=rb),
        grid=(b, A_KV, t // tq),
        in_specs=[pl.BlockSpec((None, tq, 512), lambda bi, kv, qi: (bi, qi, kv)),
                  pl.BlockSpec((None, t, LANES), lambda bi, kv, qi: (bi, 0, kv), pipeline_mode=pl.Buffered(1)),
                  pl.BlockSpec((None, LANES, t), lambda bi, kv, qi: (bi, kv, 0), pipeline_mode=pl.Buffered(1)),
                  pl.BlockSpec((TOEP_TILES, 8, BAND, BAND), lambda bi, kv, qi: (0, kv, 0, 0),
                               pipeline_mode=pl.Buffered(1)),
                  pl.BlockSpec((4, HEAD_DIM), lambda bi, kv, qi: (0, 0)),
                  pl.BlockSpec((LANES, 1), lambda bi, kv, qi: (0, 0))],
        out_specs=pl.BlockSpec((None, tq, 512), lambda bi, kv, qi: (bi, qi, kv)),
        out_shape=jax.ShapeDtypeStruct((b, t, 1024), BF16),
        scratch_shapes=[pltpu.VMEM((8 * tq, LANES), BF16),
                        pltpu.VMEM((tk, 8 * tq), F32),
                        pltpu.VMEM((tk, 8 * tq), F32),
                        pltpu.VMEM((tk, 8 * tq), BF16),
                        pltpu.VMEM((1, 8 * tq), F32),
                        pltpu.VMEM((1, 8 * tq), F32),
                        pltpu.VMEM((LANES + ONES_ROWS, 8 * tq), F32)],
        compiler_params=_cparams(("parallel", "parallel", "arbitrary")),
        name="a_prompt",
    )(q, k, vt, toep_t, lam_p, subln.reshape(LANES, 1))


def _a_sample_kernel(pt_ref, q_ref, new_ref, bp_ref, bc_ref, bn_ref, lam_ref, subln_ref, *rest,
                     pages, tq, lam_init):
    page_refs = rest[:pages]
    o_ref = rest[pages]
    qm_scr, m_scr, l_scr, acc_scr = rest[pages + 1:]
    c = pl.program_id(1)
    last = pl.num_programs(1) - 1
    low = _low_half()
    rows = 8 * tq

    @pl.when(c == 0)
    def _():
        for kv in range(A_KV):
            for g in range(4):
                qg = q_ref[:, (kv * 4 + g) * LANES:(kv * 4 + g + 1) * LANES]
                qm_scr[kv, (2 * g) * tq:(2 * g + 1) * tq, :] = jnp.where(low, qg, 0.0)
                qm_scr[kv, (2 * g + 1) * tq:(2 * g + 2) * tq, :] = jnp.where(low, 0.0, qg)
        m_scr[...] = jnp.full(m_scr.shape, NEG, F32)
        l_scr[...] = jnp.zeros(l_scr.shape, F32)
        acc_scr[...] = jnp.zeros(acc_scr.shape, F32)

    n_slot = 2 * A_KV
    is_last = c == last

    def scores(kv):
        qm = qm_scr[kv].astype(BF16)
        keys = jnp.concatenate([pr[pl.ds(kv, PAGE_SIZE, stride=n_slot), :] for pr in page_refs], axis=0)
        return _dot_nt(qm, keys.astype(BF16)) + jnp.where(is_last, bp_ref[kv], bc_ref[kv])

    def values(kv):
        return jnp.concatenate([pr[pl.ds(A_KV + kv, PAGE_SIZE, stride=n_slot), :] for pr in page_refs], axis=0).astype(BF16)

    s = jnp.concatenate([scores(kv) for kv in range(A_KV)], axis=0)
    m_prev = m_scr[...]
    m_new = jnp.maximum(m_prev, jnp.max(s, axis=-1, keepdims=True))
    alpha = jnp.exp(m_prev - m_new)
    p = jnp.exp(s - m_new)
    l_scr[...] = alpha * l_scr[...] + jnp.sum(p, axis=-1, keepdims=True)
    m_scr[...] = m_new
    pb = p.astype(BF16)
    pv = jnp.concatenate([_dot(pb[kv * rows:(kv + 1) * rows], values(kv)) for kv in range(A_KV)], axis=0)
    acc_scr[...] = alpha * acc_scr[...] + pv

    @pl.when(is_last)
    def _():
        pad = jnp.zeros((LANES - tq, LANES), F32)
        for kv in range(A_KV):
            sl = slice(kv * rows, (kv + 1) * rows)
            qm = qm_scr[kv].astype(BF16)
            kn = jnp.concatenate([new_ref[:, kv * LANES:(kv + 1) * LANES], pad], axis=0)
            vn = jnp.concatenate([new_ref[:, (A_KV + kv) * LANES:(A_KV + kv + 1) * LANES], pad], axis=0)
            s = _dot_nt(qm, kn.astype(BF16)) + bn_ref[kv]
            m_prev = m_scr[sl]
            m_new = jnp.maximum(m_prev, jnp.max(s, axis=-1, keepdims=True))
            alpha = jnp.exp(m_prev - m_new)
            p = jnp.exp(s - m_new)
            l = alpha * l_scr[sl] + jnp.sum(p, axis=-1, keepdims=True)
            acc = alpha * acc_scr[sl] + _dot(p.astype(BF16), vn.astype(BF16))
            for g in range(4):
                r0, r1 = (2 * g) * tq, (2 * g + 1) * tq
                o = _diff_finalize(acc[r0:r0 + tq], l[r0:r0 + tq], acc[r1:r1 + tq], l[r1:r1 + tq],
                                   lam_ref, subln_ref, lam_init)
                o_ref[:, (kv * 4 + g) * LANES:(kv * 4 + g + 1) * LANES] = o.astype(o_ref.dtype)


def _a_sample(q, new_kv, cache, layer, page_table, bias_past, bias_const, bias_new, lam_p, subln, lam_init,
              pages=64):
    db, tq, _ = q.shape
    n_pages = page_table.shape[1]
    n_chunks = n_pages // pages
    assert n_chunks == 1 or pages * PAGE_SIZE + 1 >= THR[-1]
    cache2 = cache.reshape(cache.shape[0], cache.shape[1], PAGE_SIZE * 2 * A_KV, LANES)
    rows = 8 * tq

    def page_spec(j):
        return pl.BlockSpec((None, None, PAGE_SIZE * 2 * A_KV, LANES),
                            lambda bi, c, pt: (layer, pt[bi * n_pages + c * pages + j], 0, 0))

    grid_spec = pltpu.PrefetchScalarGridSpec(
        num_scalar_prefetch=1,
        grid=(db, n_chunks),
        in_specs=[pl.BlockSpec((None, tq, 1024), lambda bi, c, pt: (bi, 0, 0)),
                  pl.BlockSpec((None, tq, 512), lambda bi, c, pt: (bi, 0, 0)),
                  pl.BlockSpec((A_KV, rows, pages * PAGE_SIZE), lambda bi, c, pt: (0, 0, n_chunks - 1)),
                  pl.BlockSpec((A_KV, rows, 1), lambda bi, c, pt: (0, 0, 0)),
                  pl.BlockSpec((A_KV, rows, LANES), lambda bi, c, pt: (0, 0, 0)),
                  pl.BlockSpec((4, HEAD_DIM), lambda bi, c, pt: (0, 0)),
                  pl.BlockSpec((1, LANES), lambda bi, c, pt: (0, 0))]
                 + [page_spec(j) for j in range(pages)],
        out_specs=pl.BlockSpec((None, tq, 1024), lambda bi, c, pt: (bi, 0, 0)),
        scratch_shapes=[pltpu.VMEM((A_KV, rows, LANES), F32),
                        pltpu.VMEM((A_KV * rows, 1), F32),
                        pltpu.VMEM((A_KV * rows, 1), F32),
                        pltpu.VMEM((A_KV * rows, LANES), F32)])
    return pl.pallas_call(
        functools.partial(_a_sample_kernel, pages=pages, tq=tq, lam_init=lam_init),
        grid_spec=grid_spec,
        out_shape=jax.ShapeDtypeStruct((db, tq, 1024), F32),
        compiler_params=_cparams(("parallel", "arbitrary")),
        name="a_sample",
    )(page_table.reshape(-1), q, new_kv, bias_past, bias_const, bias_new, lam_p, subln.reshape(1, LANES),
      *([cache2] * pages))


def _band_kernel(*refs, n_kv, use_sink, want_lse, n_sub):
    q_ref, kp_ref, kc_ref, vp_ref, vc_ref, bias_ref = refs[:6]
    pos = 6
    sink_ref = None
    if use_sink:
        sink_ref = refs[pos]
        pos += 1
    o_ref = refs[pos]
    lse_ref = refs[pos + 1] if want_lse else None
    qm_scr = refs[-1]
    i = pl.program_id(2)
    grp = N_HEADS // n_kv
    low = _low_half()
    key = lax.broadcasted_iota(jnp.int32, (2 * BAND, 1), 0)
    ones = jnp.ones((ONES_ROWS, 2 * BAND), BF16)
    for j in range(n_sub):
        rows = slice(j * BAND, (j + 1) * BAND)
        prv = slice((j - 1) * BAND, j * BAND)

        def kv_pair(pref, cref, kv):
            lanes = slice(kv * LANES, (kv + 1) * LANES)
            before = pref[:, lanes] if j == 0 else cref[prv, lanes]
            return jnp.concatenate([before, cref[rows, lanes]], axis=0)

        for c in range(N_HEADS // 2):
            qp = q_ref[rows, c * LANES:(c + 1) * LANES]
            zero = jnp.zeros_like(qp)
            qm_scr[(2 * c) * BAND:(2 * c + 1) * BAND, :] = jnp.where(low, qp, zero)
            qm_scr[(2 * c + 1) * BAND:(2 * c + 2) * BAND, :] = jnp.where(low, zero, qp)
        s_parts = [_dot_nt(kv_pair(kp_ref, kc_ref, kv), qm_scr[kv * grp * BAND:(kv + 1) * grp * BAND, :])
                   for kv in range(n_kv)]
        s = jnp.concatenate(s_parts, axis=1) + bias_ref[...]
        if j == 0:
            s = jnp.where((key >= BAND) | (i > 0), s, NEG)
        m = jnp.max(s, axis=0, keepdims=True)
        if use_sink:
            m = jnp.maximum(m, sink_ref[...])
        p = jnp.exp(s - m).astype(BF16)
        acc_parts = []
        for kv in range(n_kv):
            vt = kv_pair(vp_ref, vc_ref, kv).astype(F32).T[0:HEAD_DIM, :].astype(BF16)
            acc_parts.append(_dot(jnp.concatenate([vt, ones], axis=0), p[:, kv * grp * BAND:(kv + 1) * grp * BAND]))
        acc = jnp.concatenate(acc_parts, axis=1)
        l = acc[HEAD_DIM:HEAD_DIM + 1, :]
        if use_sink:
            l = l + jnp.exp(sink_ref[...] - m)
        on = acc[0:HEAD_DIM, :] / l
        for c in range(N_HEADS // 2):
            pair = jnp.concatenate([on[:, (2 * c) * BAND:(2 * c + 1) * BAND], on[:, (2 * c + 1) * BAND:(2 * c + 2) * BAND]], axis=0)
            o_ref[rows, c * LANES:(c + 1) * LANES] = pair.T.astype(o_ref.dtype)
        if want_lse:
            lse = m + jnp.log(l)
            lrows = jnp.concatenate([lse[:, h * BAND:(h + 1) * BAND] for h in range(N_HEADS)]
                                    + [jnp.zeros((LANES - N_HEADS, BAND), F32)], axis=0)
            lse_ref[rows, :] = lrows.T


def _band_prompt(q, kd, vd, bias, sinks, n_kv, want_lse, out_dtype):
    b, dil, ts, _ = q.shape
    n_sub = 4 if ts % (4 * BAND) == 0 else (2 if ts % (2 * BAND) == 0 else 1)
    blk = n_sub * BAND
    nb = ts // blk
    ck = n_kv * LANES
    cur = lambda bi, r, i: (bi, r, i, 0)
    prev = lambda bi, r, i: (bi, r, jnp.maximum(i * n_sub - 1, 0), 0)
    in_specs = [pl.BlockSpec((None, None, blk, 1024), cur),
                pl.BlockSpec((None, None, BAND, ck), prev),
                pl.BlockSpec((None, None, blk, ck), cur),
                pl.BlockSpec((None, None, BAND, ck), prev),
                pl.BlockSpec((None, None, blk, ck), cur),
                pl.BlockSpec((2 * BAND, N_HEADS * BAND), lambda bi, r, i: (0, 0))]
    args = [q, kd, kd, vd, vd, bias]
    if sinks is not None:
        in_specs.append(pl.BlockSpec((1, N_HEADS * BAND), lambda bi, r, i: (0, 0)))
        args.append(jnp.repeat(sinks, BAND).reshape(1, N_HEADS * BAND))
    out_specs = [pl.BlockSpec((None, None, blk, 1024), cur)]
    out_shape = [jax.ShapeDtypeStruct((b, dil, ts, 1024), out_dtype)]
    if want_lse:
        out_specs.append(pl.BlockSpec((None, None, blk, LANES), cur))
        out_shape.append(jax.ShapeDtypeStruct((b, dil, ts, LANES), F32))
    return pl.pallas_call(
        functools.partial(_band_kernel, n_kv=n_kv, use_sink=sinks is not None, want_lse=want_lse, n_sub=n_sub),
        grid=(b, dil, nb),
        in_specs=in_specs,
        out_specs=out_specs,
        out_shape=out_shape,
        scratch_shapes=[pltpu.VMEM((N_HEADS * BAND, LANES), BF16)],
        compiler_params=_cparams(("parallel", "parallel", "arbitrary")),
        name=f"band_d{dil}",
    )(*args)


def _decode_kernel(*refs, n_grp, n_kv, past_lens, use_sink, tq):
    q_ref, new_ref = refs[:2]
    pos = 2
    st_refs = refs[pos:pos + n_grp]; pos += n_grp
    bp_refs = refs[pos:pos + n_grp]; pos += n_grp
    bn_refs = refs[pos:pos + n_grp]; pos += n_grp
    sink_ref = None
    if use_sink:
        sink_ref = refs[pos]; pos += 1
    o_ref = refs[pos]; pos += 1
    nst_refs = refs[pos:pos + n_grp]
    width = 2 * n_kv * HEAD_DIM
    n_chunks = n_kv // 2
    heads_per_chunk = N_HEADS // n_chunks
    grp_heads = N_HEADS // n_kv
    low = _low_half()
    pad = jnp.zeros((LANES - tq, LANES), F32)
    for c in range(n_chunks):
        parts = []
        for g in range(n_grp):
            qs = jnp.concatenate(
                [q_ref[:, (g * N_HEADS + c * heads_per_chunk + hh) * LANES:(g * N_HEADS + c * heads_per_chunk + hh + 1) * LANES]
                 for hh in range(heads_per_chunk)], axis=0).astype(BF16)
            st = st_refs[g]
            kt = st[0, c * LANES:(c + 1) * LANES, :].astype(BF16)
            vt = st[1, c * LANES:(c + 1) * LANES, :].astype(BF16)
            kn = jnp.concatenate([new_ref[:, g * width + c * LANES:g * width + (c + 1) * LANES], pad], axis=0).astype(BF16)
            vn = jnp.concatenate([new_ref[:, g * width + n_kv * HEAD_DIM + c * LANES:g * width + n_kv * HEAD_DIM + (c + 1) * LANES], pad], axis=0).astype(BF16)
            sp = _dot(qs, kt) + bp_refs[g][c]
            sn = _dot_nt(qs, kn) + bn_refs[g][c]
            m = jnp.maximum(jnp.max(sp, axis=-1, keepdims=True), jnp.max(sn, axis=-1, keepdims=True))
            if use_sink:
                m = jnp.maximum(m, sink_ref[c])
            pp = jnp.exp(sp - m)
            pn = jnp.exp(sn - m)
            l = jnp.sum(pp, axis=-1, keepdims=True) + jnp.sum(pn, axis=-1, keepdims=True)
            if use_sink:
                l = l + jnp.exp(sink_ref[c] - m)
            acc = _dot_nt(pp.astype(BF16), vt) + _dot(pn.astype(BF16), vn)
            parts.append((m, l, acc))
        m_all = parts[0][0]
        for m, _, _ in parts[1:]:
            m_all = jnp.maximum(m_all, m)
        num = 0.0
        den = 0.0
        for m, l, acc in parts:
            w = jnp.exp(m - m_all)
            num = num + acc * w
            den = den + l * w
        o = num / den
        for hh in range(heads_per_chunk):
            h = c * heads_per_chunk + hh
            odd = (h // grp_heads) % 2
            oh = o[hh * tq:(hh + 1) * tq, :]
            o_ref[:, h * LANES:(h + 1) * LANES] = (jnp.where(low, 0.0, oh) if odd else jnp.where(low, oh, 0.0)).astype(o_ref.dtype)
    n_rows = n_kv * HEAD_DIM
    for g in range(n_grp):
        lp = past_lens[g]
        for kv_slot in range(2):
            c0 = g * width + kv_slot * n_rows
            new_t = jnp.concatenate([new_ref[:, c0:c0 + n_rows], jnp.zeros((LANES - tq, n_rows), F32)], axis=0).T
            for r0 in range(0, n_rows, HEAD_DIM):
                if lp > tq:
                    nst_refs[g][kv_slot, r0:r0 + HEAD_DIM, 0:lp - tq] = st_refs[g][kv_slot, r0:r0 + HEAD_DIM, tq:lp]
                nst_refs[g][kv_slot, r0:r0 + HEAD_DIM, lp - tq:lp] = new_t[r0:r0 + HEAD_DIM, 0:tq]


def _decode(q, new_kv, states, layer, bias_past, bias_new, sinks_rows, n_kv):
    db, tq, _ = q.shape
    n_grp = len(states)
    width = 2 * n_kv * HEAD_DIM
    n_chunks = n_kv // 2
    rows = tq * N_HEADS // n_chunks
    past_lens = tuple(s.shape[2] for s in states)
    st_t = [jnp.transpose(s, (0, 1, 3, 4, 5, 2)).reshape(s.shape[0], db, 2, n_kv * HEAD_DIM, s.shape[2])
            for s in states]
    in_specs = [pl.BlockSpec((None, tq, q.shape[2]), lambda bi: (bi, 0, 0)),
                pl.BlockSpec((None, tq, n_grp * width), lambda bi: (bi, 0, 0))]
    in_specs += [pl.BlockSpec((None, None, 2, n_kv * HEAD_DIM, lp), lambda bi: (layer, bi, 0, 0, 0))
                 for lp in past_lens]
    in_specs += [pl.BlockSpec((n_chunks, rows, lp), lambda bi: (0, 0, 0)) for lp in past_lens]
    in_specs += [pl.BlockSpec((n_chunks, rows, LANES), lambda bi: (0, 0, 0)) for _ in past_lens]
    args = [q, new_kv] + st_t + list(bias_past) + list(bias_new)
    if sinks_rows is not None:
        in_specs.append(pl.BlockSpec((n_chunks, rows, 1), lambda bi: (0, 0, 0)))
        args.append(sinks_rows)
    res = pl.pallas_call(
        functools.partial(_decode_kernel, n_grp=n_grp, n_kv=n_kv, past_lens=past_lens,
                          use_sink=sinks_rows is not None, tq=tq),
        grid=(db,),
        in_specs=in_specs,
        out_specs=[pl.BlockSpec((None, tq, N_HEADS * LANES), lambda bi: (bi, 0, 0))]
        + [pl.BlockSpec((None, 2, n_kv * HEAD_DIM, lp), lambda bi: (bi, 0, 0, 0)) for lp in past_lens],
        out_shape=[jax.ShapeDtypeStruct((db, tq, N_HEADS * LANES), F32)]
        + [jax.ShapeDtypeStruct((db, 2, n_kv * HEAD_DIM, lp), F32) for lp in past_lens],
        compiler_params=_cparams(("parallel",)),
        name=f"decode_kv{n_kv}",
    )(*args)
    new_states = [jnp.transpose(r.reshape(db, 2, n_kv, HEAD_DIM, lp), (0, 4, 1, 2, 3))
                  for r, lp in zip(res[1:], past_lens)]
    return res[0], new_states


def _ffn_kernel(*refs, n_attn, sample, tiles_per_seq, row_tile, ff_chunk, tq, dils):
    x_ref = refs[0]
    pos = 1
    if n_attn == 1:
        o_in = refs[pos]; pos += 1
    else:
        o_refs = refs[pos:pos + n_attn]; pos += n_attn
        lse_refs = refs[pos:pos + n_attn]; pos += n_attn
    wo_ref, g_ref, win_ref, cw_ref, cb_ref, wout_ref = refs[pos:pos + 6]; pos += 6
    if sample:
        fix1_ref, fix2_ref = refs[pos:pos + 2]; pos += 2
    xo_ref, gate_ref = refs[pos:pos + 2]; pos += 2
    gp_scr, halo_scr = refs[pos:pos + 2]; pos += 2
    il_scr = refs[pos:]
    i = pl.program_id(0)
    low = _low_half()

    @pl.when(i == 0)
    def _():
        halo_scr[...] = jnp.zeros(halo_scr.shape, F32)

    if n_attn == 1:
        ob = o_in[...]
    else:
        def natural(ref4, g, col0):
            d = dils[g]
            if d == 1:
                return ref4[0, :, col0:col0 + LANES]
            scr = il_scr[sum(1 for x in dils[:g] if x > 1)]
            for r in range(d):
                scr[pl.ds(r, row_tile // d, stride=d), :] = ref4[r, :, col0:col0 + LANES]
            return scr[...]

        lses = [natural(lse_refs[g], g, 0) for g in range(n_attn)]
        m_all = lses[0]
        for z in lses[1:]:
            m_all = jnp.maximum(m_all, z)
        es = [jnp.exp(z - m_all) for z in lses]
        den = es[0]
        for e in es[1:]:
            den = den + e
        ws = [e / den for e in es]
        chunks = []
        for c in range(1024 // LANES):
            acc = None
            for g in range(n_attn):
                wsel = jnp.where(low, ws[g][:, 2 * c:2 * c + 1], ws[g][:, 2 * c + 1:2 * c + 2])
                term = wsel * natural(o_refs[g], g, c * LANES)
                acc = term if acc is None else acc + term
            chunks.append(acc.astype(BF16))
        ob = jnp.concatenate(chunks, axis=1)
    x1 = x_ref[...] + _dot(ob, wo_ref[...])
    h = x1 * lax.rsqrt(jnp.mean(x1 * x1, axis=-1, keepdims=True) + RMS_EPS) * g_ref[...]
    hb = h.astype(BF16)

    if sample:
        t_in_seq = lax.broadcasted_iota(jnp.int32, (row_tile, 1), 0) & (tq - 1)
    y = jnp.zeros((row_tile, D_MODEL), F32)
    for f in range(D_FF // ff_chunk):
        f0 = f * ff_chunk
        gate = _dot(hb, win_ref[:, f0:f0 + ff_chunk])
        up = _dot(hb, win_ref[:, D_FF + f0:D_FF + f0 + ff_chunk])
        if sample:
            gp_scr[0:SUBLANES, :] = jnp.zeros((SUBLANES, ff_chunk), F32)
        else:
            first = (i % tiles_per_seq) == 0
            gp_scr[0:SUBLANES, :] = jnp.where(first, 0.0, halo_scr[f])
        gp_scr[SUBLANES:SUBLANES + row_tile, :] = gate
        g2 = gp_scr[SUBLANES - 2:SUBLANES - 2 + row_tile, :]
        g1 = gp_scr[SUBLANES - 1:SUBLANES - 1 + row_tile, :]
        if sample:
            g2 = jnp.where(t_in_seq >= 2, g2, fix2_ref[:, f0:f0 + ff_chunk])
            g1 = jnp.where(t_in_seq >= 1, g1, fix1_ref[:, f0:f0 + ff_chunk])
        gc = (cb_ref[:, f0:f0 + ff_chunk] + cw_ref[0:1, f0:f0 + ff_chunk] * g2
              + cw_ref[1:2, f0:f0 + ff_chunk] * g1 + cw_ref[2:3, f0:f0 + ff_chunk] * gate)
        act = gc * (1.0 / (1.0 + jnp.exp(-gc))) * up
        y = y + _dot(act.astype(BF16), wout_ref[f0:f0 + ff_chunk, :])
        if sample:
            gate_ref[:, f0:f0 + ff_chunk] = gate
        else:
            tail = gate[row_tile - SUBLANES:row_tile, :]
            halo_scr[f] = tail
            gate_ref[:, f0:f0 + ff_chunk] = tail
    xo_ref[...] = x1 + y


def _ffn(x, attn, lses, w_o, gain, w_in, conv_w, conv_b, w_out, fix, seq_len, row_tile=512,
         ff_chunk=1408):
    rows = x.shape[0]
    sample = fix is not None
    n_attn = len(attn)
    row_spec = lambda w: pl.BlockSpec((row_tile, w), lambda i: (i, 0))
    full = lambda a: pl.BlockSpec(a.shape, lambda i: (0,) * a.ndim, pipeline_mode=pl.Buffered(1))
    tiles = seq_len // row_tile
    res_spec = lambda a: pl.BlockSpec((None, a.shape[1], row_tile // a.shape[1], a.shape[3]),
                                      lambda i: (i // tiles, 0, i % tiles, 0))
    dils = (1,)
    if n_attn == 1:
        in_specs = [row_spec(D_MODEL), row_spec(attn[0].shape[1])]
        args = [x] + list(attn)
    else:
        dils = tuple(a.shape[1] for a in attn)
        in_specs = [row_spec(D_MODEL)] + [res_spec(a) for a in attn] + [res_spec(a) for a in lses]
        args = [x] + list(attn) + list(lses)
    consts = [w_o.astype(BF16), gain.reshape(1, D_MODEL), w_in.astype(BF16), conv_w,
              conv_b.reshape(1, D_FF), w_out.astype(BF16)]
    in_specs += [full(a) for a in consts]
    args += consts
    if sample:
        in_specs += [row_spec(D_FF), row_spec(D_FF)]
        args += list(fix)
        gate_spec = row_spec(D_FF)
        gate_shape = jax.ShapeDtypeStruct((rows, D_FF), F32)
        tiles_per_seq = 1
        tq = seq_len
    else:
        gate_spec = pl.BlockSpec((None, SUBLANES, D_FF), lambda i: (i, 0, 0))
        gate_shape = jax.ShapeDtypeStruct((rows // row_tile, SUBLANES, D_FF), F32)
        tiles_per_seq = seq_len // row_tile
        tq = 1
    return pl.pallas_call(
        functools.partial(_ffn_kernel, n_attn=n_attn, sample=sample, tiles_per_seq=tiles_per_seq,
                          row_tile=row_tile, ff_chunk=ff_chunk, tq=tq, dils=dils),
        grid=(rows // row_tile,),
        in_specs=in_specs,
        out_specs=[row_spec(D_MODEL), gate_spec],
        out_shape=[jax.ShapeDtypeStruct((rows, D_MODEL), F32), gate_shape],
        scratch_shapes=[pltpu.VMEM((SUBLANES + row_tile, ff_chunk), F32),
                        pltpu.VMEM((D_FF // ff_chunk, SUBLANES, ff_chunk), F32)]
        + [pltpu.VMEM((row_tile, LANES), F32) for d in dils if d > 1],
        compiler_params=_cparams(("arbitrary",)),
        name="ffn_sample" if sample else "ffn_prompt",
    )(*args)


def _expand_wo_sample(w_o, n_kv):
    grp_heads = N_HEADS // n_kv
    src = []
    for h in range(N_HEADS):
        rows = _cols(h * HEAD_DIM, HEAD_DIM)
        src += (_zeros(HEAD_DIM) + rows) if (h // grp_heads) % 2 else (rows + _zeros(HEAD_DIM))
    return _take_cols(w_o.T, src).T


def kernel(x_prompt, x_sample, cache_kv_a, state_kv_b, state_kv_c1, state_kv_c2, state_kv_c3, state_conv_ffn, page_table, rel_bias_table, norm_mix, norm_ffn, w_in_a, q_norm_a, k_norm_a, lambda_a, subln_a, w_o_a, w_in_b, q_norm_b, k_norm_b, sinks_b, w_o_b, w_in_c, q_norm_c, k_norm_c, w_o_c, w_ffn_in, conv_ffn_w, conv_ffn_b, w_ffn_out):
    bp, tp, _ = x_prompt.shape
    bs, ts, _ = x_sample.shape
    depth = norm_mix.shape[0]
    past_len = page_table.shape[1] * PAGE_SIZE
    xp = x_prompt.reshape(bp * tp, D_MODEL)
    xs = x_sample.reshape(bs * ts, D_MODEL)
    state_c = (state_kv_c1, state_kv_c2, state_kv_c3)

    toep = _toep_table(rel_bias_table)
    band_bias = {d: _band_bias_table(rel_bias_table, d) for d in (1, 4, 16)}
    a_cols = [[kv * 8 + j for j in range(8)] for kv in range(A_KV)]
    a_bias_past, a_bias_new = _decode_bias_table(rel_bias_table, a_cols, past_len, past_len + ts, 1, ts)
    a_bias_far = jnp.repeat(rel_bias_table[N_BUCKETS - 1][jnp.asarray(a_cols)], ts, axis=1)[..., None]
    b_cols = [list(range(N_HEADS))]
    b_len = state_kv_b.shape[2]
    b_bias_past, b_bias_new = _decode_bias_table(rel_bias_table, b_cols, b_len, B_WINDOW, 1, ts)
    c_cols = [list(range(8)), list(range(8, 16))]
    c_bias = [_decode_bias_table(rel_bias_table, c_cols, state_c[g].shape[2], w, d, ts)
              for g, (w, d) in enumerate(C_PAIRS)]

    kv_a_p, kv_a_s, kv_b_p, kv_b_s = [], [], [], []
    kv_c_p = [[] for _ in C_PAIRS]
    kv_c_s = [[] for _ in C_PAIRS]
    conv_p, conv_s = [], []
    for i in range(depth):
        n = i // N_MIXERS
        if i % N_MIXERS == 0:
            lam_init = 0.8 - 0.6 * math.exp(-0.3 * i)
            q, kvn, k, vt = _run_proj(xp, norm_mix[i], w_in_a[n], q_norm_a[n], k_norm_a[n], _layout_a(True), 512,
                                      q_scale=LOG2E, t_cols=_cols(1280, 256), seq_len=tp)
            op = _a_prompt(q.reshape(bp, tp, 1024), k.reshape(bp, tp, 256), vt, toep, lambda_a[n], subln_a[n],
                           lam_init)
            attn_p, lse_p = [op.reshape(bp * tp, 1024)], None
            kv_a_p.append(kvn.reshape(bp, tp, 2, A_KV, 2 * HEAD_DIM))
            qs, kvs = _run_proj(xs, norm_mix[i], w_in_a[n], q_norm_a[n], k_norm_a[n], _layout_a(False), 512)
            os_ = _a_sample(qs.reshape(bs, ts, 1024), kvs.reshape(bs, ts, 512), cache_kv_a, n, page_table,
                            a_bias_past, a_bias_far, a_bias_new, lambda_a[n], subln_a[n], lam_init)
            attn_s = [os_.reshape(bs * ts, 1024).astype(BF16)]
            kv_a_s.append(kvs.reshape(bs, ts, 2, A_KV, 2 * HEAD_DIM))
            w_o_p = w_o_a[n]
            w_o_s = w_o_a[n]
        elif i % N_MIXERS == 1:
            kvn, q, kd, vd = _run_proj(xp, norm_mix[i], w_in_b[n], q_norm_b[n], k_norm_b[n],
                                       _layout_band_prompt((1,), B_KV), 512)
            op = _band_prompt(q.reshape(bp, 1, tp, 1024), kd.reshape(bp, 1, tp, B_KV * LANES),
                              vd.reshape(bp, 1, tp, B_KV * LANES), band_bias[1], sinks_b[n], B_KV, False, BF16)[0]
            attn_p, lse_p = [op.reshape(bp * tp, 1024)], None
            wb = min(B_WINDOW, tp)
            kv_b_p.append(kvn.reshape(bp, tp, -1)[:, tp - wb:].reshape(bp, wb, 2, B_KV, HEAD_DIM))
            qs, kvs = _run_proj(xs, norm_mix[i], w_in_b[n], q_norm_b[n], k_norm_b[n],
                                _layout_band_sample(1, B_KV), 512)
            sink_rows = jnp.repeat(sinks_b[n], ts).reshape(1, N_HEADS * ts, 1)
            os_, nst = _decode(qs.reshape(bs, ts, -1), kvs.reshape(bs, ts, -1), [state_kv_b], n,
                               [b_bias_past], [b_bias_new], sink_rows, B_KV)
            attn_s = [os_.reshape(bs * ts, N_HEADS * LANES).astype(BF16)]
            kv_b_s.append(nst[0])
            w_o_p = w_o_b[n]
            w_o_s = _expand_wo_sample(w_o_b[n], B_KV)
        else:
            c_dils = tuple(d for _, d in C_PAIRS)
            outs = _run_proj(xp, norm_mix[i], w_in_c[n], q_norm_c[n], k_norm_c[n],
                             _layout_band_prompt(c_dils, C_KV), 256, seq_len=tp)
            kvn = outs[0]
            attn_p, lse_p = [], []
            for g, (w, d) in enumerate(C_PAIRS):
                q, kd, vd = (a.reshape(bp, d, tp // d, a.shape[-1]) for a in outs[1 + 3 * g:4 + 3 * g])
                o_g, lse_g = _band_prompt(q, kd, vd, band_bias[d], None, C_KV, True, F32)
                attn_p.append(o_g)
                lse_p.append(lse_g)
            kvn = kvn.reshape(bp, tp, -1)
            gw = 2 * C_KV * HEAD_DIM
            for g, (w, d) in enumerate(C_PAIRS):
                wc = min(w, tp)
                kv_c_p[g].append(kvn[:, tp - wc:, g * gw:(g + 1) * gw].reshape(bp, wc, 2, C_KV, HEAD_DIM))
            qs, kvs = _run_proj(xs, norm_mix[i], w_in_c[n], q_norm_c[n], k_norm_c[n],
                                _layout_band_sample(3, C_KV), 256)
            os_, nst = _decode(qs.reshape(bs, ts, -1), kvs.reshape(bs, ts, -1), list(state_c), n,
                               [cb[0] for cb in c_bias], [cb[1] for cb in c_bias], None, C_KV)
            attn_s = [os_.reshape(bs * ts, N_HEADS * LANES).astype(BF16)]
            for g in range(3):
                kv_c_s[g].append(nst[g])
            w_o_p = w_o_c[n]
            w_o_s = _expand_wo_sample(w_o_c[n], C_KV)

        xp, gate_tail = _ffn(xp, attn_p, lse_p, w_o_p, norm_ffn[i], w_ffn_in[i], conv_ffn_w[i],
                             conv_ffn_b[i], w_ffn_out[i], None, tp)
        tiles = tp // 512
        conv_p.append(gate_tail.reshape(bp, tiles, SUBLANES, D_FF)[:, -1, SUBLANES - (CONV_W - 1):])
        hist = state_conv_ffn[i]
        zero = jnp.zeros((bs, ts - 1, D_FF), F32)
        fix1 = jnp.concatenate([hist[:, 1:2], zero], axis=1).reshape(bs * ts, D_FF)
        fix2 = jnp.concatenate([hist[:, 0:1], hist[:, 1:2], zero[:, 1:]], axis=1).reshape(bs * ts, D_FF)
        xs, gate_s = _ffn(xs, attn_s, None, w_o_s, norm_ffn[i], w_ffn_in[i], conv_ffn_w[i],
                          conv_ffn_b[i], w_ffn_out[i], (fix1, fix2), ts, row_tile=256)
        conv_s.append(gate_s.reshape(bs, ts, D_FF)[:, ts - (CONV_W - 1):])

    return (xp.reshape(bp, tp, D_MODEL), xs.reshape(bs, ts, D_MODEL),
            jnp.stack(kv_a_p), jnp.stack(kv_a_s),
            jnp.stack(kv_b_p), jnp.stack(kv_b_s),
            jnp.stack(kv_c_p[0]), jnp.stack(kv_c_s[0]),
            jnp.stack(kv_c_p[1]), jnp.stack(kv_c_s[1]),
            jnp.stack(kv_c_p[2]), jnp.stack(kv_c_s[2]),
            jnp.stack(conv_p), jnp.stack(conv_s))
```
